```python
import math
import jax, jax.numpy as jnp
from jax import lax
import numpy as np


D_MODEL = 1024
BATCH = 4
SEQ = 4096
DEPTH = 2
DEC_BATCH = 128
DEC_SEQ = 1
PAST_LEN = 2048
PAGE_SIZE = 128

F32 = jnp.float32

GROUP_CH = 16
N_GROUPS = D_MODEL // GROUP_CH
STATE = 64
DT_MIN = 1e-3
DT_MAX = 1e-1

N_HEADS = 16
HEAD_DIM = D_MODEL // N_HEADS
N_KV = 4
GQ = N_HEADS // N_KV
CMP_BLOCK = 32
CMP_STRIDE = 16
CMP_HIDDEN = 2 * HEAD_DIM
SEL_BLOCK = 64
TOP_N = 16
WINDOW = 512
QBLOCK = 128
Q_COLS = N_HEADS * HEAD_DIM
KV_COLS = 6 * N_KV * HEAD_DIM
GATE_COLS = 3 * N_HEADS
NSA_COLS = Q_COLS + KV_COLS + GATE_COLS

N_BUCKETS = 32
MAX_DISTANCE = 128

D_FF = 2816
N_EXPERTS = 8
TOP_K = 2
EXPERT_FF = 1408

EPS = 1e-6
NEG = -1e30
FORCE = 1e4

kernel_name = 's5_nsa_hybrid_decode_step'


def rmsnorm(x, g):
    x32 = x.astype(F32)
    y = x32 * lax.rsqrt(jnp.mean(x32 * x32, axis=-1, keepdims=True) + EPS) * g.astype(F32)
    return y.astype(x.dtype)


def swiglu(h, w_in, w_out):
    a, b = jnp.split(h @ w_in, 2, axis=-1)
    return (jax.nn.silu(a) * b) @ w_out


def moe_swiglu(h, router, w_in, w_out):
    logits = jnp.einsum('bld,de->ble', h.astype(F32), router.astype(F32))
    top_v, top_i = lax.top_k(logits, TOP_K)
    w = jax.nn.softmax(top_v, axis=-1)
    gate = jnp.sum(jax.nn.one_hot(top_i, N_EXPERTS, dtype=F32) * w[..., None], axis=-2)
    out = jnp.zeros(h.shape, F32)
    for e in range(N_EXPERTS):
        out = out + gate[..., e:e + 1] * swiglu(h, w_in[e], w_out[e])
    return out.astype(h.dtype)


def complex_affine_combine(e1, e2):
    a1r, a1i, b1r, b1i = e1
    a2r, a2i, b2r, b2i = e2
    return (a2r * a1r - a2i * a1i,
            a2r * a1i + a2i * a1r,
            a2r * b1r - a2i * b1i + b2r,
            a2r * b1i + a2i * b1r + b2i)


def s5_ssm(u, h0_re, h0_im, lam_re, lam_im, log_dt, b_re, b_im, c_re, c_im, d_skip):
    bsz, length, _ = u.shape
    u32 = u.astype(F32)
    ug = u32.reshape(bsz, length, N_GROUPS, GROUP_CH)
    dt = jnp.exp(log_dt.astype(F32))[:, None]
    lr = lam_re.astype(F32)
    li = lam_im.astype(F32)
    mag = jnp.exp(lr * dt)
    ab_re = mag * jnp.cos(li * dt)
    ab_im = mag * jnp.sin(li * dt)
    den = lr * lr + li * li
    nr = ab_re - 1.0
    f_re = (nr * lr + ab_im * li) / den
    f_im = (ab_im * lr - nr * li) / den
    br = b_re.astype(F32)
    bim = b_im.astype(F32)
    bb_re = f_re[..., None] * br - f_im[..., None] * bim
    bb_im = f_re[..., None] * bim + f_im[..., None] * br
    x_re = jnp.einsum('blgc,gpc->blgp', ug, bb_re)
    x_im = jnp.einsum('blgc,gpc->blgp', ug, bb_im)
    if h0_re is not None:
        hr = h0_re.astype(F32)
        hi = h0_im.astype(F32)
        x_re = x_re.at[:, 0].add(ab_re * hr - ab_im * hi)
        x_im = x_im.at[:, 0].add(ab_re * hi + ab_im * hr)
    a_re = jnp.broadcast_to(ab_re, (1, length) + ab_re.shape)
    a_im = jnp.broadcast_to(ab_im, (1, length) + ab_im.shape)
    _, _, h_re, h_im = lax.associative_scan(complex_affine_combine, (a_re, a_im, x_re, x_im), axis=1)
    y = (jnp.einsum('blgp,gcp->blgc', h_re, c_re.astype(F32))
         - jnp.einsum('blgp,gcp->blgc', h_im, c_im.astype(F32)))
    y = y.reshape(bsz, length, D_MODEL) + d_skip.astype(F32) * u32
    return y, h_re[:, -1], h_im[:, -1]


def s5_mixer(h, h0_re, h0_im, lam_re, lam_im, log_dt, b_re, b_im, c_re, c_im, d_skip, w_glu):
    y, hr, hi = s5_ssm(h, h0_re, h0_im, lam_re, lam_im, log_dt, b_re, b_im, c_re, c_im, d_skip)
    g = jax.nn.gelu(y)
    a, b = jnp.split(g @ w_glu, 2, axis=-1)
    return (a * jax.nn.sigmoid(b)).astype(h.dtype), hr, hi


def t5_bucket(dist):
    n = jnp.maximum(dist, 0)
    exact = N_BUCKETS // 2
    logpart = exact + (jnp.log(jnp.maximum(n, 1).astype(F32) / exact)
                       / math.log(MAX_DISTANCE / exact) * (N_BUCKETS - exact)).astype(jnp.int32)
    return jnp.where(n < exact, n, jnp.minimum(logpart, N_BUCKETS - 1))


def sel_cover_weights(nc, nb):
    ratio = SEL_BLOCK // CMP_STRIDE
    span = CMP_BLOCK // CMP_STRIDE
    off = (jnp.arange(ratio)[:, None] - jnp.arange(span)[None, :]).reshape(-1)
    target = jnp.arange(nb)[:, None] * ratio + off[None, :]
    return jnp.sum(jnp.arange(nc)[:, None, None] == target[None], axis=-1).astype(F32)


def nsa_project(h, w_in):
    bsz, length, _ = h.shape
    p = h @ w_in
    q = p[..., :Q_COLS].reshape(bsz, length, N_HEADS, HEAD_DIM)
    kv = p[..., Q_COLS:Q_COLS + KV_COLS].reshape(bsz, length, 6, N_KV, HEAD_DIM)
    gate = jax.nn.sigmoid(p[..., Q_COLS + KV_COLS:].astype(F32)).reshape(bsz, length, 3, N_HEADS)
    return q, kv, gate


def nsa_compress(rows, pe, w1, w2):
    length = rows.shape[1]
    nc = (length - CMP_BLOCK) // CMP_STRIDE + 1
    idx = jnp.arange(nc)[:, None] * CMP_STRIDE + jnp.arange(CMP_BLOCK)[None, :]
    win = rows[:, idx] + pe[None, None, :, None, :]
    hid = jax.nn.gelu(jnp.einsum('bnlgd,lde->bnge', win, w1))
    return jnp.einsum('bnge,ef->bngf', hid, w2)


def sel_blocks(rows):
    bsz, length = rows.shape[:2]
    nb = -(-length // SEL_BLOCK)
    rows = jnp.pad(rows, ((0, 0), (0, nb * SEL_BLOCK - length), (0, 0), (0, 0)))
    return rows.reshape(bsz, nb, SEL_BLOCK, N_KV, HEAD_DIM).transpose(0, 3, 1, 2, 4)


def nsa_attend(q, qpos, kc, vc, kb, vb, kw, vw, kwpos, rel_bias):
    bsz, nq = q.shape[:2]
    nc, nb = kc.shape[1], kb.shape[2]
    qg = q.astype(F32).reshape(bsz, nq, N_KV, GQ, HEAD_DIM) * (HEAD_DIM ** -0.5)
    tab = rel_bias.astype(F32).reshape(N_BUCKETS, N_KV, GQ)

    cend = jnp.arange(nc) * CMP_STRIDE + (CMP_BLOCK - 1)
    dist_c = qpos[:, None] - cend[None, :]
    mask_c = (dist_c >= 0)[None, :, None, None, :]
    s_c = (jnp.einsum('bqgrd,bngd->bqgrn', qg, kc)
           + jnp.transpose(tab[t5_bucket(dist_c)], (0, 2, 3, 1))[None])
    p_c = jax.nn.softmax(jnp.where(mask_c, s_c, NEG), axis=-1) * mask_c
    o_c = jnp.einsum('bqgrn,bngd->bqgrd', p_c, vc)

    imp = jnp.einsum('bqgrn,nj->bqgj', p_c, sel_cover_weights(nc, nb))
    j = jnp.arange(nb)
    jt = (qpos // SEL_BLOCK)[:, None]
    forced = ((j == 0) | (j == jt) | (j == jt - 1))[None, :, None, :]
    vis = (j * SEL_BLOCK <= qpos[:, None])[None, :, None, :]
    score = jnp.where(vis, jnp.where(forced, FORCE, imp), NEG)
    _, idx = lax.top_k(score, min(TOP_N, nb))
    bi = jnp.arange(bsz)[:, None, None, None]
    gi = jnp.arange(N_KV)[None, None, :, None]
    ks = kb[bi, gi, idx]
    vs = vb[bi, gi, idx]
    kpos = idx[..., None] * SEL_BLOCK + jnp.arange(SEL_BLOCK)
    dist_s = qpos[None, :, None, None, None] - kpos
    mask_s = (dist_s >= 0)[:, :, :, None]
    bias_s = jnp.moveaxis(tab[t5_bucket(dist_s), gi[..., None]], -1, 3)
    s_s = jnp.where(mask_s, jnp.einsum('bqgrd,bqgkld->bqgrkl', qg, ks) + bias_s, NEG)
    shp = s_s.shape
    p_s = jax.nn.softmax(s_s.reshape(shp[:4] + (-1,)), axis=-1).reshape(shp) * mask_s
    o_s = jnp.einsum('bqgrkl,bqgkld->bqgrd', p_s, vs)

    dist_w = qpos[:, None] - kwpos[None, :]
    mask_w = ((dist_w >= 0) & (dist_w <= WINDOW) & (kwpos[None, :] >= 0))[None, :, None, None, :]
    s_w = (jnp.einsum('bqgrd,bkgd->bqgrk', qg, kw)
           + jnp.transpose(tab[t5_bucket(dist_w)], (0, 2, 3, 1))[None])
    p_w = jax.nn.softmax(jnp.where(mask_w, s_w, NEG), axis=-1) * mask_w
    o_w = jnp.einsum('bqgrk,bkgd->bqgrd', p_w, vw)

    def heads(o):
        return o.reshape(bsz, nq, N_HEADS, HEAD_DIM)
    return heads(o_c), heads(o_s), heads(o_w)


def nsa_merge(o_c, o_s, o_w, gate, w_o, dtype):
    o = gate[:, :, 0, :, None] * o_c + gate[:, :, 1, :, None] * o_s + gate[:, :, 2, :, None] * o_w
    bsz, length = o.shape[:2]
    return (o.reshape(bsz, length, D_MODEL) @ w_o).astype(dtype)


def nsa_prompt(h, rel_bias, w_in, phi_pe, phi_w1, phi_w2, w_o):
    bsz, length, _ = h.shape
    q, kv, gate = nsa_project(h, w_in)
    kc = nsa_compress(kv[:, :, 0], phi_pe[0], phi_w1[0], phi_w2[0])
    vc = nsa_compress(kv[:, :, 1], phi_pe[1], phi_w1[1], phi_w2[1])
    kb = sel_blocks(kv[:, :, 2])
    vb = sel_blocks(kv[:, :, 3])
    pad = ((0, 0), (WINDOW, 0), (0, 0), (0, 0))
    kwp = jnp.pad(kv[:, :, 4], pad)
    vwp = jnp.pad(kv[:, :, 5], pad)
    nqb = length // QBLOCK

    def block(args):
        qb, s = args
        kw = lax.dynamic_slice_in_dim(kwp, s, WINDOW + QBLOCK, axis=1)
        vw = lax.dynamic_slice_in_dim(vwp, s, WINDOW + QBLOCK, axis=1)
        kwpos = s - WINDOW + jnp.arange(WINDOW + QBLOCK)
        qpos = s + jnp.arange(QBLOCK)
        return nsa_attend(qb, qpos, kc, vc, kb, vb, kw, vw, kwpos, rel_bias)

    qbs = q.reshape(bsz, nqb, QBLOCK, N_HEADS, HEAD_DIM).swapaxes(0, 1)
    starts = jnp.arange(nqb, dtype=jnp.int32) * QBLOCK
    o_c, o_s, o_w = lax.map(block, (qbs, starts))

    def unblock(o):
        return o.swapaxes(0, 1).reshape(bsz, length, N_HEADS, HEAD_DIM)
    y = nsa_merge(unblock(o_c), unblock(o_s), unblock(o_w), gate, w_o, h.dtype)
    wp = min(WINDOW, length)
    return y, kv[:, :, :4], kv[:, length - wp:, 4:]


def nsa_sample(h, cache_kv, cache_win, page_table, rel_bias, w_in, phi_pe, phi_w1, phi_w2, w_o):
    bsz, length, _ = h.shape
    q, kv, gate = nsa_project(h, w_in)
    past = cache_kv[page_table]
    past_len = past.shape[1] * past.shape[2]
    full = jnp.concatenate([past.reshape(bsz, past_len, 4, N_KV, HEAD_DIM),
                            kv[:, :, :4].astype(past.dtype)], axis=1)
    kc = nsa_compress(full[:, :, 0], phi_pe[0], phi_w1[0], phi_w2[0])
    vc = nsa_compress(full[:, :, 1], phi_pe[1], phi_w1[1], phi_w2[1])
    kb = sel_blocks(full[:, :, 2])
    vb = sel_blocks(full[:, :, 3])
    wb = cache_win.shape[1]
    win = jnp.concatenate([cache_win, kv[:, :, 4:].astype(cache_win.dtype)], axis=1)
    kwpos = jnp.concatenate([past_len - wb + jnp.arange(wb), past_len + jnp.arange(length)])
    qpos = past_len + jnp.arange(length)
    o_c, o_s, o_w = nsa_attend(q, qpos, kc, vc, kb, vb, win[:, :, 0], win[:, :, 1], kwpos, rel_bias)
    y = nsa_merge(o_c, o_s, o_w, gate, w_o, h.dtype)
    return y, kv[:, :, :4], kv[:, :, 4:]


def setup_inputs(seed: int = 0) -> dict:
    key = jax.random.key(seed)
    k = jax.random.split(key, 30)

    def nrm(kk, shape, scale=1.0):
        return jax.random.normal(kk, shape, F32) * scale

    n_pages = PAST_LEN // PAGE_SIZE
    used = DEC_BATCH * n_pages
    n_pool = used + max(1, used // 4)
    wb = min(WINDOW, PAST_LEN)
    page_table = jax.random.permutation(k[6], n_pool)[:used].reshape(DEC_BATCH, n_pages).astype(jnp.int32)
    n_idx = jnp.arange(STATE, dtype=F32)
    return {
        'x_prompt': nrm(k[0], (BATCH, SEQ, D_MODEL)),
        'x_sample': nrm(k[1], (DEC_BATCH, DEC_SEQ, D_MODEL)),
        'state_s5_re': nrm(k[2], (DEC_BATCH, N_GROUPS, STATE), 0.1),
        'state_s5_im': nrm(k[3], (DEC_BATCH, N_GROUPS, STATE), 0.1),
        'cache_kv': nrm(k[4], (n_pool, PAGE_SIZE, 4, N_KV, HEAD_DIM)),
        'cache_win': nrm(k[5], (DEC_BATCH, wb, 2, N_KV, HEAD_DIM)),
        'page_table': page_table,
        'rel_bias': nrm(k[7], (N_BUCKETS, N_HEADS), 0.1),
        'norm_mix': 1.0 + nrm(k[8], (DEPTH, D_MODEL), 0.01),
        'norm_ffn': 1.0 + nrm(k[9], (DEPTH, D_MODEL), 0.01),
        'norm_final': 1.0 + nrm(k[10], (D_MODEL,), 0.01),
        's5_lam_re': -0.5 + nrm(k[11], (N_GROUPS, STATE), 0.01),
        's5_lam_im': math.pi * n_idx[None, :] + nrm(k[12], (N_GROUPS, STATE), 0.01),
        's5_log_dt': jax.random.uniform(k[13], (N_GROUPS,), F32, math.log(DT_MIN), math.log(DT_MAX)),
        's5_b_re': nrm(k[14], (N_GROUPS, STATE, GROUP_CH), (2 * GROUP_CH) ** -0.5),
        's5_b_im': nrm(k[15], (N_GROUPS, STATE, GROUP_CH), (2 * GROUP_CH) ** -0.5),
        's5_c_re': nrm(k[16], (N_GROUPS, GROUP_CH, STATE), STATE ** -0.5),
        's5_c_im': nrm(k[17], (N_GROUPS, GROUP_CH, STATE), STATE ** -0.5),
        's5_d': nrm(k[18], (D_MODEL,)),
        's5_w_glu': nrm(k[19], (D_MODEL, 2 * D_MODEL), D_MODEL ** -0.5),
        'ffn_w_in': nrm(k[20], (D_MODEL, 2 * D_FF), D_MODEL ** -0.5),
        'ffn_w_out': nrm(k[21], (D_FF, D_MODEL), D_FF ** -0.5),
        'nsa_w_in': nrm(k[22], (D_MODEL, NSA_COLS), D_MODEL ** -0.5),
        'nsa_phi_pe': nrm(k[23], (2, CMP_BLOCK, HEAD_DIM), 0.1),
        'nsa_phi_w1': nrm(k[24], (2, CMP_BLOCK, HEAD_DIM, CMP_HIDDEN), (CMP_BLOCK * HEAD_DIM) ** -0.5),
        'nsa_phi_w2': nrm(k[25], (2, CMP_HIDDEN, HEAD_DIM), CMP_HIDDEN ** -0.5),
        'nsa_w_o': nrm(k[26], (D_MODEL, D_MODEL), D_MODEL ** -0.5),
        'moe_router': nrm(k[27], (D_MODEL, N_EXPERTS), D_MODEL ** -0.5),
        'moe_w_in': nrm(k[28], (N_EXPERTS, D_MODEL, 2 * EXPERT_FF), D_MODEL ** -0.5),
        'moe_w_out': nrm(k[29], (N_EXPERTS, EXPERT_FF, D_MODEL), EXPERT_FF ** -0.5),
    }


def reference(x_prompt, x_sample, state_s5_re, state_s5_im, cache_kv, cache_win, page_table,
              rel_bias, norm_mix, norm_ffn, norm_final, s5_lam_re, s5_lam_im, s5_log_dt,
              s5_b_re, s5_b_im, s5_c_re, s5_c_im, s5_d, s5_w_glu, ffn_w_in, ffn_w_out,
              nsa_w_in, nsa_phi_pe, nsa_phi_w1, nsa_phi_w2, nsa_w_o,
              moe_router, moe_w_in, moe_w_out):
    s5p = (s5_lam_re, s5_lam_im, s5_log_dt, s5_b_re, s5_b_im, s5_c_re, s5_c_im, s5_d, s5_w_glu)
    nsap = (nsa_w_in, nsa_phi_pe, nsa_phi_w1, nsa_phi_w2, nsa_w_o)
    yp, ys = x_prompt, x_sample
    for i in range(DEPTH):
        hp = rmsnorm(yp, norm_mix[i])
        hs = rmsnorm(ys, norm_mix[i])
        if i % 2 == 0:
            op, s5_re_p, s5_im_p = s5_mixer(hp, None, None, *s5p)
            os_, s5_re_s, s5_im_s = s5_mixer(hs, state_s5_re, state_s5_im, *s5p)
        else:
            op, kv_p, win_p = nsa_prompt(hp, rel_bias, *nsap)
            os_, kv_s, win_s = nsa_sample(hs, cache_kv, cache_win, page_table, rel_bias, *nsap)
        yp = yp + op
        ys = ys + os_
        hp = rmsnorm(yp, norm_ffn[i])
        hs = rmsnorm(ys, norm_ffn[i])
        if i % 2 == 0:
            yp = yp + swiglu(hp, ffn_w_in, ffn_w_out).astype(yp.dtype)
            ys = ys + swiglu(hs, ffn_w_in, ffn_w_out).astype(ys.dtype)
        else:
            yp = yp + moe_swiglu(hp, moe_router, moe_w_in, moe_w_out)
            ys = ys + moe_swiglu(hs, moe_router, moe_w_in, moe_w_out)
    y_prompt = rmsnorm(yp, norm_final)
    y_sample = rmsnorm(ys, norm_final)
    return (y_prompt, y_sample, s5_re_p, s5_im_p, kv_p, win_p, s5_re_s, s5_im_s, kv_s, win_s)
```

```python
import functools
import math

import numpy as np
import jax
import jax.numpy as jnp
from jax import lax
from jax.experimental import pallas as pl
from jax.experimental.pallas import tpu as pltpu

F32 = jnp.float32
BF16 = jnp.bfloat16
HIGHEST = lax.Precision.HIGHEST

D_MODEL = 1024
GROUP_CH = 16
N_GROUPS = D_MODEL // GROUP_CH
STATE = 64
N_HEADS = 16
HEAD_DIM = 64
N_KV = 4
GQ = N_HEADS // N_KV
CMP_BLOCK = 32
CMP_STRIDE = 16
CMP_HIDDEN = 2 * HEAD_DIM
SEL_BLOCK = 64
TOP_N = 16
WINDOW = 512
Q_COLS = N_HEADS * HEAD_DIM
KV_COLS = 6 * N_KV * HEAD_DIM
GATE_COLS = 3 * N_HEADS
N_BUCKETS = 32
MAX_DISTANCE = 128
D_FF = 2816
N_EXPERTS = 8
EXPERT_FF = 1408
EPS = 1e-6
NEG = -1e30
FORCE = 1e4

LANES = 128
S5_T = 8
S5_SG = LANES // GROUP_CH
S5_NSG = N_GROUPS // S5_SG
S5_SL = S5_SG * STATE
QT = 128


def _cp(sem, vmem_mb):
    return pltpu.CompilerParams(dimension_semantics=sem, vmem_limit_bytes=vmem_mb << 20)


def _cmul(ar, ai, br, bi):
    return ar * br - ai * bi, ar * bi + ai * br


def _s5_prep_kernel(lr_r, li_r, ldt_r, br_r, bi_r, lr_c, li_c, ldt_c, cr_c, ci_c,
                    wk_ref, ws_ref, wv_ref, wb_ref, wc_ref, apow_ref, a1_ref):
    def disc(lr, li, ldt):
        dt = jnp.exp(ldt)
        mag = jnp.exp(lr * dt)
        abr = mag * jnp.cos(li * dt)
        abi = mag * jnp.sin(li * dt)
        den = lr * lr + li * li
        nr = abr - 1.0
        fr = (nr * lr + abi * li) / den
        fi = (abi * lr - nr * li) / den
        return abr, abi, fr, fi

    abr, abi, fr, fi = disc(lr_r[0], li_r[0], ldt_r[0])
    rg = lax.broadcasted_iota(jnp.int32, (LANES, S5_SL), 0) // GROUP_CH
    cg = lax.broadcasted_iota(jnp.int32, (LANES, S5_SL), 1) // STATE
    mask_r = (rg == cg).astype(F32)
    b_r, b_i = br_r[...], bi_r[...]
    bbr = (fr * b_r - fi * b_i) * mask_r
    bbi = (fr * b_i + fi * b_r) * mask_r
    abr_c, abi_c, _, _ = disc(lr_c[0], li_c[0], ldt_c[0])
    rg2 = lax.broadcasted_iota(jnp.int32, (S5_SL, LANES), 0) // STATE
    cg2 = lax.broadcasted_iota(jnp.int32, (S5_SL, LANES), 1) // GROUP_CH
    mask_c = (rg2 == cg2).astype(F32)
    c_r = cr_c[0] * mask_c
    c_i = ci_c[0] * mask_c

    wb_ref[0, :, :S5_SL] = bbr
    wb_ref[0, :, S5_SL:] = bbi
    wc_ref[0, :S5_SL, :] = c_r
    wc_ref[0, S5_SL:, :] = -c_i
    a1_ref[0, :, :S5_SL] = abr
    a1_ref[0, :, S5_SL:] = abi

    pr = jnp.ones_like(abr)
    pi = jnp.zeros_like(abr)
    pr_c = jnp.ones_like(abr_c)
    pi_c = jnp.zeros_like(abr_c)
    for tau in range(S5_T + 1):
        wv_ref[tau, 0, :S5_SL, :] = (c_r * pr_c - c_i * pi_c).astype(BF16)
        wv_ref[tau, 0, S5_SL:, :] = (-(c_r * pi_c + c_i * pr_c)).astype(BF16)
        if tau < S5_T:
            er, ei = _cmul(pr, pi, bbr, bbi)
            s = S5_T - 1 - tau
            ws_ref[s, 0, :, :S5_SL] = er.astype(BF16)
            ws_ref[s, 0, :, S5_SL:] = ei.astype(BF16)
            k = (jnp.dot(er, c_r, precision=HIGHEST, preferred_element_type=F32)
                 - jnp.dot(ei, c_i, precision=HIGHEST, preferred_element_type=F32))
            wk_ref[tau, 0] = k.astype(BF16)
        pr, pi = _cmul(pr, pi, abr, abi)
        pr_c, pi_c = _cmul(pr_c, pi_c, abr_c, abi_c)
    tr, ti = jnp.ones_like(abr), jnp.zeros_like(abr)
    for _ in range(S5_T):
        tr, ti = _cmul(tr, ti, abr, abi)
    qr, qi = jnp.ones_like(abr), jnp.zeros_like(abr)
    for i in range(S5_T + 1):
        apow_ref[0, i:i + 1, :S5_SL] = qr
        apow_ref[0, i:i + 1, S5_SL:] = qi
        qr, qi = _cmul(qr, qi, tr, ti)


def _s5_prep(lam_re, lam_im, log_dt, b_re, b_im, c_re, c_im):
    n = S5_NSG
    row = lambda a: a.reshape(n, 1, S5_SL)
    col = lambda a: a.reshape(n, S5_SL, 1)
    ldt = jnp.repeat(log_dt, STATE)
    bt = lambda b: jnp.tile(b.transpose(0, 2, 1).reshape(D_MODEL, STATE), (1, S5_SG))
    ct = lambda c: jnp.tile(c.reshape(n, LANES, STATE).transpose(0, 2, 1), (1, S5_SG, 1))
    rspec = pl.BlockSpec((1, 1, S5_SL), lambda s: (s, 0, 0))
    cspec = pl.BlockSpec((1, S5_SL, 1), lambda s: (s, 0, 0))
    t1 = S5_T + 1
    return pl.pallas_call(
        _s5_prep_kernel,
        grid=(n,),
        in_specs=[rspec, rspec, rspec,
                  pl.BlockSpec((LANES, S5_SL), lambda s: (s, 0)), pl.BlockSpec((LANES, S5_SL), lambda s: (s, 0)),
                  cspec, cspec, cspec,
                  pl.BlockSpec((1, S5_SL, LANES), lambda s: (s, 0, 0)), pl.BlockSpec((1, S5_SL, LANES), lambda s: (s, 0, 0))],
        out_specs=[pl.BlockSpec((S5_T, 1, LANES, LANES), lambda s: (0, s, 0, 0)),
                   pl.BlockSpec((S5_T, 1, LANES, 2 * S5_SL), lambda s: (0, s, 0, 0)),
                   pl.BlockSpec((t1, 1, 2 * S5_SL, LANES), lambda s: (0, s, 0, 0)),
                   pl.BlockSpec((1, LANES, 2 * S5_SL), lambda s: (s, 0, 0)),
                   pl.BlockSpec((1, 2 * S5_SL, LANES), lambda s: (s, 0, 0)),
                   pl.BlockSpec((1, t1 + 7, 2 * S5_SL), lambda s: (s, 0, 0)),
                   pl.BlockSpec((1, 1, 2 * S5_SL), lambda s: (s, 0, 0))],
        out_shape=[jax.ShapeDtypeStruct((S5_T, n, LANES, LANES), BF16),
                   jax.ShapeDtypeStruct((S5_T, n, LANES, 2 * S5_SL), BF16),
                   jax.ShapeDtypeStruct((t1, n, 2 * S5_SL, LANES), BF16),
                   jax.ShapeDtypeStruct((n, LANES, 2 * S5_SL), F32),
                   jax.ShapeDtypeStruct((n, 2 * S5_SL, LANES), F32),
                   jax.ShapeDtypeStruct((n, t1 + 7, 2 * S5_SL), F32),
                   jax.ShapeDtypeStruct((n, 1, 2 * S5_SL), F32)],
        compiler_params=_cp(("arbitrary",), 48),
        name="s5_prep",
    )(row(lam_re), row(lam_im), row(ldt), bt(b_re), bt(b_im),
      col(lam_re), col(lam_im), col(ldt), ct(c_re), ct(c_im))


def _rms(x, g):
    return x * lax.rsqrt(jnp.mean(x * x, axis=-1, keepdims=True) + EPS) * g


def _rmsnorm_kernel(x_ref, g_ref, o_ref):
    o_ref[...] = _rms(x_ref[...], g_ref[...]).astype(o_ref.dtype)


def _rmsnorm(x, g, tm, dtype=F32):
    n = x.shape[0]
    return pl.pallas_call(
        _rmsnorm_kernel,
        grid=(n // tm,),
        in_specs=[pl.BlockSpec((tm, D_MODEL), lambda i: (i, 0)), pl.BlockSpec((1, D_MODEL), lambda i: (0, 0))],
        out_specs=pl.BlockSpec((tm, D_MODEL), lambda i: (i, 0)),
        out_shape=jax.ShapeDtypeStruct((n, D_MODEL), dtype),
        compiler_params=_cp(("parallel",), 32),
        name="rmsnorm",
    )(x, g.reshape(1, D_MODEL))


def _s5_prompt_kernel(u_ref, d_ref, wk_ref, ws_ref, wv_ref, apow_ref, g_ref, sre_ref, sim_ref,
                      ub_ref, s_ref, p_ref, gc_ref, hs_ref, y_ref):
    length = u_ref.shape[0]
    r = length // S5_T
    ng = r // 8

    acc = jnp.zeros((r, 2 * S5_SL), F32)
    for s in range(S5_T):
        us = u_ref[pl.ds(s, r, stride=S5_T), :].astype(BF16)
        ub_ref[s] = us
        acc = acc + jnp.dot(us, ws_ref[s, 0], preferred_element_type=F32)
    for c in range(2 * S5_SL // LANES):
        s_ref[c] = acc[:, c * LANES:(c + 1) * LANES]

    nsl = S5_SL // LANES
    for c in range(nsl):
        lr = slice(c * LANES, (c + 1) * LANES)
        li = slice(S5_SL + c * LANES, S5_SL + (c + 1) * LANES)

        def apw(i):
            return apow_ref[0, i:i + 1, lr], apow_ref[0, i:i + 1, li]

        a1r, a1i = apw(1)
        pr = s_ref[c, pl.ds(0, ng, stride=8), :]
        pi = s_ref[nsl + c, pl.ds(0, ng, stride=8), :]
        p_ref[0, c] = pr
        p_ref[0, nsl + c] = pi
        for i in range(1, 8):
            qr, qi = _cmul(a1r, a1i, pr, pi)
            pr = qr + s_ref[c, pl.ds(i, ng, stride=8), :]
            pi = qi + s_ref[nsl + c, pl.ds(i, ng, stride=8), :]
            p_ref[i, c] = pr
            p_ref[i, nsl + c] = pi

        a8r, a8i = apw(8)

        def carry(m, g):
            gr, gi = g
            gc_ref[c, pl.ds(m, 1), :] = gr
            gc_ref[nsl + c, pl.ds(m, 1), :] = gi
            nr, ni = _cmul(a8r, a8i, gr, gi)
            return (nr + p_ref[7, c, pl.ds(m, 1), :], ni + p_ref[7, nsl + c, pl.ds(m, 1), :])

        z = jnp.zeros((1, LANES), F32)
        er, ei = lax.fori_loop(0, ng, carry, (z, z))
        sre_ref[0, :, lr] = er
        sim_ref[0, :, lr] = ei

        gr = gc_ref[c]
        gi = gc_ref[nsl + c]
        hs_ref[c, pl.ds(0, ng, stride=8), :] = gr
        hs_ref[nsl + c, pl.ds(0, ng, stride=8), :] = gi
        for i in range(1, 8):
            air, aii = apw(i)
            hr, hi = _cmul(air, aii, gr, gi)
            hs_ref[c, pl.ds(i, ng, stride=8), :] = hr + p_ref[i - 1, c]
            hs_ref[nsl + c, pl.ds(i, ng, stride=8), :] = hi + p_ref[i - 1, nsl + c]

    hs = jnp.concatenate([hs_ref[c] for c in range(2 * S5_SL // LANES)], axis=1).astype(BF16)
    for t in range(S5_T):
        o = jnp.dot(hs, wv_ref[t + 1, 0], preferred_element_type=F32)
        for s in range(t + 1):
            o = o + jnp.dot(ub_ref[s], wk_ref[t - s, 0], preferred_element_type=F32)
        y_ref[pl.ds(t, r, stride=S5_T), :] = o
    y = y_ref[...] + d_ref[...] * u_ref[...]
    g_ref[...] = jax.nn.gelu(y).astype(g_ref.dtype)


def _s5_prompt(u, d_skip, wk, ws, wv, apow, batch, length):
    r = length // S5_T
    t1 = S5_T + 1
    return pl.pallas_call(
        _s5_prompt_kernel,
        grid=(S5_NSG, batch),
        in_specs=[pl.BlockSpec((length, LANES), lambda s, b: (b, s)),
                  pl.BlockSpec((1, LANES), lambda s, b: (0, s)),
                  pl.BlockSpec((S5_T, 1, LANES, LANES), lambda s, b: (0, s, 0, 0)),
                  pl.BlockSpec((S5_T, 1, LANES, 2 * S5_SL), lambda s, b: (0, s, 0, 0)),
                  pl.BlockSpec((t1, 1, 2 * S5_SL, LANES), lambda s, b: (0, s, 0, 0)),
                  pl.BlockSpec((1, t1 + 7, 2 * S5_SL), lambda s, b: (s, 0, 0))],
        out_specs=[pl.BlockSpec((length, LANES), lambda s, b: (b, s)),
                   pl.BlockSpec((1, 1, S5_SL), lambda s, b: (b, 0, s)),
                   pl.BlockSpec((1, 1, S5_SL), lambda s, b: (b, 0, s))],
        out_shape=[jax.ShapeDtypeStruct((batch * length, D_MODEL), BF16),
                   jax.ShapeDtypeStruct((batch, 1, N_GROUPS * STATE), F32),
                   jax.ShapeDtypeStruct((batch, 1, N_GROUPS * STATE), F32)],
        scratch_shapes=[pltpu.VMEM((S5_T, r, LANES), BF16),
                        pltpu.VMEM((2 * S5_SL // LANES, r, LANES), F32),
                        pltpu.VMEM((8, 2 * S5_SL // LANES, r // 8, LANES), F32),
                        pltpu.VMEM((2 * S5_SL // LANES, r // 8, LANES), F32),
                        pltpu.VMEM((2 * S5_SL // LANES, r, LANES), F32),
                        pltpu.VMEM((length, LANES), F32)],
        compiler_params=_cp(("arbitrary", "arbitrary"), 56),
        name="s5_prompt",
    )(u, d_skip.reshape(1, D_MODEL), wk, ws, wv, apow)


def _resident(shape):
    nd = len(shape)
    return pl.BlockSpec(shape, lambda *_: (0,) * nd, pipeline_mode=pl.Buffered(1))


def _rows(tm, width):
    return pl.BlockSpec((tm, width), lambda i: (i, 0))


def _glu_kernel(g_ref, x_ref, w_ref, gain_ref, y_ref, h_ref):
    p = jnp.dot(g_ref[...], w_ref[...], preferred_element_type=F32)
    y = x_ref[...] + p[:, :D_MODEL] * jax.nn.sigmoid(p[:, D_MODEL:])
    y_ref[...] = y
    h_ref[...] = _rms(y, gain_ref[...]).astype(h_ref.dtype)


def _glu(g, x, w_glu, gain, tm):
    n = g.shape[0]
    return pl.pallas_call(
        _glu_kernel,
        grid=(n // tm,),
        in_specs=[_rows(tm, D_MODEL), _rows(tm, D_MODEL), _resident((D_MODEL, 2 * D_MODEL)), _resident((1, D_MODEL))],
        out_specs=[_rows(tm, D_MODEL), _rows(tm, D_MODEL)],
        out_shape=[jax.ShapeDtypeStruct((n, D_MODEL), F32), jax.ShapeDtypeStruct((n, D_MODEL), BF16)],
        compiler_params=_cp(("parallel",), 48),
        name="s5_glu",
    )(g, x, w_glu, gain.reshape(1, D_MODEL))


def _chunks(total, size):
    return [(s, min(size, total - s)) for s in range(0, total, size)]


def _swiglu_acc(h, win_ref, wout_ref, ff, lead=()):
    acc = None
    for s, w in _chunks(ff, 256):
        a = jnp.dot(h, win_ref[lead + (slice(None), slice(s, s + w))], preferred_element_type=F32)
        b = jnp.dot(h, win_ref[lead + (slice(None), slice(ff + s, ff + s + w))], preferred_element_type=F32)
        act = (jax.nn.silu(a) * b).astype(BF16)
        z = jnp.dot(act, wout_ref[lead + (slice(s, s + w), slice(None))], preferred_element_type=F32)
        acc = z if acc is None else acc + z
    return acc


def _ffn_kernel(h_ref, y_ref, win_ref, wout_ref, gain_ref, yo_ref, ho_ref):
    y = y_ref[...] + _swiglu_acc(h_ref[...], win_ref, wout_ref, D_FF)
    yo_ref[...] = y
    ho_ref[...] = _rms(y, gain_ref[...]).astype(ho_ref.dtype)


def _ffn(h, y, w_in, w_out, gain, tm):
    n = h.shape[0]
    return pl.pallas_call(
        _ffn_kernel,
        grid=(n // tm,),
        in_specs=[_rows(tm, D_MODEL), _rows(tm, D_MODEL), _resident((D_MODEL, 2 * D_FF)), _resident((D_FF, D_MODEL)),
                  _resident((1, D_MODEL))],
        out_specs=[_rows(tm, D_MODEL), _rows(tm, D_MODEL)],
        out_shape=[jax.ShapeDtypeStruct((n, D_MODEL), F32), jax.ShapeDtypeStruct((n, D_MODEL), BF16)],
        compiler_params=_cp(("parallel",), 56),
        name="ffn_swiglu",
    )(h, y, w_in, w_out, gain.reshape(1, D_MODEL))


NSA_PAD = 2688
KV_OFF = Q_COLS


def _nsa_proj_kernel(h_ref, w_ref, q_ref, kvf_ref, winf_ref, kvb_ref, gate_ref):
    p = jnp.dot(h_ref[...], w_ref[...], preferred_element_type=F32)
    q_ref[...] = (p[:, :Q_COLS] * (HEAD_DIM ** -0.5)).astype(q_ref.dtype)
    kvf_ref[...] = p[:, KV_OFF:KV_OFF + 1024]
    winf_ref[...] = p[:, KV_OFF + 1024:KV_OFF + 1536]
    kvb_ref[...] = p[:, KV_OFF + 512:KV_OFF + 1536].astype(kvb_ref.dtype)
    gate_ref[...] = jax.nn.sigmoid(p[:, KV_OFF + 1536:])


def _nsa_proj(h, w_in_pad, tm):
    n = h.shape[0]
    return pl.pallas_call(
        _nsa_proj_kernel,
        grid=(n // tm,),
        in_specs=[_rows(tm, D_MODEL), _resident((D_MODEL, NSA_PAD))],
        out_specs=[_rows(tm, Q_COLS), _rows(tm, 1024), _rows(tm, 512), _rows(tm, 1024), _rows(tm, LANES)],
        out_shape=[jax.ShapeDtypeStruct((n, Q_COLS), BF16),
                   jax.ShapeDtypeStruct((n, 1024), F32),
                   jax.ShapeDtypeStruct((n, 512), F32),
                   jax.ShapeDtypeStruct((n, 1024), BF16),
                   jax.ShapeDtypeStruct((n, LANES), F32)],
        compiler_params=_cp(("parallel",), 48),
        name="nsa_proj",
    )(h, w_in_pad)


def _merge_kernel(oc_ref, os_ref, ow_ref, y_ref, wo_ref, gain_ref, router_ref, yo_ref, ho_ref, mg_ref):
    o = (oc_ref[...].astype(F32) + os_ref[...].astype(F32) + ow_ref[...].astype(F32)).astype(BF16)
    y = y_ref[...] + jnp.dot(o, wo_ref[...], preferred_element_type=F32)
    yo_ref[...] = y
    hf = _rms(y, gain_ref[...])
    ho_ref[...] = hf.astype(ho_ref.dtype)
    logits = jnp.dot(hf, router_ref[...], precision=HIGHEST, preferred_element_type=F32)
    lane = lax.broadcasted_iota(jnp.int32, logits.shape, 1)
    logits = jnp.where(lane < N_EXPERTS, logits, -jnp.inf)
    m1 = jnp.max(logits, axis=-1, keepdims=True)
    i1 = jnp.min(jnp.where(logits == m1, lane, LANES), axis=-1, keepdims=True)
    rest = jnp.where(lane == i1, -jnp.inf, logits)
    m2 = jnp.max(rest, axis=-1, keepdims=True)
    i2 = jnp.min(jnp.where(rest == m2, lane, LANES), axis=-1, keepdims=True)
    e2 = jnp.exp(m2 - m1)
    den = 1.0 + e2
    mg_ref[...] = jnp.where(lane == i1, 1.0 / den, 0.0) + jnp.where(lane == i2, e2 / den, 0.0)


def _merge(oc, os_, ow, y, w_o, gain, router_pad, tm):
    n = y.shape[0]
    return pl.pallas_call(
        _merge_kernel,
        grid=(n // tm,),
        in_specs=[_rows(tm, D_MODEL)] * 4 + [_resident((D_MODEL, D_MODEL)), _resident((1, D_MODEL)),
                                             _resident((D_MODEL, LANES))],
        out_specs=[_rows(tm, D_MODEL), _rows(tm, D_MODEL), _rows(tm, LANES)],
        out_shape=[jax.ShapeDtypeStruct((n, D_MODEL), F32), jax.ShapeDtypeStruct((n, D_MODEL), BF16),
                   jax.ShapeDtypeStruct((n, LANES), F32)],
        compiler_params=_cp(("parallel",), 48),
        name="nsa_merge_router",
    )(oc, os_, ow, y, w_o, gain.reshape(1, D_MODEL), router_pad)


def _moe_kernel(h_ref, y_ref, mg_ref, win_ref, wout_ref, gain_ref, o_ref, acc_ref):
    e = pl.program_id(1)

    @pl.when(e == 0)
    def _():
        acc_ref[...] = jnp.zeros_like(acc_ref)

    mg = mg_ref[...]
    lane = lax.broadcasted_iota(jnp.int32, mg.shape, 1)
    gcol = jnp.sum(jnp.where(lane == e, mg, 0.0), axis=-1, keepdims=True)
    acc_ref[...] += gcol * _swiglu_acc(h_ref[...], win_ref, wout_ref, EXPERT_FF, lead=(0,))

    @pl.when(e == N_EXPERTS - 1)
    def _():
        o_ref[...] = _rms(y_ref[...] + acc_ref[...], gain_ref[...])


def _moe(h, y, mg, w_in, w_out, gain, tm):
    n = h.shape[0]
    rows = lambda w: pl.BlockSpec((tm, w), lambda i, e: (i, 0))
    return pl.pallas_call(
        _moe_kernel,
        grid=(n // tm, N_EXPERTS),
        in_specs=[rows(D_MODEL), rows(D_MODEL), rows(LANES),
                  pl.BlockSpec((1, D_MODEL, 2 * EXPERT_FF), lambda i, e: (e, 0, 0)),
                  pl.BlockSpec((1, EXPERT_FF, D_MODEL), lambda i, e: (e, 0, 0)),
                  pl.BlockSpec((1, D_MODEL), lambda i, e: (0, 0))],
        out_specs=rows(D_MODEL),
        out_shape=jax.ShapeDtypeStruct((n, D_MODEL), F32),
        scratch_shapes=[pltpu.VMEM((tm, D_MODEL), F32)],
        compiler_params=_cp(("parallel", "arbitrary"), 56),
        name="moe_final_norm",
    )(h, y, mg, w_in, w_out, gain.reshape(1, D_MODEL))


def _t5_bucket_np(dist):
    n = np.maximum(np.asarray(dist, np.int64), 0)
    exact = N_BUCKETS // 2
    logpart = exact + np.floor(np.log(np.maximum(n, 1) / exact) / math.log(MAX_DISTANCE / exact)
                               * (N_BUCKETS - exact)).astype(np.int64)
    return np.where(n < exact, n, np.minimum(logpart, N_BUCKETS - 1)).astype(np.int32)


def _bias_kernel(rb_ref, idx_ref, o_ref):
    h = pl.program_id(0)
    idx = idx_ref[...]
    far = rb_ref[N_BUCKETS - 1, h]
    acc = jnp.zeros(idx.shape, F32)
    for b in range(N_BUCKETS - 1):
        acc = jnp.where(idx == b, rb_ref[b, h] - far, acc)
    o_ref[0] = acc


def _bias_tables(rel_bias, idx):
    m = idx.shape[0]
    return pl.pallas_call(
        _bias_kernel,
        grid=(N_HEADS,),
        in_specs=[pl.BlockSpec(memory_space=pltpu.SMEM), pl.BlockSpec((m, LANES), lambda h: (0, 0))],
        out_specs=pl.BlockSpec((1, m, LANES), lambda h: (h, 0, 0)),
        out_shape=jax.ShapeDtypeStruct((N_HEADS, m, LANES), F32),
        compiler_params=_cp(("arbitrary",), 32),
        name="t5_bias_tables",
    )(rel_bias, idx)


CMP_NEAR = 32


def _prompt_bias_idx():
    a = np.arange(QT)[:, None]
    b = np.arange(QT)[None, :]
    t0 = _t5_bucket_np(a - b)
    t1 = _t5_bucket_np(QT + a - b)
    m = np.arange(LANES)[None, :]
    gc = np.where(m < CMP_NEAR, _t5_bucket_np(a + (CMP_NEAR // 2) * CMP_STRIDE - (CMP_BLOCK - 1) - CMP_STRIDE * m),
                  N_BUCKETS - 1)
    return np.concatenate([t0, t1, gc], axis=0).astype(np.int32)


def _flatten_pairs(x_ref, nh):
    lane = lax.broadcasted_iota(jnp.int32, (nh, LANES), 1)
    ev, od = [], []
    for m in range(CMP_STRIDE // 2):
        sa = x_ref[pl.ds(2 * m, nh, stride=CMP_STRIDE), :]
        sb = x_ref[pl.ds(2 * m + 1, nh, stride=CMP_STRIDE), :]
        ev.append(jnp.where(lane < HEAD_DIM, sa, pltpu.roll(sb, HEAD_DIM, 1)))
        od.append(jnp.where(lane < HEAD_DIM, pltpu.roll(sa, HEAD_DIM, 1), sb))
    return jnp.concatenate(ev, axis=1), jnp.concatenate(od, axis=1)


def _compress_rows(xf, w1_ref, w1f_ref, pe_ref, w2_ref, slot):
    rows = xf.shape[0]
    pe = pe_ref[slot]
    bias = jnp.concatenate(
        [jnp.dot(pe[:, :1024], w1f_ref[slot, :, :CMP_HIDDEN], precision=HIGHEST, preferred_element_type=F32),
         jnp.dot(pe[:, 1024:], w1f_ref[slot, :, CMP_HIDDEN:], precision=HIGHEST, preferred_element_type=F32)],
        axis=1)[0:1]
    pq = jnp.dot(xf, w1_ref[slot], preferred_element_type=F32) + bias
    hid = pq[:, :CMP_HIDDEN] + pltpu.roll(pq[:, CMP_HIDDEN:], rows - 1, 0)
    return jnp.dot(jax.nn.gelu(hid).astype(BF16), w2_ref[slot], preferred_element_type=F32)


def _compress_prompt_kernel(x0, x1, x2, x3, w1_ref, w1f_ref, pe_ref, w2_ref, o_ref):
    nh = x0.shape[0] // CMP_STRIDE
    xs = (x0, x1, x2, x3)
    for slot in range(2):
        parts = []
        for c in range(2):
            ev, od = _flatten_pairs(xs[2 * slot + c], nh)
            parts += [ev, od]
        xf = jnp.concatenate(parts, axis=0).astype(BF16)
        o_ref[0, slot] = _compress_rows(xf, w1_ref, w1f_ref, pe_ref, w2_ref, slot).astype(o_ref.dtype)


def _compress_weights(phi_pe, phi_w1, phi_w2):
    half = CMP_BLOCK // 2
    w1 = jnp.concatenate([phi_w1[:, :half].reshape(2, half * HEAD_DIM, CMP_HIDDEN),
                          phi_w1[:, half:].reshape(2, half * HEAD_DIM, CMP_HIDDEN)], axis=2)
    pe = jnp.broadcast_to(phi_pe.reshape(2, 1, CMP_BLOCK * HEAD_DIM), (2, 8, CMP_BLOCK * HEAD_DIM))
    return w1.astype(BF16), w1, pe, phi_w2.astype(BF16)


def _compress_prompt(kvf, cw, batch, length):
    w1b, w1f, pe, w2b = cw
    nh = length // CMP_STRIDE
    xspec = lambda c: pl.BlockSpec((length, LANES), lambda b: (b, c))
    return pl.pallas_call(
        _compress_prompt_kernel,
        grid=(batch,),
        in_specs=[xspec(0), xspec(1), xspec(2), xspec(3), _resident(w1b.shape), _resident(w1f.shape),
                  _resident(pe.shape), _resident(w2b.shape)],
        out_specs=pl.BlockSpec((1, 2, N_KV * nh, HEAD_DIM), lambda b: (b, 0, 0, 0)),
        out_shape=jax.ShapeDtypeStruct((batch, 2, N_KV * nh, HEAD_DIM), BF16),
        compiler_params=_cp(("parallel",), 48),
        name="nsa_compress_prompt",
    )(kvf, kvf, kvf, kvf, w1b, w1f, pe, w2b)


_NT = (((1,), (1,)), ((), ()))


def _cover_t_np(nc_pad, nb_pad):
    ratio = SEL_BLOCK // CMP_STRIDE
    span = CMP_BLOCK // CMP_STRIDE
    off = (np.arange(ratio)[:, None] - np.arange(span)[None, :]).reshape(-1)
    target = np.arange(nb_pad)[:, None] * ratio + off[None, :]
    cov = np.sum(np.arange(nc_pad)[:, None, None] == target[None], axis=-1)
    return np.ascontiguousarray(cov.T).astype(np.float32)


def _rank_select(score, jidx, nsel):
    nb = score.shape[0]
    cnt = jnp.zeros(score.shape, F32)
    for i in range(nb):
        ri = score[i:i + 1, :]
        cnt = cnt + jnp.where((ri > score) | ((ri == score) & (jidx > i)), 1.0, 0.0)
    return cnt < nsel


def _cmp_attn_kernel(q_ref, kcv_ref, gate_ref, ghi_ref, glo_ref, covt_ref, o_ref, sel_ref, bias_ref, selt_ref):
    i = pl.program_id(1)
    q0 = i * QT
    nc = kcv_ref.shape[2] // N_KV
    nb = covt_ref.shape[0]
    base = q0 // CMP_STRIDE - CMP_NEAR // 2
    rr = lax.broadcasted_iota(jnp.int32, (LANES, nc), 0)
    cc = lax.broadcasted_iota(jnp.int32, (LANES, nc), 1)
    place = ((cc == base + rr) & (rr < CMP_NEAR)).astype(BF16)
    bias_ref[...] = (jnp.dot(ghi_ref[...], place, preferred_element_type=F32)
                     + jnp.dot(glo_ref[...], place, preferred_element_type=F32))
    t = q0 + lax.broadcasted_iota(jnp.int32, (QT, nc), 0)
    cend = lax.broadcasted_iota(jnp.int32, (QT, nc), 1) * CMP_STRIDE + (CMP_BLOCK - 1)
    vis = t >= cend
    visf = vis.astype(F32)
    madd = jnp.where(vis, 0.0, NEG)
    gate = gate_ref[...]
    jidx = lax.broadcasted_iota(jnp.int32, (nb, QT), 0)
    tq = q0 + lax.broadcasted_iota(jnp.int32, (nb, QT), 1)
    jt = tq // SEL_BLOCK
    forced = (jidx == 0) | (jidx == jt) | (jidx == jt - 1)
    visb = jidx * SEL_BLOCK <= tq
    for g in range(N_KV):
        kc = kcv_ref[0, 0, g * nc:(g + 1) * nc, :]
        vc = kcv_ref[0, 1, g * nc:(g + 1) * nc, :]
        psum = jnp.zeros((QT, nc), F32)
        for r in range(GQ):
            h = g * GQ + r
            qh = q_ref[:, h * HEAD_DIM:(h + 1) * HEAD_DIM]
            s = lax.dot_general(qh, kc, _NT, preferred_element_type=F32)
            s = s + bias_ref[h * QT:(h + 1) * QT, :] + madd
            m = jnp.max(s, axis=-1, keepdims=True)
            e = jnp.exp(s - m)
            p = (e * visf) * (1.0 / jnp.sum(e, axis=-1, keepdims=True))
            psum = psum + p
            o = jnp.dot(p.astype(BF16), vc, preferred_element_type=F32)
            o_ref[:, h * HEAD_DIM:(h + 1) * HEAD_DIM] = (o * gate[:, h:h + 1]).astype(o_ref.dtype)
        imp = lax.dot_general(covt_ref[...], psum, _NT, precision=HIGHEST, preferred_element_type=F32)
        score = jnp.where(visb, jnp.where(forced, FORCE, imp), NEG)
        sel = _rank_select(score, jidx, min(TOP_N, nb)) & visb
        selt_ref[g * nb:(g + 1) * nb, :] = sel.astype(F32)
    sel_ref[...] = selt_ref[...].T


def _cmp_attn(q, kcv, gate, ghi, glo, batch, length):
    nq = length // QT
    nc = kcv.shape[2] // N_KV
    nb = length // SEL_BLOCK
    covt = jnp.asarray(_cover_t_np(nc, nb))
    row = lambda w: pl.BlockSpec((QT, w), lambda b, i: (b * nq + i, 0))
    return pl.pallas_call(
        _cmp_attn_kernel,
        grid=(batch, nq),
        in_specs=[row(Q_COLS), pl.BlockSpec((1, 2, N_KV * nc, HEAD_DIM), lambda b, i: (b, 0, 0, 0)), row(LANES),
                  _resident(ghi.shape), _resident(glo.shape), _resident(covt.shape)],
        out_specs=[row(Q_COLS), row(N_KV * nb)],
        out_shape=[jax.ShapeDtypeStruct((batch * length, Q_COLS), BF16),
                   jax.ShapeDtypeStruct((batch * length, N_KV * nb), F32)],
        scratch_shapes=[pltpu.VMEM((N_HEADS * QT, nc), F32), pltpu.VMEM((N_KV * nb, QT), F32)],
        compiler_params=_cp(("parallel", "arbitrary"), 48),
        name="nsa_cmp_attn_select",
    )(q, kcv, gate, ghi, glo, covt)


def _flash_kernel(*refs, selected):
    if selected:
        q_ref, kv_ref, gate_ref, tab_ref, sel_ref, ex_ref, o_ref, m_ref, l_ref, acc_ref, madd_ref = refs
    else:
        q_ref, kv_ref, gate_ref, tab_ref, o_ref, m_ref, l_ref, acc_ref = refs
    i = pl.program_id(1)
    ra = lax.broadcasted_iota(jnp.int32, (QT, QT), 0)
    cb = lax.broadcasted_iota(jnp.int32, (QT, QT), 1)
    causal = jnp.where(ra >= cb, 0.0, NEG)
    edge = jnp.where(ra <= cb, 0.0, NEG)
    m_ref[...] = jnp.full(m_ref.shape, -jnp.inf, F32)
    l_ref[...] = jnp.zeros(l_ref.shape, F32)
    acc_ref[...] = jnp.zeros(acc_ref.shape, F32)
    gate = gate_ref[...]
    gbase = N_HEADS if selected else 2 * N_HEADS

    for g in range(N_KV):
        if selected:
            sel = sel_ref[:, g * ex_ref.shape[0]:(g + 1) * ex_ref.shape[0]].astype(BF16)
            madd_ref[...] = (jnp.dot(sel, ex_ref[...], preferred_element_type=F32) - 1.0) * (-NEG)
        qs = [q_ref[:, (g * GQ + r) * HEAD_DIM:(g * GQ + r + 1) * HEAD_DIM] for r in range(GQ)]

        def tile(j, kind):
            rows = pl.ds(pl.multiple_of(j * QT, QT), QT)
            kt = kv_ref[rows, g * HEAD_DIM:(g + 1) * HEAD_DIM]
            vt = kv_ref[rows, N_KV * HEAD_DIM + g * HEAD_DIM:N_KV * HEAD_DIM + (g + 1) * HEAD_DIM]
            extra = None
            if selected:
                extra = madd_ref[:, pl.ds(pl.multiple_of(j * QT, QT), QT)]
            if kind == "diag":
                extra = causal if extra is None else extra + causal
            if kind == "edge":
                extra = edge
            for r in range(GQ):
                h = g * GQ + r
                s = lax.dot_general(qs[r], kt, _NT, preferred_element_type=F32)
                if kind == "diag":
                    s = s + tab_ref[h, :QT, :]
                if kind == "near":
                    s = s + tab_ref[h, QT:, :]
                if extra is not None:
                    s = s + extra
                m_old = m_ref[h]
                m_new = jnp.maximum(m_old, jnp.max(s, axis=-1, keepdims=True))
                alpha = jnp.exp(m_old - m_new)
                p = jnp.exp(s - m_new)
                l_ref[h] = alpha * l_ref[h] + jnp.sum(p, axis=-1, keepdims=True)
                acc_ref[h] = alpha * acc_ref[h] + jnp.dot(p.astype(BF16), vt, preferred_element_type=F32)
                m_ref[h] = m_new

        if selected:
            def far_body(j, c):
                tile(j, "far")
                return c
            lax.fori_loop(0, jnp.maximum(i - 1, 0), far_body, 0)
        else:
            for d in (4, 3, 2):
                @pl.when(i >= d)
                def _():
                    tile(i - d, "edge" if d == WINDOW // QT else "far")

        @pl.when(i >= 1)
        def _():
            tile(i - 1, "near")

        tile(i, "diag")

    for h in range(N_HEADS):
        scale = gate[:, gbase + h:gbase + h + 1] / l_ref[h]
        o_ref[:, h * HEAD_DIM:(h + 1) * HEAD_DIM] = (acc_ref[h] * scale).astype(o_ref.dtype)


def _ex_np(nb, length):
    return (np.arange(nb)[:, None] == (np.arange(length)[None, :] // SEL_BLOCK)).astype(np.float32)


def _flash(q, kvb, gate, tab, selmask, batch, length, selected):
    nq = length // QT
    nb = length // SEL_BLOCK
    row = lambda w: pl.BlockSpec((QT, w), lambda b, i: (b * nq + i, 0))
    in_specs = [row(Q_COLS), pl.BlockSpec((length, 2 * N_KV * HEAD_DIM), lambda b, i: (b, 0 if selected else 1)),
                row(LANES), _resident(tab.shape)]
    args = [q, kvb, gate, tab]
    scratch = [pltpu.VMEM((N_HEADS, QT, 1), F32), pltpu.VMEM((N_HEADS, QT, 1), F32),
               pltpu.VMEM((N_HEADS, QT, HEAD_DIM), F32)]
    if selected:
        ex = jnp.asarray(_ex_np(nb, length), BF16)
        in_specs += [row(N_KV * nb), _resident(ex.shape)]
        args += [selmask, ex]
        scratch.append(pltpu.VMEM((QT, length), F32))
    return pl.pallas_call(
        functools.partial(_flash_kernel, selected=selected),
        grid=(batch, nq),
        in_specs=in_specs,
        out_specs=row(Q_COLS),
        out_shape=jax.ShapeDtypeStruct((batch * length, Q_COLS), BF16),
        scratch_shapes=scratch,
        compiler_params=_cp(("parallel", "arbitrary"), 48),
        name="nsa_selected_attn" if selected else "nsa_window_attn",
    )(*args)


def _s5_sample_kernel(u_ref, hr_ref, hi_ref, wb_ref, wc_ref, a1_ref, d_ref, g_ref, sre_ref, sim_ref):
    u = u_ref[...]
    x = jnp.dot(u, wb_ref[0], precision=HIGHEST, preferred_element_type=F32)
    ar, ai = a1_ref[0, :, :S5_SL], a1_ref[0, :, S5_SL:]
    h0r, h0i = hr_ref[...], hi_ref[...]
    hr = x[:, :S5_SL] + (ar * h0r - ai * h0i)
    hi = x[:, S5_SL:] + (ar * h0i + ai * h0r)
    sre_ref[...] = hr
    sim_ref[...] = hi
    y = jnp.dot(jnp.concatenate([hr, hi], axis=1), wc_ref[0], precision=HIGHEST, preferred_element_type=F32)
    g_ref[...] = jax.nn.gelu(y + d_ref[...] * u).astype(g_ref.dtype)


def _s5_sample(u, h0_re, h0_im, wb, wc, a1, d_skip):
    n = u.shape[0]
    st = pl.BlockSpec((n, S5_SL), lambda s: (0, s))
    return pl.pallas_call(
        _s5_sample_kernel,
        grid=(S5_NSG,),
        in_specs=[pl.BlockSpec((n, LANES), lambda s: (0, s)), st, st,
                  pl.BlockSpec((1, LANES, 2 * S5_SL), lambda s: (s, 0, 0)),
                  pl.BlockSpec((1, 2 * S5_SL, LANES), lambda s: (s, 0, 0)),
                  pl.BlockSpec((1, 1, 2 * S5_SL), lambda s: (s, 0, 0)),
                  pl.BlockSpec((1, LANES), lambda s: (0, s))],
        out_specs=[pl.BlockSpec((n, LANES), lambda s: (0, s)), st, st],
        out_shape=[jax.ShapeDtypeStruct((n, D_MODEL), BF16),
                   jax.ShapeDtypeStruct((n, N_GROUPS * STATE), F32),
                   jax.ShapeDtypeStruct((n, N_GROUPS * STATE), F32)],
        compiler_params=_cp(("parallel",), 32),
        name="s5_sample",
    )(u, h0_re, h0_im, wb, wc, a1, d_skip.reshape(1, D_MODEL))


def _bias_rows_kernel(idx_ref, rb_ref, o_ref):
    rb = rb_ref[...]
    shifted = rb - rb[N_BUCKETS - 1:N_BUCKETS, :]
    lane = lax.broadcasted_iota(jnp.int32, o_ref.shape, 1)
    onehot = (idx_ref[...] == lane).astype(F32)
    o_ref[...] = jnp.dot(onehot, shifted, precision=HIGHEST, preferred_element_type=F32)


def _bias_rows(rel_bias, idx_col):
    m = idx_col.shape[0]
    rb = jnp.pad(rel_bias, ((0, LANES - N_BUCKETS), (0, LANES - N_HEADS)))
    return pl.pallas_call(
        _bias_rows_kernel,
        in_specs=[pl.BlockSpec((m, 1), lambda: (0, 0)), pl.BlockSpec((LANES, LANES), lambda: (0, 0))],
        out_specs=pl.BlockSpec((m, LANES), lambda: (0, 0)),
        out_shape=jax.ShapeDtypeStruct((m, LANES), F32),
        name="t5_bias_rows",
    )(idx_col, rb)


def _softmax_rows(s, valid):
    m = jnp.max(s, axis=0, keepdims=True)
    e = jnp.exp(s - m)
    return (e * valid) * (1.0 / jnp.sum(e, axis=0, keepdims=True))


def _heads_out(p, v_of_group):
    pt = p.T[:N_HEADS].astype(BF16)
    rowg = lax.broadcasted_iota(jnp.int32, (N_HEADS, HEAD_DIM), 0) // GQ
    out = jnp.zeros((N_HEADS, HEAD_DIM), F32)
    for g in range(N_KV):
        out = out + jnp.where(rowg == g, jnp.dot(pt, v_of_group(g), preferred_element_type=F32), 0.0)
    return out


def _nsa_decode_kernel(pt_ref, x0, x1, x2, x3, xs_ref, win_ref, qbd_ref, new_ref, gate_ref,
                       w1_ref, w1f_ref, pe_ref, w2_ref, covt_ref, bias_ref, ex_ref, o_ref,
                       xf_ref, ks_ref, vs_ref, kw_ref, vw_ref, *, past_len, n_cmp):
    p = pl.program_id(1)
    npages = pl.num_programs(1)
    page = x0.shape[0]
    hp = page // CMP_STRIDE
    nh = past_len // CMP_STRIDE
    ncp = nh
    nkv = N_KV * HEAD_DIM
    sel_rows = ks_ref.shape[0]
    win_rows = kw_ref.shape[0]
    wb = win_ref.shape[0]

    xs = (x0, x1, x2, x3)
    for slot in range(2):
        for c in range(2):
            ev, od = _flatten_pairs(xs[2 * slot + c], hp)
            for k, val in ((2 * c, ev), (2 * c + 1, od)):
                xf_ref[slot, pl.ds(pl.multiple_of(k * nh + p * hp, hp), hp), :] = val
    rows = pl.ds(pl.multiple_of(p * page, page), page)
    ks_ref[rows, :] = xs_ref[:, :nkv].astype(BF16)
    vs_ref[rows, :] = xs_ref[:, nkv:].astype(BF16)

    @pl.when(p == npages - 1)
    def _():
        qbd = qbd_ref[0]
        lane = lax.broadcasted_iota(jnp.int32, (LANES, LANES), 1)
        subl = lax.broadcasted_iota(jnp.int32, (LANES, LANES), 0)
        headlane = lane < N_HEADS
        new = new_ref[0].astype(F32)
        first = lax.broadcasted_iota(jnp.int32, (LANES, nkv), 0) == 0
        pad_rows = lambda r: jnp.where(first, jnp.broadcast_to(r, (LANES, nkv)), 0.0).astype(BF16)
        ks_ref[past_len:, :] = pad_rows(new[:, :nkv])
        vs_ref[past_len:, :] = pad_rows(new[:, nkv:2 * nkv])
        kw_ref[:wb, :] = win_ref[:, :nkv].astype(BF16)
        vw_ref[:wb, :] = win_ref[:, nkv:].astype(BF16)
        kw_ref[wb:, :] = pad_rows(new[:, 2 * nkv:3 * nkv])
        vw_ref[wb:, :] = pad_rows(new[:, 3 * nkv:])

        kc = _compress_rows(xf_ref[0].astype(BF16), w1_ref, w1f_ref, pe_ref, w2_ref, 0).astype(BF16)
        vc = _compress_rows(xf_ref[1].astype(BF16), w1_ref, w1f_ref, pe_ref, w2_ref, 1).astype(BF16)
        s = jnp.zeros((ncp, LANES), F32)
        for g in range(N_KV):
            s = s + jnp.dot(kc[g * ncp:(g + 1) * ncp], qbd[g * HEAD_DIM:(g + 1) * HEAD_DIM],
                            preferred_element_type=F32)
        nidx = lax.broadcasted_iota(jnp.int32, (ncp, LANES), 0)
        valid_c = (nidx < n_cmp).astype(F32)
        s = s + bias_ref[:ncp, :] + jnp.where(nidx < n_cmp, 0.0, NEG)
        pc = _softmax_rows(s, valid_c)
        o_c = _heads_out(pc, lambda g: vc[g * ncp:(g + 1) * ncp])

        grp = ((subl // GQ == lane) & (subl < N_HEADS)).astype(F32)
        psum = jnp.dot(pc, grp, precision=HIGHEST, preferred_element_type=F32)
        imp = jnp.dot(covt_ref[...], psum, precision=HIGHEST, preferred_element_type=F32)
        jt = past_len // SEL_BLOCK
        nb = jt + 1

        def scored(x, j):
            forced = (j == 0) | (j == jt) | (j == jt - 1)
            return jnp.where(j < nb, jnp.where(forced, FORCE, x), NEG)

        sc_col = scored(imp, subl)
        sc_row = scored(imp.T, lane)
        selm = jnp.zeros((LANES, LANES), F32)
        for g in range(N_KV):
            col = sc_col[:, g:g + 1]
            row = sc_row[g:g + 1, :]
            ahead = (row > col) | ((row == col) & (lane < subl))
            rank = jnp.sum(jnp.where(ahead, 1.0, 0.0), axis=-1, keepdims=True)
            chosen = jnp.where((rank < min(TOP_N, nb)) & (subl[:, :1] < nb), 1.0, 0.0)
            selm = selm + chosen * jnp.where((lane // GQ == g) & headlane, 1.0, 0.0)

        pos = lax.broadcasted_iota(jnp.int32, (sel_rows, LANES), 0)
        inblk = jnp.dot(ex_ref[...], selm.astype(BF16), preferred_element_type=F32)
        ok_s = (inblk > 0.5) & (pos <= past_len)
        s = jnp.dot(ks_ref[...], qbd, preferred_element_type=F32) + bias_ref[ncp:ncp + sel_rows, :]
        s = jnp.where(ok_s, s, NEG)
        ps = _softmax_rows(s, ok_s.astype(F32))
        o_s = _heads_out(ps, lambda g: vs_ref[:, g * HEAD_DIM:(g + 1) * HEAD_DIM])

        posw = lax.broadcasted_iota(jnp.int32, (win_rows, LANES), 0)
        ok_w = posw <= wb
        s = jnp.dot(kw_ref[...], qbd, preferred_element_type=F32) + bias_ref[ncp + sel_rows:, :]
        s = jnp.where(ok_w, s, NEG)
        pw = _softmax_rows(s, ok_w.astype(F32))
        o_w = _heads_out(pw, lambda g: vw_ref[:, g * HEAD_DIM:(g + 1) * HEAD_DIM])

        gate = gate_ref[0]
        o_ref[0] = gate[:, 0:1] * o_c + gate[:, 1:2] * o_s + gate[:, 2:3] * o_w


def _decode_bias_idx(past_len, wb, ncp, sel_rows, win_rows):
    n = np.arange(ncp)
    cmp_d = past_len - (n * CMP_STRIDE + CMP_BLOCK - 1)
    sel_d = past_len - np.arange(sel_rows)
    win_d = np.concatenate([wb - np.arange(wb), np.zeros(win_rows - wb, np.int64)])
    return _t5_bucket_np(np.concatenate([cmp_d, sel_d, win_d]))[:, None].astype(np.int32)


def _nsa_decode(cache_kv, cache_win, page_table, qbd, new_rows, gate3, cw, rel_bias):
    n_pool, page = cache_kv.shape[:2]
    nseq, npages = page_table.shape
    past_len = npages * page
    wb = cache_win.shape[1]
    w1b, w1f, pe, w2b = cw
    nh = past_len // CMP_STRIDE
    n_cmp = (past_len + 1 - CMP_BLOCK) // CMP_STRIDE + 1
    sel_rows = past_len + LANES
    win_rows = wb + LANES
    nkv = N_KV * HEAD_DIM
    kvflat = cache_kv.reshape(n_pool * page, 4 * nkv)
    winflat = cache_win.reshape(nseq * wb, 2 * nkv)
    covt = jnp.asarray(_cover_t_np(LANES, LANES))
    bias = _bias_rows(rel_bias, jnp.asarray(_decode_bias_idx(past_len, wb, nh, sel_rows, win_rows)))
    ex = jnp.asarray((np.arange(sel_rows)[:, None] // SEL_BLOCK == np.arange(LANES)[None, :]).astype(np.float32), BF16)
    xspec = lambda c: pl.BlockSpec((page, LANES), lambda b, p, pt: (pt[b, p], c))
    const = lambda a: pl.BlockSpec(a.shape, lambda b, p, pt: (0,) * a.ndim)
    grid_spec = pltpu.PrefetchScalarGridSpec(
        num_scalar_prefetch=1,
        grid=(nseq, npages),
        in_specs=[xspec(0), xspec(1), xspec(2), xspec(3),
                  pl.BlockSpec((page, 2 * nkv), lambda b, p, pt: (pt[b, p], 1)),
                  pl.BlockSpec((wb, 2 * nkv), lambda b, p, pt: (b, 0)),
                  pl.BlockSpec((1, N_KV * HEAD_DIM, LANES), lambda b, p, pt: (b, 0, 0)),
                  pl.BlockSpec((1, 1, 4 * nkv), lambda b, p, pt: (b, 0, 0)),
                  pl.BlockSpec((1, N_HEADS, LANES), lambda b, p, pt: (b, 0, 0)),
                  const(w1b), const(w1f), const(pe), const(w2b), const(covt), const(bias), const(ex)],
        out_specs=pl.BlockSpec((1, N_HEADS, HEAD_DIM), lambda b, p, pt: (b, 0, 0)),
        scratch_shapes=[pltpu.VMEM((2, N_KV * nh, CMP_STRIDE * HEAD_DIM), F32),
                        pltpu.VMEM((sel_rows, nkv), BF16), pltpu.VMEM((sel_rows, nkv), BF16),
                        pltpu.VMEM((win_rows, nkv), BF16), pltpu.VMEM((win_rows, nkv), BF16)])
    out = pl.pallas_call(
        functools.partial(_nsa_decode_kernel, past_len=past_len, n_cmp=n_cmp),
        grid_spec=grid_spec,
        out_shape=jax.ShapeDtypeStruct((nseq, N_HEADS, HEAD_DIM), F32),
        compiler_params=_cp(("arbitrary", "arbitrary"), 56),
        name="nsa_decode",
    )(page_table, kvflat, kvflat, kvflat, kvflat, kvflat, winflat, qbd, new_rows, gate3, w1b, w1f, pe, w2b, covt, bias, ex)
    return out.reshape(nseq, Q_COLS)


PROMPT_TM = 512


def kernel(x_prompt, x_sample, state_s5_re, state_s5_im, cache_kv, cache_win, page_table, rel_bias, norm_mix, norm_ffn, norm_final, s5_lam_re, s5_lam_im, s5_log_dt, s5_b_re, s5_b_im, s5_c_re, s5_c_im, s5_d, s5_w_glu, ffn_w_in, ffn_w_out, nsa_w_in, nsa_phi_pe, nsa_phi_w1, nsa_phi_w2, nsa_w_o, moe_router, moe_w_in, moe_w_out):
    batch, length, _ = x_prompt.shape
    nseq = x_sample.shape[0]
    xp = x_prompt.reshape(batch * length, D_MODEL)
    xs = x_sample.reshape(nseq, D_MODEL)
    nstate = N_GROUPS * STATE

    wk, ws, wv, wb, wc, apow, a1 = _s5_prep(s5_lam_re, s5_lam_im, s5_log_dt, s5_b_re, s5_b_im, s5_c_re, s5_c_im)
    w_glu = s5_w_glu.astype(BF16)
    ffn_in, ffn_out = ffn_w_in.astype(BF16), ffn_w_out.astype(BF16)
    nsa_in = jnp.pad(nsa_w_in, ((0, 0), (0, NSA_PAD - nsa_w_in.shape[1]))).astype(BF16)
    w_o = nsa_w_o.astype(BF16)
    router = jnp.pad(moe_router, ((0, 0), (0, LANES - N_EXPERTS)))
    moe_in, moe_out = moe_w_in.astype(BF16), moe_w_out.astype(BF16)
    cw = _compress_weights(nsa_phi_pe, nsa_phi_w1, nsa_phi_w2)
    tabs = _bias_tables(rel_bias, jnp.asarray(_prompt_bias_idx()))
    tab = tabs[:, :2 * QT]
    gflat = tabs[:, 2 * QT:].reshape(N_HEADS * QT, LANES)
    ghi = gflat.astype(BF16)
    glo = (gflat - ghi.astype(F32)).astype(BF16)

    def layer0(x, tm, mixer):
        u = _rmsnorm(x, norm_mix[0], tm)
        g, s_re, s_im = mixer(u)
        y1, h1 = _glu(g, x, w_glu, norm_ffn[0], tm)
        y2, h2 = _ffn(h1, y1, ffn_in, ffn_out, norm_mix[1], tm)
        return y2, h2, s_re, s_im

    def layer1_tail(oc, os_, ow, y2, tm):
        y3, h3, mg = _merge(oc, os_, ow, y2, w_o, norm_ffn[1], router, tm)
        return _moe(h3, y3, mg, moe_in, moe_out, norm_final, tm)

    y2p, h2p, sre_p, sim_p = layer0(xp, PROMPT_TM, lambda u: _s5_prompt(u, s5_d, wk, ws, wv, apow, batch, length))
    q, kvf, winf, kvb, gate = _nsa_proj(h2p, nsa_in, PROMPT_TM)
    kcv = _compress_prompt(kvf, cw, batch, length)
    oc, sel = _cmp_attn(q, kcv, gate, ghi, glo, batch, length)
    os_ = _flash(q, kvb, gate, tab, sel, batch, length, True)
    ow = _flash(q, kvb, gate, tab, None, batch, length, False)
    yp = layer1_tail(oc, os_, ow, y2p, PROMPT_TM)

    y2s, h2s, sre_s, sim_s = layer0(
        xs, nseq, lambda u: _s5_sample(u, state_s5_re.reshape(nseq, nstate), state_s5_im.reshape(nseq, nstate),
                                       wb, wc, a1, s5_d))
    qs, kvfs, winfs, kvbs, gates = _nsa_proj(h2s, nsa_in, nseq)
    qt = jnp.tile(qs.reshape(nseq, N_HEADS, HEAD_DIM).transpose(0, 2, 1), (1, N_KV, 1))
    own = (np.arange(N_KV * HEAD_DIM)[:, None] // HEAD_DIM) == (np.arange(N_HEADS)[None, :] // GQ)
    qbd = jnp.pad(jnp.where(own[None], qt, jnp.zeros((), BF16)), ((0, 0), (0, 0), (0, LANES - N_HEADS)))
    gate3 = jnp.pad(gates[:, :GATE_COLS].reshape(nseq, 3, N_HEADS).transpose(0, 2, 1),
                    ((0, 0), (0, 0), (0, LANES - 3)))
    od = _nsa_decode(cache_kv, cache_win, page_table, qbd, kvbs.reshape(nseq, 1, 4 * N_KV * HEAD_DIM), gate3, cw,
                     rel_bias)
    zero = jnp.zeros((nseq, D_MODEL), BF16)
    ys = layer1_tail(od.astype(BF16), zero, zero, y2s, nseq)

    wp = min(WINDOW, length)
    return (yp.reshape(batch, length, D_MODEL),
            ys.reshape(nseq, 1, D_MODEL),
            sre_p.reshape(batch, N_GROUPS, STATE), sim_p.reshape(batch, N_GROUPS, STATE),
            kvf.reshape(batch, length, 4, N_KV, HEAD_DIM),
            winf.reshape(batch, length, 2, N_KV, HEAD_DIM)[:, length - wp:],
            sre_s.reshape(nseq, N_GROUPS, STATE), sim_s.reshape(nseq, N_GROUPS, STATE),
            kvfs.reshape(nseq, 1, 4, N_KV, HEAD_DIM),
            winfs.reshape(nseq, 1, 2, N_KV, HEAD_DIM))
```

```python
import functools
import math

import numpy as np
import jax
import jax.numpy as jnp
from jax import lax
from jax.experimental import pallas as pl
from jax.experimental.pallas import tpu as pltpu

F32 = jnp.float32
BF16 = jnp.bfloat16
HIGHEST = lax.Precision.HIGHEST

D_MODEL = 1024
GROUP_CH = 16
N_GROUPS = D_MODEL // GROUP_CH
STATE = 64
N_HEADS = 16
HEAD_DIM = 64
N_KV = 4
GQ = N_HEADS // N_KV
CMP_BLOCK = 32
CMP_STRIDE = 16
CMP_HIDDEN = 2 * HEAD_DIM
SEL_BLOCK = 64
TOP_N = 16
WINDOW = 512
Q_COLS = N_HEADS * HEAD_DIM
KV_COLS = 6 * N_KV * HEAD_DIM
GATE_COLS = 3 * N_HEADS
N_BUCKETS = 32
MAX_DISTANCE = 128
D_FF = 2816
N_EXPERTS = 8
EXPERT_FF = 1408
EPS = 1e-6
NEG = -1e30
FORCE = 1e4

LANES = 128
S5_T = 8
S5_SG = LANES // GROUP_CH
S5_NSG = N_GROUPS // S5_SG
S5_SL = S5_SG * STATE
QT = 128


def _cp(sem, vmem_mb):
    return pltpu.CompilerParams(dimension_semantics=sem, vmem_limit_bytes=vmem_mb << 20)


def _cmul(ar, ai, br, bi):
    return ar * br - ai * bi, ar * bi + ai * br


def _s5_prep_kernel(lr_r, li_r, ldt_r, br_r, bi_r, lr_c, li_c, ldt_c, cr_c, ci_c,
                    wk_ref, ws_ref, wv_ref, apow_ref, a1_ref):
    def disc(lr, li, ldt):
        dt = jnp.exp(ldt)
        mag = jnp.exp(lr * dt)
        abr = mag * jnp.cos(li * dt)
        abi = mag * jnp.sin(li * dt)
        den = lr * lr + li * li
        nr = abr - 1.0
        fr = (nr * lr + abi * li) / den
        fi = (abi * lr - nr * li) / den
        return abr, abi, fr, fi

    abr, abi, fr, fi = disc(lr_r[0], li_r[0], ldt_r[0])
    rg = lax.broadcasted_iota(jnp.int32, (LANES, S5_SL), 0) // GROUP_CH
    cg = lax.broadcasted_iota(jnp.int32, (LANES, S5_SL), 1) // STATE
    mask_r = (rg == cg).astype(F32)
    b_r, b_i = br_r[...], bi_r[...]
    bbr = (fr * b_r - fi * b_i) * mask_r
    bbi = (fr * b_i + fi * b_r) * mask_r
    abr_c, abi_c, _, _ = disc(lr_c[0], li_c[0], ldt_c[0])
    rg2 = lax.broadcasted_iota(jnp.int32, (S5_SL, LANES), 0) // STATE
    cg2 = lax.broadcasted_iota(jnp.int32, (S5_SL, LANES), 1) // GROUP_CH
    mask_c = (rg2 == cg2).astype(F32)
    c_r = cr_c[0] * mask_c
    c_i = ci_c[0] * mask_c

    a1_ref[0, :, :S5_SL] = abr
    a1_ref[0, :, S5_SL:] = abi

    pr = jnp.ones_like(abr)
    pi = jnp.zeros_like(abr)
    pr_c = jnp.ones_like(abr_c)
    pi_c = jnp.zeros_like(abr_c)
    for tau in range(S5_T + 1):
        wv_ref[tau, 0, :S5_SL, :] = (c_r * pr_c - c_i * pi_c).astype(BF16)
        wv_ref[tau, 0, S5_SL:, :] = (-(c_r * pi_c + c_i * pr_c)).astype(BF16)
        if tau < S5_T:
            er, ei = _cmul(pr, pi, bbr, bbi)
            s = S5_T - 1 - tau
            ws_ref[s, 0, :, :S5_SL] = er.astype(BF16)
            ws_ref[s, 0, :, S5_SL:] = ei.astype(BF16)
            k = (jnp.dot(er, c_r, precision=HIGHEST, preferred_element_type=F32)
                 - jnp.dot(ei, c_i, precision=HIGHEST, preferred_element_type=F32))
            wk_ref[tau, 0] = k.astype(BF16)
        pr, pi = _cmul(pr, pi, abr, abi)
        pr_c, pi_c = _cmul(pr_c, pi_c, abr_c, abi_c)
    tr, ti = jnp.ones_like(abr), jnp.zeros_like(abr)
    for _ in range(S5_T):
        tr, ti = _cmul(tr, ti, abr, abi)
    qr, qi = jnp.ones_like(abr), jnp.zeros_like(abr)
    for i in range(S5_T + 1):
        apow_ref[0, i:i + 1, :S5_SL] = qr
        apow_ref[0, i:i + 1, S5_SL:] = qi
        qr, qi = _cmul(qr, qi, tr, ti)


def _s5_prep(lam_re, lam_im, log_dt, b_re, b_im, c_re, c_im):
    n = S5_NSG
    row = lambda a: a.reshape(n, 1, S5_SL)
    col = lambda a: a.reshape(n, S5_SL, 1)
    ldt = jnp.repeat(log_dt, STATE)
    bt = lambda b: jnp.tile(b.transpose(0, 2, 1).reshape(D_MODEL, STATE), (1, S5_SG))
    ct = lambda c: jnp.tile(c.reshape(n, LANES, STATE).transpose(0, 2, 1), (1, S5_SG, 1))
    rspec = pl.BlockSpec((1, 1, S5_SL), lambda s: (s, 0, 0))
    cspec = pl.BlockSpec((1, S5_SL, 1), lambda s: (s, 0, 0))
    t1 = S5_T + 1
    return pl.pallas_call(
        _s5_prep_kernel,
        grid=(n,),
        in_specs=[rspec, rspec, rspec,
                  pl.BlockSpec((LANES, S5_SL), lambda s: (s, 0)), pl.BlockSpec((LANES, S5_SL), lambda s: (s, 0)),
                  cspec, cspec, cspec,
                  pl.BlockSpec((1, S5_SL, LANES), lambda s: (s, 0, 0)), pl.BlockSpec((1, S5_SL, LANES), lambda s: (s, 0, 0))],
        out_specs=[pl.BlockSpec((S5_T, 1, LANES, LANES), lambda s: (0, s, 0, 0)),
                   pl.BlockSpec((S5_T, 1, LANES, 2 * S5_SL), lambda s: (0, s, 0, 0)),
                   pl.BlockSpec((t1, 1, 2 * S5_SL, LANES), lambda s: (0, s, 0, 0)),
                   pl.BlockSpec((1, t1 + 7, 2 * S5_SL), lambda s: (s, 0, 0)),
                   pl.BlockSpec((1, 1, 2 * S5_SL), lambda s: (s, 0, 0))],
        out_shape=[jax.ShapeDtypeStruct((S5_T, n, LANES, LANES), BF16),
                   jax.ShapeDtypeStruct((S5_T, n, LANES, 2 * S5_SL), BF16),
                   jax.ShapeDtypeStruct((t1, n, 2 * S5_SL, LANES), BF16),
                   jax.ShapeDtypeStruct((n, t1 + 7, 2 * S5_SL), F32),
                   jax.ShapeDtypeStruct((n, 1, 2 * S5_SL), F32)],
        compiler_params=_cp(("arbitrary",), 48),
        name="s5_prep",
    )(row(lam_re), row(lam_im), row(ldt), bt(b_re), bt(b_im),
      col(lam_re), col(lam_im), col(ldt), ct(c_re), ct(c_im))


def _rms(x, g):
    return x * lax.rsqrt(jnp.mean(x * x, axis=-1, keepdims=True) + EPS) * g


def _rmsnorm_kernel(x_ref, g_ref, o_ref):
    o_ref[...] = _rms(x_ref[...], g_ref[...]).astype(o_ref.dtype)


def _rmsnorm(x, g, tm, dtype=F32):
    n = x.shape[0]
    return pl.pallas_call(
        _rmsnorm_kernel,
        grid=(n // tm,),
        in_specs=[pl.BlockSpec((tm, D_MODEL), lambda i: (i, 0)), pl.BlockSpec((1, D_MODEL), lambda i: (0, 0))],
        out_specs=pl.BlockSpec((tm, D_MODEL), lambda i: (i, 0)),
        out_shape=jax.ShapeDtypeStruct((n, D_MODEL), dtype),
        compiler_params=_cp(("parallel",), 32),
        name="rmsnorm",
    )(x, g.reshape(1, D_MODEL))


def _s5_prompt_kernel(u_ref, d_ref, wk_ref, ws_ref, wv_ref, apow_ref, g_ref, sre_ref, sim_ref,
                      ub_ref, s_ref, p_ref, gc_ref, hs_ref, y_ref):
    length = u_ref.shape[0]
    r = length // S5_T
    ng = r // 8

    acc = jnp.zeros((r, 2 * S5_SL), F32)
    for s in range(S5_T):
        us = u_ref[pl.ds(s, r, stride=S5_T), :].astype(BF16)
        ub_ref[s] = us
        acc = acc + jnp.dot(us, ws_ref[s, 0], preferred_element_type=F32)
    for c in range(2 * S5_SL // LANES):
        s_ref[c] = acc[:, c * LANES:(c + 1) * LANES]

    nsl = S5_SL // LANES
    for c in range(nsl):
        lr = slice(c * LANES, (c + 1) * LANES)
        li = slice(S5_SL + c * LANES, S5_SL + (c + 1) * LANES)

        def apw(i):
            return apow_ref[0, i:i + 1, lr], apow_ref[0, i:i + 1, li]

        a1r, a1i = apw(1)
        pr = s_ref[c, pl.ds(0, ng, stride=8), :]
        pi = s_ref[nsl + c, pl.ds(0, ng, stride=8), :]
        p_ref[0, c] = pr
        p_ref[0, nsl + c] = pi
        for i in range(1, 8):
            qr, qi = _cmul(a1r, a1i, pr, pi)
            pr = qr + s_ref[c, pl.ds(i, ng, stride=8), :]
            pi = qi + s_ref[nsl + c, pl.ds(i, ng, stride=8), :]
            p_ref[i, c] = pr
            p_ref[i, nsl + c] = pi

        a8r, a8i = apw(8)

        def carry(m, g):
            gr, gi = g
            gc_ref[c, pl.ds(m, 1), :] = gr
            gc_ref[nsl + c, pl.ds(m, 1), :] = gi
            nr, ni = _cmul(a8r, a8i, gr, gi)
            return (nr + p_ref[7, c, pl.ds(m, 1), :], ni + p_ref[7, nsl + c, pl.ds(m, 1), :])

        z = jnp.zeros((1, LANES), F32)
        er, ei = lax.fori_loop(0, ng, carry, (z, z))
        sre_ref[0, :, lr] = er
        sim_ref[0, :, lr] = ei

        gr = gc_ref[c]
        gi = gc_ref[nsl + c]
        hs_ref[c, pl.ds(0, ng, stride=8), :] = gr
        hs_ref[nsl + c, pl.ds(0, ng, stride=8), :] = gi
        for i in range(1, 8):
            air, aii = apw(i)
            hr, hi = _cmul(air, aii, gr, gi)
            hs_ref[c, pl.ds(i, ng, stride=8), :] = hr + p_ref[i - 1, c]
            hs_ref[nsl + c, pl.ds(i, ng, stride=8), :] = hi + p_ref[i - 1, nsl + c]

    hs = jnp.concatenate([hs_ref[c] for c in range(2 * S5_SL // LANES)], axis=1).astype(BF16)
    for t in range(S5_T):
        o = jnp.dot(hs, wv_ref[t + 1, 0], preferred_element_type=F32)
        for s in range(t + 1):
            o = o + jnp.dot(ub_ref[s], wk_ref[t - s, 0], preferred_element_type=F32)
        y_ref[pl.ds(t, r, stride=S5_T), :] = o
    y = y_ref[...] + d_ref[...] * u_ref[...]
    g_ref[...] = jax.nn.gelu(y).astype(g_ref.dtype)


def _s5_prompt(u, d_skip, wk, ws, wv, apow, batch, length):
    r = length // S5_T
    t1 = S5_T + 1
    return pl.pallas_call(
        _s5_prompt_kernel,
        grid=(S5_NSG, batch),
        in_specs=[pl.BlockSpec((length, LANES), lambda s, b: (b, s)),
                  pl.BlockSpec((1, LANES), lambda s, b: (0, s)),
                  pl.BlockSpec((S5_T, 1, LANES, LANES), lambda s, b: (0, s, 0, 0)),
                  pl.BlockSpec((S5_T, 1, LANES, 2 * S5_SL), lambda s, b: (0, s, 0, 0)),
                  pl.BlockSpec((t1, 1, 2 * S5_SL, LANES), lambda s, b: (0, s, 0, 0)),
                  pl.BlockSpec((1, t1 + 7, 2 * S5_SL), lambda s, b: (s, 0, 0))],
        out_specs=[pl.BlockSpec((length, LANES), lambda s, b: (b, s)),
                   pl.BlockSpec((1, 1, S5_SL), lambda s, b: (b, 0, s)),
                   pl.BlockSpec((1, 1, S5_SL), lambda s, b: (b, 0, s))],
        out_shape=[jax.ShapeDtypeStruct((batch * length, D_MODEL), BF16),
                   jax.ShapeDtypeStruct((batch, 1, N_GROUPS * STATE), F32),
                   jax.ShapeDtypeStruct((batch, 1, N_GROUPS * STATE), F32)],
        scratch_shapes=[pltpu.VMEM((S5_T, r, LANES), BF16),
                        pltpu.VMEM((2 * S5_SL // LANES, r, LANES), F32),
                        pltpu.VMEM((8, 2 * S5_SL // LANES, r // 8, LANES), F32),
                        pltpu.VMEM((2 * S5_SL // LANES, r // 8, LANES), F32),
                        pltpu.VMEM((2 * S5_SL // LANES, r, LANES), F32),
                        pltpu.VMEM((length, LANES), F32)],
        compiler_params=_cp(("arbitrary", "arbitrary"), 56),
        name="s5_prompt",
    )(u, d_skip.reshape(1, D_MODEL), wk, ws, wv, apow)


def _resident(shape):
    nd = len(shape)
    return pl.BlockSpec(shape, lambda *_: (0,) * nd, pipeline_mode=pl.Buffered(1))


def _rows(tm, width):
    return pl.BlockSpec((tm, width), lambda i: (i, 0))


def _mm(a, b):
    return jnp.dot(a, b, preferred_element_type=F32)


def _glu_kernel(g_ref, x_ref, w_ref, gain_ref, y_ref, h_ref):
    p = _mm(g_ref[...], w_ref[...])
    y = x_ref[...] + p[:, :D_MODEL] * jax.nn.sigmoid(p[:, D_MODEL:])
    y_ref[...] = y
    h_ref[...] = _rms(y, gain_ref[...]).astype(h_ref.dtype)


def _glu(g, x, w_glu, gain, tm):
    n = g.shape[0]
    return pl.pallas_call(
        _glu_kernel,
        grid=(n // tm,),
        in_specs=[_rows(tm, D_MODEL), _rows(tm, D_MODEL), _resident((D_MODEL, 2 * D_MODEL)), _resident((1, D_MODEL))],
        out_specs=[_rows(tm, D_MODEL), _rows(tm, D_MODEL)],
        out_shape=[jax.ShapeDtypeStruct((n, D_MODEL), F32), jax.ShapeDtypeStruct((n, D_MODEL), g.dtype)],
        compiler_params=_cp(("parallel",), 48),
        name="s5_glu",
    )(g, x, w_glu, gain.reshape(1, D_MODEL))


def _chunks(total, size):
    return [(s, min(size, total - s)) for s in range(0, total, size)]


def _swiglu_acc(h, win_ref, wout_ref, ff, lead=()):
    acc = None
    for s, w in _chunks(ff, 256):
        a = jnp.dot(h, win_ref[lead + (slice(None), slice(s, s + w))], preferred_element_type=F32)
        b = jnp.dot(h, win_ref[lead + (slice(None), slice(ff + s, ff + s + w))], preferred_element_type=F32)
        act = (jax.nn.silu(a) * b).astype(BF16)
        z = jnp.dot(act, wout_ref[lead + (slice(s, s + w), slice(None))], preferred_element_type=F32)
        acc = z if acc is None else acc + z
    return acc


def _ffn_kernel(h_ref, y_ref, win_ref, wout_ref, gain_ref, yo_ref, ho_ref):
    y = y_ref[...] + _swiglu_acc(h_ref[...], win_ref, wout_ref, D_FF)
    yo_ref[...] = y
    ho_ref[...] = _rms(y, gain_ref[...]).astype(ho_ref.dtype)


def _ffn(h, y, w_in, w_out, gain, tm):
    n = h.shape[0]
    return pl.pallas_call(
        _ffn_kernel,
        grid=(n // tm,),
        in_specs=[_rows(tm, D_MODEL), _rows(tm, D_MODEL), _resident((D_MODEL, 2 * D_FF)), _resident((D_FF, D_MODEL)),
                  _resident((1, D_MODEL))],
        out_specs=[_rows(tm, D_MODEL), _rows(tm, D_MODEL)],
        out_shape=[jax.ShapeDtypeStruct((n, D_MODEL), F32), jax.ShapeDtypeStruct((n, D_MODEL), BF16)],
        compiler_params=_cp(("parallel",), 56),
        name="ffn_swiglu",
    )(h, y, w_in, w_out, gain.reshape(1, D_MODEL))


NSA_PAD = 2688
KV_OFF = Q_COLS


def _nsa_proj_kernel(h_ref, w_ref, q_ref, kvf_ref, winf_ref, kvb_ref, gate_ref):
    p = _mm(h_ref[...], w_ref[...])
    q_ref[...] = (p[:, :Q_COLS] * (HEAD_DIM ** -0.5)).astype(q_ref.dtype)
    kvf_ref[...] = p[:, KV_OFF:KV_OFF + 1024]
    winf_ref[...] = p[:, KV_OFF + 1024:KV_OFF + 1536]
    kvb_ref[...] = p[:, KV_OFF + 512:KV_OFF + 1536].astype(kvb_ref.dtype)
    gate_ref[...] = jax.nn.sigmoid(p[:, KV_OFF + 1536:])


def _nsa_proj(h, w_in_pad, tm):
    n = h.shape[0]
    return pl.pallas_call(
        _nsa_proj_kernel,
        grid=(n // tm,),
        in_specs=[_rows(tm, D_MODEL), _resident((D_MODEL, NSA_PAD))],
        out_specs=[_rows(tm, Q_COLS), _rows(tm, 1024), _rows(tm, 512), _rows(tm, 1024), _rows(tm, LANES)],
        out_shape=[jax.ShapeDtypeStruct((n, Q_COLS), h.dtype),
                   jax.ShapeDtypeStruct((n, 1024), F32),
                   jax.ShapeDtypeStruct((n, 512), F32),
                   jax.ShapeDtypeStruct((n, 1024), h.dtype),
                   jax.ShapeDtypeStruct((n, LANES), F32)],
        compiler_params=_cp(("parallel",), 48),
        name="nsa_proj",
    )(h, w_in_pad)


def _merge_kernel(oc_ref, os_ref, ow_ref, y_ref, wo_ref, gain_ref, router_ref, yo_ref, ho_ref, mg_ref):
    o = (oc_ref[...].astype(F32) + os_ref[...].astype(F32) + ow_ref[...].astype(F32)).astype(wo_ref.dtype)
    y = y_ref[...] + _mm(o, wo_ref[...])
    yo_ref[...] = y
    hf = _rms(y, gain_ref[...])
    ho_ref[...] = hf.astype(ho_ref.dtype)
    logits = _mm(hf.astype(BF16), router_ref[...])
    lane = lax.broadcasted_iota(jnp.int32, logits.shape, 1)
    logits = jnp.where(lane < N_EXPERTS, logits, -jnp.inf)
    m1 = jnp.max(logits, axis=-1, keepdims=True)
    i1 = jnp.min(jnp.where(logits == m1, lane, LANES), axis=-1, keepdims=True)
    rest = jnp.where(lane == i1, -jnp.inf, logits)
    m2 = jnp.max(rest, axis=-1, keepdims=True)
    i2 = jnp.min(jnp.where(rest == m2, lane, LANES), axis=-1, keepdims=True)
    e2 = jnp.exp(m2 - m1)
    den = 1.0 + e2
    mg_ref[...] = jnp.where(lane == i1, 1.0 / den, 0.0) + jnp.where(lane == i2, e2 / den, 0.0)


def _merge(oc, os_, ow, y, w_o, gain, router_pad, tm):
    n = y.shape[0]
    return pl.pallas_call(
        _merge_kernel,
        grid=(n // tm,),
        in_specs=[_rows(tm, D_MODEL)] * 4 + [_resident((D_MODEL, D_MODEL)), _resident((1, D_MODEL)),
                                             _resident((D_MODEL, LANES))],
        out_specs=[_rows(tm, D_MODEL), _rows(tm, D_MODEL), _rows(tm, LANES)],
        out_shape=[jax.ShapeDtypeStruct((n, D_MODEL), F32), jax.ShapeDtypeStruct((n, D_MODEL), BF16),
                   jax.ShapeDtypeStruct((n, LANES), F32)],
        compiler_params=_cp(("parallel",), 48),
        name="nsa_merge_router",
    )(oc, os_, ow, y, w_o, gain.reshape(1, D_MODEL), router_pad)


def _moe_kernel(h_ref, y_ref, mg_ref, win_ref, wout_ref, gain_ref, o_ref, acc_ref):
    e = pl.program_id(1)

    @pl.when(e == 0)
    def _():
        acc_ref[...] = jnp.zeros_like(acc_ref)

    mg = mg_ref[...]
    lane = lax.broadcasted_iota(jnp.int32, mg.shape, 1)
    gcol = jnp.sum(jnp.where(lane == e, mg, 0.0), axis=-1, keepdims=True)
    acc_ref[...] += gcol * _swiglu_acc(h_ref[...], win_ref, wout_ref, EXPERT_FF, lead=(0,))

    @pl.when(e == N_EXPERTS - 1)
    def _():
        o_ref[...] = _rms(y_ref[...] + acc_ref[...], gain_ref[...])


def _moe(h, y, mg, w_in, w_out, gain, tm):
    n = h.shape[0]
    rows = lambda w: pl.BlockSpec((tm, w), lambda i, e: (i, 0))
    return pl.pallas_call(
        _moe_kernel,
        grid=(n // tm, N_EXPERTS),
        in_specs=[rows(D_MODEL), rows(D_MODEL), rows(LANES),
                  pl.BlockSpec((1, D_MODEL, 2 * EXPERT_FF), lambda i, e: (e, 0, 0)),
                  pl.BlockSpec((1, EXPERT_FF, D_MODEL), lambda i, e: (e, 0, 0)),
                  pl.BlockSpec((1, D_MODEL), lambda i, e: (0, 0))],
        out_specs=rows(D_MODEL),
        out_shape=jax.ShapeDtypeStruct((n, D_MODEL), F32),
        scratch_shapes=[pltpu.VMEM((tm, D_MODEL), F32)],
        compiler_params=_cp(("parallel", "arbitrary"), 56),
        name="moe_final_norm",
    )(h, y, mg, w_in, w_out, gain.reshape(1, D_MODEL))


def _t5_bucket_np(dist):
    n = np.maximum(np.asarray(dist, np.int64), 0)
    exact = N_BUCKETS // 2
    logpart = exact + np.floor(np.log(np.maximum(n, 1) / exact) / math.log(MAX_DISTANCE / exact)
                               * (N_BUCKETS - exact)).astype(np.int64)
    return np.where(n < exact, n, np.minimum(logpart, N_BUCKETS - 1)).astype(np.int32)


def _bias_kernel(rb_ref, idx_ref, o_ref):
    h = pl.program_id(0)
    idx = idx_ref[...]
    far = rb_ref[N_BUCKETS - 1, h]
    acc = jnp.zeros(idx.shape, F32)
    for b in range(N_BUCKETS - 1):
        acc = jnp.where(idx == b, rb_ref[b, h] - far, acc)
    o_ref[0] = acc


def _bias_tables(rel_bias, idx):
    m = idx.shape[0]
    return pl.pallas_call(
        _bias_kernel,
        grid=(N_HEADS,),
        in_specs=[pl.BlockSpec(memory_space=pltpu.SMEM), pl.BlockSpec((m, LANES), lambda h: (0, 0))],
        out_specs=pl.BlockSpec((1, m, LANES), lambda h: (h, 0, 0)),
        out_shape=jax.ShapeDtypeStruct((N_HEADS, m, LANES), F32),
        compiler_params=_cp(("arbitrary",), 32),
        name="t5_bias_tables",
    )(rel_bias, idx)


CMP_NEAR = 32


def _prompt_bias_idx():
    a = np.arange(QT)[:, None]
    b = np.arange(QT)[None, :]
    t0 = _t5_bucket_np(a - b)
    t1 = _t5_bucket_np(QT + a - b)
    m = np.arange(LANES)[None, :]
    gc = np.where(m < CMP_NEAR, _t5_bucket_np(a + (CMP_NEAR // 2) * CMP_STRIDE - (CMP_BLOCK - 1) - CMP_STRIDE * m),
                  N_BUCKETS - 1)
    return np.concatenate([t0, t1, gc], axis=0).astype(np.int32)


def _flatten_pairs(load, nh):
    lane = lax.broadcasted_iota(jnp.int32, (nh, LANES), 1)
    ev, od = [], []
    for m in range(CMP_STRIDE // 2):
        sa = load(2 * m)
        sb = load(2 * m + 1)
        ev.append(jnp.where(lane < HEAD_DIM, sa, pltpu.roll(sb, HEAD_DIM, 1)))
        od.append(jnp.where(lane < HEAD_DIM, pltpu.roll(sa, HEAD_DIM, 1), sb))
    return jnp.concatenate(ev, axis=1), jnp.concatenate(od, axis=1)


def _compress_rows(xf, w1_ref, pe_ref, w2_ref, slot):
    rows = xf.shape[0]
    half = CMP_STRIDE * HEAD_DIM
    first = _mm((xf + pe_ref[slot, 0:1, :half]).astype(BF16), w1_ref[slot, :, :CMP_HIDDEN])
    second = _mm((xf + pe_ref[slot, 0:1, half:]).astype(BF16), w1_ref[slot, :, CMP_HIDDEN:])
    act = jax.nn.gelu(first + pltpu.roll(second, rows - 1, 0))
    return _mm(act.astype(BF16), w2_ref[slot])


def _compress_prompt_kernel(x0, x1, x2, x3, w1_ref, pe_ref, w2_ref, o_ref):
    nh = x0.shape[0] // CMP_STRIDE
    xs = (x0, x1, x2, x3)
    for slot in range(2):
        parts = []
        for c in range(2):
            x_ref = xs[2 * slot + c]
            ev, od = _flatten_pairs(lambda s: x_ref[pl.ds(s, nh, stride=CMP_STRIDE), :], nh)
            parts += [ev, od]
        xf = jnp.concatenate(parts, axis=0)
        o_ref[0, slot] = _compress_rows(xf, w1_ref, pe_ref, w2_ref, slot).astype(o_ref.dtype)


def _compress_weights(phi_pe, phi_w1, phi_w2):
    half = CMP_BLOCK // 2
    w1 = jnp.concatenate([phi_w1[:, :half].reshape(2, half * HEAD_DIM, CMP_HIDDEN),
                          phi_w1[:, half:].reshape(2, half * HEAD_DIM, CMP_HIDDEN)], axis=2)
    pe = jnp.broadcast_to(phi_pe.reshape(2, 1, CMP_BLOCK * HEAD_DIM), (2, 8, CMP_BLOCK * HEAD_DIM))
    return w1.astype(BF16), pe, phi_w2.astype(BF16)


def _compress_prompt(kvf, cw, batch, length):
    w1b, pe, w2b = cw
    nh = length // CMP_STRIDE
    xspec = lambda c: pl.BlockSpec((length, LANES), lambda b: (b, c))
    return pl.pallas_call(
        _compress_prompt_kernel,
        grid=(batch,),
        in_specs=[xspec(0), xspec(1), xspec(2), xspec(3), _resident(w1b.shape), _resident(pe.shape),
                  _resident(w2b.shape)],
        out_specs=pl.BlockSpec((1, 2, N_KV * nh, HEAD_DIM), lambda b: (b, 0, 0, 0)),
        out_shape=jax.ShapeDtypeStruct((batch, 2, N_KV * nh, HEAD_DIM), BF16),
        compiler_params=_cp(("parallel",), 48),
        name="nsa_compress_prompt",
    )(kvf, kvf, kvf, kvf, w1b, pe, w2b)


_NT = (((1,), (1,)), ((), ()))


def _cover_t_np(nc_pad, nb_pad):
    ratio = SEL_BLOCK // CMP_STRIDE
    span = CMP_BLOCK // CMP_STRIDE
    off = (np.arange(ratio)[:, None] - np.arange(span)[None, :]).reshape(-1)
    target = np.arange(nb_pad)[:, None] * ratio + off[None, :]
    cov = np.sum(np.arange(nc_pad)[:, None, None] == target[None], axis=-1)
    return np.ascontiguousarray(cov.T).astype(np.float32)


def _rank_select(score, jidx, nsel):
    nb = score.shape[0]
    cnt = jnp.zeros(score.shape, F32)
    for i in range(nb):
        ri = score[i:i + 1, :]
        cnt = cnt + jnp.where((ri > score) | ((ri == score) & (jidx > i)), 1.0, 0.0)
    return cnt < nsel


def _cmp_attn_kernel(q_ref, kcv_ref, gate_ref, ghi_ref, glo_ref, covt_ref, o_ref, sel_ref, bias_ref, selt_ref):
    i = pl.program_id(1)
    q0 = i * QT
    nc = kcv_ref.shape[2] // N_KV
    nb = covt_ref.shape[0]
    base = q0 // CMP_STRIDE - CMP_NEAR // 2
    rr = lax.broadcasted_iota(jnp.int32, (LANES, nc), 0)
    cc = lax.broadcasted_iota(jnp.int32, (LANES, nc), 1)
    place = ((cc == base + rr) & (rr < CMP_NEAR)).astype(BF16)
    bias_ref[...] = (jnp.dot(ghi_ref[...], place, preferred_element_type=F32)
                     + jnp.dot(glo_ref[...], place, preferred_element_type=F32))
    t = q0 + lax.broadcasted_iota(jnp.int32, (QT, nc), 0)
    cend = lax.broadcasted_iota(jnp.int32, (QT, nc), 1) * CMP_STRIDE + (CMP_BLOCK - 1)
    vis = t >= cend
    visf = vis.astype(F32)
    madd = jnp.where(vis, 0.0, NEG)
    gate = gate_ref[...]
    jidx = lax.broadcasted_iota(jnp.int32, (nb, QT), 0)
    tq = q0 + lax.broadcasted_iota(jnp.int32, (nb, QT), 1)
    jt = tq // SEL_BLOCK
    forced = (jidx == 0) | (jidx == jt) | (jidx == jt - 1)
    visb = jidx * SEL_BLOCK <= tq
    for g in range(N_KV):
        kc = kcv_ref[0, 0, g * nc:(g + 1) * nc, :]
        vc = kcv_ref[0, 1, g * nc:(g + 1) * nc, :]
        psum = jnp.zeros((QT, nc), F32)
        for r in range(GQ):
            h = g * GQ + r
            qh = q_ref[:, h * HEAD_DIM:(h + 1) * HEAD_DIM]
            s = lax.dot_general(qh, kc, _NT, preferred_element_type=F32)
            s = s + bias_ref[h * QT:(h + 1) * QT, :] + madd
            m = jnp.max(s, axis=-1, keepdims=True)
            e = jnp.exp(s - m)
            p = ((e * visf) * (1.0 / jnp.sum(e, axis=-1, keepdims=True))).astype(BF16)
            psum = psum + p.astype(F32)
            o = _mm(p, vc)
            o_ref[:, h * HEAD_DIM:(h + 1) * HEAD_DIM] = (o * gate[:, h:h + 1]).astype(o_ref.dtype)
        imp = lax.dot_general(covt_ref[...], psum, _NT, precision=HIGHEST, preferred_element_type=F32)
        score = jnp.where(visb, jnp.where(forced, FORCE, imp), NEG)
        sel = _rank_select(score, jidx, min(TOP_N, nb)) & visb
        selt_ref[g * nb:(g + 1) * nb, :] = sel.astype(F32)
    sel_ref[...] = selt_ref[...].T


def _cmp_attn(q, kcv, gate, ghi, glo, batch, length):
    nq = length // QT
    nc = kcv.shape[2] // N_KV
    nb = length // SEL_BLOCK
    covt = jnp.asarray(_cover_t_np(nc, nb))
    row = lambda w: pl.BlockSpec((QT, w), lambda b, i: (b * nq + i, 0))
    return pl.pallas_call(
        _cmp_attn_kernel,
        grid=(batch, nq),
        in_specs=[row(Q_COLS), pl.BlockSpec((1, 2, N_KV * nc, HEAD_DIM), lambda b, i: (b, 0, 0, 0)), row(LANES),
                  _resident(ghi.shape), _resident(glo.shape), _resident(covt.shape)],
        out_specs=[row(Q_COLS), row(N_KV * nb)],
        out_shape=[jax.ShapeDtypeStruct((batch * length, Q_COLS), BF16),
                   jax.ShapeDtypeStruct((batch * length, N_KV * nb), F32)],
        scratch_shapes=[pltpu.VMEM((N_HEADS * QT, nc), F32), pltpu.VMEM((N_KV * nb, QT), F32)],
        compiler_params=_cp(("parallel", "arbitrary"), 48),
        name="nsa_cmp_attn_select",
    )(q, kcv, gate, ghi, glo, covt)


def _flash_kernel(*refs, selected):
    if selected:
        q_ref, kv_ref, gate_ref, tab_ref, sel_ref, ex_ref, o_ref, s_ref, mx_ref, mb_ref, ls_ref, acc_ref, madd_ref = refs
    else:
        q_ref, kv_ref, gate_ref, tab_ref, o_ref, s_ref, mx_ref, mb_ref, ls_ref, acc_ref = refs
    i = pl.program_id(1)
    sbase = kv_ref.shape[0] if selected else 0
    ra = lax.broadcasted_iota(jnp.int32, (QT, QT), 0)
    cb = lax.broadcasted_iota(jnp.int32, (QT, QT), 1)
    rep = lambda a: jnp.concatenate([a] * GQ, axis=0)
    causal = rep(jnp.where(ra >= cb, 0.0, NEG))
    edge = rep(jnp.where(ra <= cb, 0.0, NEG))
    gate = gate_ref[...]
    gbase = N_HEADS if selected else 2 * N_HEADS
    wide = 2 * QT
    if selected:
        nbulk = jnp.maximum(i - 1, 0) // 2
        specials = [(i - 2, "far", (i >= 2) & (i % 2 == 0)), (i - 1, "near", i >= 1), (i, "diag", None)]
    else:
        nbulk = None
        specials = [(i - 4, "edge", i >= 4), (i - 3, "far", i >= 3), (i - 2, "far", i >= 2), (i - 1, "near", i >= 1),
                    (i, "diag", None)]

    def guarded(cond, fn):
        if cond is None:
            fn()
        else:
            pl.when(cond)(fn)

    for g in range(N_KV):
        kcol = slice(g * HEAD_DIM, (g + 1) * HEAD_DIM)
        vcol = slice(N_KV * HEAD_DIM + g * HEAD_DIM, N_KV * HEAD_DIM + (g + 1) * HEAD_DIM)
        q4 = jnp.concatenate([q_ref[:, (g * GQ + r) * HEAD_DIM:(g * GQ + r + 1) * HEAD_DIM] for r in range(GQ)],
                             axis=0)
        if selected:
            sel = sel_ref[:, g * ex_ref.shape[0]:(g + 1) * ex_ref.shape[0]].astype(BF16)
            madd_ref[...] = (jnp.dot(sel, ex_ref[...], preferred_element_type=F32) - 1.0) * (-NEG)
        mx_ref[...] = jnp.full(mx_ref.shape, -jnp.inf, F32)
        ls_ref[...] = jnp.zeros(ls_ref.shape, F32)
        acc_ref[...] = jnp.zeros(acc_ref.shape, F32)

        def bulk_logits(c, carry):
            cols = pl.ds(pl.multiple_of(c * wide, wide), wide)
            s = lax.dot_general(q4, kv_ref[cols, kcol], _NT, preferred_element_type=F32) + rep(madd_ref[:, cols])
            s_ref[:, cols] = s
            mx_ref[...] = jnp.maximum(mx_ref[...], jnp.maximum(s[:, :QT], s[:, QT:]))
            return carry

        def tile_logits(j, kind, slot):
            rows = pl.ds(pl.multiple_of(j * QT, QT), QT)
            s = lax.dot_general(q4, kv_ref[rows, kcol], _NT, preferred_element_type=F32)
            if kind in ("diag", "near"):
                k = 0 if kind == "diag" else 1
                s = s + jnp.concatenate([tab_ref[g * GQ + r, k * QT:(k + 1) * QT, :] for r in range(GQ)], axis=0)
            if kind == "diag":
                s = s + causal
            if kind == "edge":
                s = s + edge
            if selected:
                s = s + rep(madd_ref[:, rows])
            s_ref[:, sbase + slot * QT:sbase + (slot + 1) * QT] = s
            mx_ref[...] = jnp.maximum(mx_ref[...], s)

        if selected:
            lax.fori_loop(0, nbulk, bulk_logits, 0)
        for slot, (j, kind, cond) in enumerate(specials):
            guarded(cond, functools.partial(tile_logits, j, kind, slot))

        mb_ref[...] = jnp.broadcast_to(jnp.max(mx_ref[...], axis=-1, keepdims=True), mb_ref.shape)

        def bulk_pv(c, carry):
            cols = pl.ds(pl.multiple_of(c * wide, wide), wide)
            m = mb_ref[...]
            p0 = jnp.exp(s_ref[:, pl.ds(pl.multiple_of(c * wide, wide), QT)] - m)
            p1 = jnp.exp(s_ref[:, pl.ds(pl.multiple_of(c * wide + QT, QT), QT)] - m)
            ls_ref[...] += p0 + p1
            p = jnp.concatenate([p0, p1], axis=1).astype(BF16)
            acc_ref[...] += jnp.dot(p, kv_ref[cols, vcol], preferred_element_type=F32)
            return carry

        def tile_pv(j, slot):
            rows = pl.ds(pl.multiple_of(j * QT, QT), QT)
            p = jnp.exp(s_ref[:, sbase + slot * QT:sbase + (slot + 1) * QT] - mb_ref[...])
            ls_ref[...] += p
            acc_ref[...] += jnp.dot(p.astype(BF16), kv_ref[rows, vcol], preferred_element_type=F32)

        if selected:
            lax.fori_loop(0, nbulk, bulk_pv, 0)
        for slot, (j, kind, cond) in enumerate(specials):
            guarded(cond, functools.partial(tile_pv, j, slot))

        l = jnp.sum(ls_ref[...], axis=-1, keepdims=True)
        for r in range(GQ):
            h = g * GQ + r
            scale = gate[:, gbase + h:gbase + h + 1] / l[r * QT:(r + 1) * QT]
            o_ref[:, h * HEAD_DIM:(h + 1) * HEAD_DIM] = (acc_ref[r * QT:(r + 1) * QT, :] * scale).astype(o_ref.dtype)


def _ex_np(nb, length):
    return (np.arange(nb)[:, None] == (np.arange(length)[None, :] // SEL_BLOCK)).astype(np.float32)


def _flash(q, kvb, gate, tab, selmask, batch, length, selected):
    nq = length // QT
    nb = length // SEL_BLOCK
    row = lambda w: pl.BlockSpec((QT, w), lambda b, i: (b * nq + i, 0))
    in_specs = [row(Q_COLS), pl.BlockSpec((length, 2 * N_KV * HEAD_DIM), lambda b, i: (b, 0 if selected else 1)),
                row(LANES), _resident(tab.shape)]
    args = [q, kvb, gate, tab]
    rows4 = GQ * QT
    n_special = 3 if selected else WINDOW // QT + 1
    scratch = [pltpu.VMEM((rows4, (length if selected else 0) + n_special * QT), F32),
               pltpu.VMEM((rows4, QT), F32),
               pltpu.VMEM((rows4, QT), F32),
               pltpu.VMEM((rows4, QT), F32),
               pltpu.VMEM((rows4, HEAD_DIM), F32)]
    if selected:
        ex = jnp.asarray(_ex_np(nb, length), BF16)
        in_specs += [row(N_KV * nb), _resident(ex.shape)]
        args += [selmask, ex]
        scratch.append(pltpu.VMEM((QT, length), F32))
    return pl.pallas_call(
        functools.partial(_flash_kernel, selected=selected),
        grid=(batch, nq),
        in_specs=in_specs,
        out_specs=row(Q_COLS),
        out_shape=jax.ShapeDtypeStruct((batch * length, Q_COLS), BF16),
        scratch_shapes=scratch,
        compiler_params=_cp(("parallel", "arbitrary"), 48),
        name="nsa_selected_attn" if selected else "nsa_window_attn",
    )(*args)


def _s5_sample_kernel(u_ref, hr_ref, hi_ref, ws_ref, wv_ref, a1_ref, d_ref, g_ref, sre_ref, sim_ref):
    u = u_ref[...]
    x = _mm(u.astype(BF16), ws_ref[0, 0])
    ar, ai = a1_ref[0, :, :S5_SL], a1_ref[0, :, S5_SL:]
    h0r, h0i = hr_ref[...], hi_ref[...]
    hr = x[:, :S5_SL] + (ar * h0r - ai * h0i)
    hi = x[:, S5_SL:] + (ar * h0i + ai * h0r)
    sre_ref[...] = hr
    sim_ref[...] = hi
    y = _mm(jnp.concatenate([hr, hi], axis=1).astype(BF16), wv_ref[0, 0])
    g_ref[...] = jax.nn.gelu(y + d_ref[...] * u).astype(g_ref.dtype)


def _s5_sample(u, h0_re, h0_im, ws, wv, a1, d_skip):
    n = u.shape[0]
    st = pl.BlockSpec((n, S5_SL), lambda s: (0, s))
    return pl.pallas_call(
        _s5_sample_kernel,
        grid=(S5_NSG,),
        in_specs=[pl.BlockSpec((n, LANES), lambda s: (0, s)), st, st,
                  pl.BlockSpec((1, 1, LANES, 2 * S5_SL), lambda s: (S5_T - 1, s, 0, 0)),
                  pl.BlockSpec((1, 1, 2 * S5_SL, LANES), lambda s: (0, s, 0, 0)),
                  pl.BlockSpec((1, 1, 2 * S5_SL), lambda s: (s, 0, 0)),
                  pl.BlockSpec((1, LANES), lambda s: (0, s))],
        out_specs=[pl.BlockSpec((n, LANES), lambda s: (0, s)), st, st],
        out_shape=[jax.ShapeDtypeStruct((n, D_MODEL), BF16),
                   jax.ShapeDtypeStruct((n, N_GROUPS * STATE), F32),
                   jax.ShapeDtypeStruct((n, N_GROUPS * STATE), F32)],
        compiler_params=_cp(("parallel",), 32),
        name="s5_sample",
    )(u, h0_re, h0_im, ws, wv, a1, d_skip.reshape(1, D_MODEL))


def _softmax_lanes(s, valid):
    m = jnp.max(s, axis=-1, keepdims=True)
    e = jnp.exp(s - m)
    return (e * valid) * (1.0 / jnp.sum(e, axis=-1, keepdims=True))


def _own_group(o):
    rowg = lax.broadcasted_iota(jnp.int32, (N_HEADS, HEAD_DIM), 0) // GQ
    out = jnp.zeros((N_HEADS, HEAD_DIM), F32)
    for g in range(N_KV):
        out = out + jnp.where(rowg == g, o[:, g * HEAD_DIM:(g + 1) * HEAD_DIM], 0.0)
    return out


def _nsa_decode_kernel(pt_ref, x_ref, win_ref, qbd_ref, new_ref, gate_ref,
                       w1_ref, pe_ref, w2_ref, cov_ref, bias_ref, ex_ref, o_ref,
                       xt_ref, xf_ref, ks_ref, vs_ref, kw_ref, vw_ref, *, past_len, n_cmp):
    p = pl.program_id(1)
    npages = pl.num_programs(1)
    page = x_ref.shape[2]
    hp = page // CMP_STRIDE
    nh = past_len // CMP_STRIDE
    nkv = N_KV * HEAD_DIM
    sel_cols = ks_ref.shape[1]
    win_cols = kw_ref.shape[1]
    wb = win_ref.shape[2]

    def stash(ref, cols, x):
        ref[:, cols] = x.astype(BF16)

    xt = x_ref[0, :2 * nkv, :].T
    for c in range(2 * nkv // LANES):
        xt_ref[c] = xt[:, c * LANES:(c + 1) * LANES]
    for slot in range(2):
        for c in range(2):
            blk = 2 * slot + c
            ev, od = _flatten_pairs(lambda s: xt_ref[blk, pl.ds(s, hp, stride=CMP_STRIDE), :], hp)
            for k, val in ((2 * c, ev), (2 * c + 1, od)):
                xf_ref[slot, pl.ds(pl.multiple_of(k * nh + p * hp, hp), hp), :] = val
    cols = pl.ds(pl.multiple_of(p * page, page), page)
    stash(ks_ref, cols, x_ref[0, 2 * nkv:3 * nkv, :])
    stash(vs_ref, cols, x_ref[0, 3 * nkv:, :])

    @pl.when(p == npages - 1)
    def _():
        qbd = qbd_ref[0]
        new = new_ref[0]
        lane0 = lax.broadcasted_iota(jnp.int32, (nkv, LANES), 1) == 0
        newcol = lambda k: jnp.where(lane0, jnp.broadcast_to(new[:, k:k + 1], (nkv, LANES)), 0.0)
        stash(ks_ref, slice(past_len, sel_cols), newcol(0))
        stash(vs_ref, slice(past_len, sel_cols), newcol(1))
        stash(kw_ref, slice(0, wb), win_ref[0, :nkv, :])
        stash(vw_ref, slice(0, wb), win_ref[0, nkv:, :])
        stash(kw_ref, slice(wb, win_cols), newcol(2))
        stash(vw_ref, slice(wb, win_cols), newcol(3))
        rowh = lax.broadcasted_iota(jnp.int32, (N_HEADS, LANES), 0)

        kc = _compress_rows(xf_ref[0], w1_ref, pe_ref, w2_ref, 0).astype(BF16)
        vc = _compress_rows(xf_ref[1], w1_ref, pe_ref, w2_ref, 1).astype(BF16)
        s = jnp.zeros((N_HEADS, nh), F32)
        for g in range(N_KV):
            s = s + lax.dot_general(qbd[:, g * HEAD_DIM:(g + 1) * HEAD_DIM], kc[g * nh:(g + 1) * nh], _NT,
                                    preferred_element_type=F32)
        nidx = lax.broadcasted_iota(jnp.int32, (N_HEADS, nh), 1)
        s = s + bias_ref[:, :nh] + jnp.where(nidx < n_cmp, 0.0, NEG)
        pc = _softmax_lanes(s, (nidx < n_cmp).astype(F32)).astype(BF16)
        o_c = _own_group(jnp.concatenate([_mm(pc, vc[g * nh:(g + 1) * nh]) for g in range(N_KV)], axis=1))

        pcf = pc.astype(F32)
        psum = jnp.concatenate([jnp.sum(pcf[g * GQ:(g + 1) * GQ], axis=0, keepdims=True) for g in range(N_KV)]
                               + [jnp.zeros((8 - N_KV, nh), F32)], axis=0)
        imp = jnp.dot(psum, cov_ref[...], precision=HIGHEST, preferred_element_type=F32)
        jt = past_len // SEL_BLOCK
        nb = jt + 1
        lane = lax.broadcasted_iota(jnp.int32, (LANES, LANES), 1)
        subl = lax.broadcasted_iota(jnp.int32, (LANES, LANES), 0)

        def scored(x, j):
            forced = (j == 0) | (j == jt) | (j == jt - 1)
            return jnp.where(j < nb, jnp.where(forced, FORCE, x), NEG)

        imp_sq = jnp.concatenate([imp, jnp.zeros((LANES - 8, LANES), F32)], axis=0)
        sc_row = scored(imp_sq, lane)
        sc_col = scored(imp_sq.T, subl)
        selh = jnp.zeros((N_HEADS, LANES), F32)
        for g in range(N_KV):
            col = sc_col[:, g:g + 1]
            row = sc_row[g:g + 1, :]
            ahead = (col > row) | ((col == row) & (subl < lane))
            rank = jnp.sum(jnp.where(ahead, 1.0, 0.0), axis=0, keepdims=True)
            chosen = jnp.where((rank < min(TOP_N, nb)) & (lane[:1] < nb), 1.0, 0.0)
            selh = selh + jnp.where(rowh // GQ == g, chosen, 0.0)

        pos = lax.broadcasted_iota(jnp.int32, (N_HEADS, sel_cols), 1)
        inblk = jnp.dot(selh.astype(BF16), ex_ref[...], preferred_element_type=F32)
        ok_s = (inblk > 0.5) & (pos <= past_len)
        s = _mm(qbd, ks_ref[...]) + bias_ref[:, nh:nh + sel_cols]
        ps = _softmax_lanes(jnp.where(ok_s, s, NEG), ok_s.astype(F32)).astype(BF16)
        o_s = _own_group(lax.dot_general(ps, vs_ref[...], _NT, preferred_element_type=F32))

        posw = lax.broadcasted_iota(jnp.int32, (N_HEADS, win_cols), 1)
        ok_w = posw <= wb
        s = _mm(qbd, kw_ref[...]) + bias_ref[:, nh + sel_cols:nh + sel_cols + win_cols]
        pw = _softmax_lanes(jnp.where(ok_w, s, NEG), ok_w.astype(F32)).astype(BF16)
        o_w = _own_group(lax.dot_general(pw, vw_ref[...], _NT, preferred_element_type=F32))

        gate = gate_ref[0]
        o_ref[0] = gate[:, 0:1] * o_c + gate[:, 1:2] * o_s + gate[:, 2:3] * o_w


def _decode_bias_idx(past_len, wb, ncp, sel_cols, win_cols):
    n = np.arange(ncp)
    cmp_d = past_len - (n * CMP_STRIDE + CMP_BLOCK - 1)
    sel_d = past_len - np.arange(sel_cols)
    win_d = np.concatenate([wb - np.arange(wb), np.zeros(win_cols - wb, np.int64)])
    idx = _t5_bucket_np(np.concatenate([cmp_d, sel_d, win_d]))
    pad = -len(idx) % (8 * LANES)
    return np.concatenate([idx, np.full(pad, N_BUCKETS - 1)]).reshape(-1, LANES).astype(np.int32)


def _nsa_decode(cache_kv, cache_win, page_table, qbd, new_cols, gate3, cw, rel_bias):
    n_pool, page = cache_kv.shape[:2]
    nseq, npages = page_table.shape
    past_len = npages * page
    wb = cache_win.shape[1]
    w1b, pe, w2b = cw
    nh = past_len // CMP_STRIDE
    n_cmp = (past_len + 1 - CMP_BLOCK) // CMP_STRIDE + 1
    sel_cols = past_len + LANES
    win_cols = wb + LANES
    nkv = N_KV * HEAD_DIM
    kvt = cache_kv.transpose(0, 2, 3, 4, 1).reshape(n_pool, 4 * nkv, page)
    wint = cache_win.transpose(0, 2, 3, 4, 1).reshape(nseq, 2 * nkv, wb)
    cov = jnp.asarray(_cover_t_np(LANES, LANES).T)
    idx = _decode_bias_idx(past_len, wb, nh, sel_cols, win_cols)
    bias = _bias_tables(rel_bias, jnp.asarray(idx)).reshape(N_HEADS, idx.size)
    ex = jnp.asarray((np.arange(LANES)[:, None] == np.arange(sel_cols)[None, :] // SEL_BLOCK).astype(np.float32), BF16)
    const = lambda a: pl.BlockSpec(a.shape, lambda b, p, pt: (0,) * a.ndim)
    grid_spec = pltpu.PrefetchScalarGridSpec(
        num_scalar_prefetch=1,
        grid=(nseq, npages),
        in_specs=[pl.BlockSpec((1, 4 * nkv, page), lambda b, p, pt: (pt[b, p], 0, 0)),
                  pl.BlockSpec((1, 2 * nkv, wb), lambda b, p, pt: (b, 0, 0)),
                  pl.BlockSpec((1, N_HEADS, nkv), lambda b, p, pt: (b, 0, 0)),
                  pl.BlockSpec((1, nkv, LANES), lambda b, p, pt: (b, 0, 0)),
                  pl.BlockSpec((1, N_HEADS, LANES), lambda b, p, pt: (b, 0, 0)),
                  const(w1b), const(pe), const(w2b), const(cov), const(bias), const(ex)],
        out_specs=pl.BlockSpec((1, N_HEADS, HEAD_DIM), lambda b, p, pt: (b, 0, 0)),
        scratch_shapes=[pltpu.VMEM((2 * nkv // LANES, page, LANES), F32),
                        pltpu.VMEM((2, N_KV * nh, CMP_STRIDE * HEAD_DIM), F32),
                        pltpu.VMEM((nkv, sel_cols), BF16), pltpu.VMEM((nkv, sel_cols), BF16),
                        pltpu.VMEM((nkv, win_cols), BF16), pltpu.VMEM((nkv, win_cols), BF16)])
    out = pl.pallas_call(
        functools.partial(_nsa_decode_kernel, past_len=past_len, n_cmp=n_cmp),
        grid_spec=grid_spec,
        out_shape=jax.ShapeDtypeStruct((nseq, N_HEADS, HEAD_DIM), F32),
        compiler_params=_cp(("arbitrary", "arbitrary"), 56),
        name="nsa_decode",
    )(page_table, kvt, wint, qbd, new_cols, gate3, w1b, pe, w2b, cov, bias, ex)
    return out.reshape(nseq, Q_COLS)


def _decode_operands(qs, kvbs, gates):
    nseq = qs.shape[0]
    nkv = N_KV * HEAD_DIM
    own = (np.arange(N_HEADS)[:, None] // GQ) == (np.arange(nkv)[None, :] // HEAD_DIM)
    qbd = jnp.where(own[None], jnp.tile(qs.reshape(nseq, N_HEADS, HEAD_DIM), (1, 1, N_KV)), 0.0)
    new_cols = jnp.pad(kvbs.astype(F32).reshape(nseq, 4, nkv).transpose(0, 2, 1), ((0, 0), (0, 0), (0, LANES - 4)))
    gate3 = jnp.pad(gates[:, :GATE_COLS].reshape(nseq, 3, N_HEADS).transpose(0, 2, 1),
                    ((0, 0), (0, 0), (0, LANES - 3)))
    return qbd, new_cols, gate3


PROMPT_TM = 512


def kernel(x_prompt, x_sample, state_s5_re, state_s5_im, cache_kv, cache_win, page_table, rel_bias, norm_mix, norm_ffn, norm_final, s5_lam_re, s5_lam_im, s5_log_dt, s5_b_re, s5_b_im, s5_c_re, s5_c_im, s5_d, s5_w_glu, ffn_w_in, ffn_w_out, nsa_w_in, nsa_phi_pe, nsa_phi_w1, nsa_phi_w2, nsa_w_o, moe_router, moe_w_in, moe_w_out):
    batch, length, _ = x_prompt.shape
    nseq = x_sample.shape[0]
    xp = x_prompt.reshape(batch * length, D_MODEL)
    xs = x_sample.reshape(nseq, D_MODEL)
    nstate = N_GROUPS * STATE

    wk, ws, wv, apow, a1 = _s5_prep(s5_lam_re, s5_lam_im, s5_log_dt, s5_b_re, s5_b_im, s5_c_re, s5_c_im)
    w_glu = s5_w_glu.astype(BF16)
    ffn_in, ffn_out = ffn_w_in.astype(BF16), ffn_w_out.astype(BF16)
    nsa_in = jnp.pad(nsa_w_in, ((0, 0), (0, NSA_PAD - nsa_w_in.shape[1]))).astype(BF16)
    w_o = nsa_w_o.astype(BF16)
    router = jnp.pad(moe_router, ((0, 0), (0, LANES - N_EXPERTS))).astype(BF16)
    moe_in, moe_out = moe_w_in.astype(BF16), moe_w_out.astype(BF16)
    cw = _compress_weights(nsa_phi_pe, nsa_phi_w1, nsa_phi_w2)
    tabs = _bias_tables(rel_bias, jnp.asarray(_prompt_bias_idx()))
    tab = tabs[:, :2 * QT]
    gflat = tabs[:, 2 * QT:].reshape(N_HEADS * QT, LANES)
    ghi = gflat.astype(BF16)
    glo = (gflat - ghi.astype(F32)).astype(BF16)

    up = _rmsnorm(xp, norm_mix[0], PROMPT_TM)
    gp, sre_p, sim_p = _s5_prompt(up, s5_d, wk, ws, wv, apow, batch, length)
    y1p, h1p = _glu(gp, xp, w_glu, norm_ffn[0], PROMPT_TM)
    y2p, h2p = _ffn(h1p, y1p, ffn_in, ffn_out, norm_mix[1], PROMPT_TM)
    q, kvf, winf, kvb, gate = _nsa_proj(h2p, nsa_in, PROMPT_TM)
    kcv = _compress_prompt(kvf, cw, batch, length)
    oc, sel = _cmp_attn(q, kcv, gate, ghi, glo, batch, length)
    os_ = _flash(q, kvb, gate, tab, sel, batch, length, True)
    ow = _flash(q, kvb, gate, tab, None, batch, length, False)
    y3p, h3p, mgp = _merge(oc, os_, ow, y2p, w_o, norm_ffn[1], router, PROMPT_TM)
    yp = _moe(h3p, y3p, mgp, moe_in, moe_out, norm_final, PROMPT_TM)

    us = _rmsnorm(xs, norm_mix[0], nseq)
    gs, sre_s, sim_s = _s5_sample(us, state_s5_re.reshape(nseq, nstate), state_s5_im.reshape(nseq, nstate),
                                  ws, wv, a1, s5_d)
    y1s, h1s = _glu(gs, xs, w_glu, norm_ffn[0], nseq)
    y2s, h2s = _ffn(h1s, y1s, ffn_in, ffn_out, norm_mix[1], nseq)
    qs, kvfs, winfs, kvbs, gates = _nsa_proj(h2s, nsa_in, nseq)
    qbd, new_cols, gate3 = _decode_operands(qs, kvbs, gates)
    od = _nsa_decode(cache_kv, cache_win, page_table, qbd, new_cols, gate3, cw, rel_bias)
    zero = jnp.zeros((nseq, D_MODEL), F32)
    y3s, h3s, mgs = _merge(od, zero, zero, y2s, w_o, norm_ffn[1], router, nseq)
    ys = _moe(h3s, y3s, mgs, moe_in, moe_out, norm_final, nseq)

    wp = min(WINDOW, length)
    return (yp.reshape(batch, length, D_MODEL),
            ys.reshape(nseq, 1, D_MODEL),
            sre_p.reshape(batch, N_GROUPS, STATE), sim_p.reshape(batch, N_GROUPS, STATE),
            kvf.reshape(batch, length, 4, N_KV, HEAD_DIM),
            winf.reshape(batch, length, 2, N_KV, HEAD_DIM)[:, length - wp:],
            sre_s.reshape(nseq, N_GROUPS, STATE), sim_s.reshape(nseq, N_GROUPS, STATE),
            kvfs.reshape(nseq, 1, 4, N_KV, HEAD_DIM),
            winfs.reshape(nseq, 1, 2, N_KV, HEAD_DIM))
```

```python
import functools
import math

import numpy as np
import jax
import jax.numpy as jnp
from jax import lax
from jax.experimental import pallas as pl
from jax.experimental.pallas import tpu as pltpu

F32 = jnp.float32
BF16 = jnp.bfloat16
HIGHEST = lax.Precision.HIGHEST

D_MODEL = 1024
GROUP_CH = 16
N_GROUPS = D_MODEL // GROUP_CH
STATE = 64
N_HEADS = 16
HEAD_DIM = 64
N_KV = 4
GQ = N_HEADS // N_KV
CMP_BLOCK = 32
CMP_STRIDE = 16
CMP_HIDDEN = 2 * HEAD_DIM
SEL_BLOCK = 64
TOP_N = 16
WINDOW = 512
Q_COLS = N_HEADS * HEAD_DIM
KV_COLS = 6 * N_KV * HEAD_DIM
GATE_COLS = 3 * N_HEADS
N_BUCKETS = 32
MAX_DISTANCE = 128
D_FF = 2816
N_EXPERTS = 8
EXPERT_FF = 1408
EPS = 1e-6
NEG = -1e30
FORCE = 1e4

LANES = 128
S5_T = 8
S5_SG = LANES // GROUP_CH
S5_NSG = N_GROUPS // S5_SG
S5_SL = S5_SG * STATE
QT = 128


def _cp(sem, vmem_mb):
    return pltpu.CompilerParams(dimension_semantics=sem, vmem_limit_bytes=vmem_mb << 20)


def _cmul(ar, ai, br, bi):
    return ar * br - ai * bi, ar * bi + ai * br


def _s5_prep_kernel(lr_r, li_r, ldt_r, br_r, bi_r, lr_c, li_c, ldt_c, cr_c, ci_c,
                    wk_ref, ws_ref, wv_ref, apow_ref, a1_ref):
    def disc(lr, li, ldt):
        dt = jnp.exp(ldt)
        mag = jnp.exp(lr * dt)
        abr = mag * jnp.cos(li * dt)
        abi = mag * jnp.sin(li * dt)
        den = lr * lr + li * li
        nr = abr - 1.0
        fr = (nr * lr + abi * li) / den
        fi = (abi * lr - nr * li) / den
        return abr, abi, fr, fi

    abr, abi, fr, fi = disc(lr_r[0], li_r[0], ldt_r[0])
    rg = lax.broadcasted_iota(jnp.int32, (LANES, S5_SL), 0) // GROUP_CH
    cg = lax.broadcasted_iota(jnp.int32, (LANES, S5_SL), 1) // STATE
    mask_r = (rg == cg).astype(F32)
    b_r, b_i = br_r[...], bi_r[...]
    bbr = (fr * b_r - fi * b_i) * mask_r
    bbi = (fr * b_i + fi * b_r) * mask_r
    abr_c, abi_c, _, _ = disc(lr_c[0], li_c[0], ldt_c[0])
    rg2 = lax.broadcasted_iota(jnp.int32, (S5_SL, LANES), 0) // STATE
    cg2 = lax.broadcasted_iota(jnp.int32, (S5_SL, LANES), 1) // GROUP_CH
    mask_c = (rg2 == cg2).astype(F32)
    c_r = cr_c[0] * mask_c
    c_i = ci_c[0] * mask_c

    a1_ref[0, :, :S5_SL] = abr
    a1_ref[0, :, S5_SL:] = abi

    pr = jnp.ones_like(abr)
    pi = jnp.zeros_like(abr)
    pr_c = jnp.ones_like(abr_c)
    pi_c = jnp.zeros_like(abr_c)
    for tau in range(S5_T + 1):
        wv_ref[tau, 0, :S5_SL, :] = (c_r * pr_c - c_i * pi_c).astype(BF16)
        wv_ref[tau, 0, S5_SL:, :] = (-(c_r * pi_c + c_i * pr_c)).astype(BF16)
        if tau < S5_T:
            er, ei = _cmul(pr, pi, bbr, bbi)
            s = S5_T - 1 - tau
            ws_ref[s, 0, :, :S5_SL] = er.astype(BF16)
            ws_ref[s, 0, :, S5_SL:] = ei.astype(BF16)
            k = (jnp.dot(er, c_r, precision=HIGHEST, preferred_element_type=F32)
                 - jnp.dot(ei, c_i, precision=HIGHEST, preferred_element_type=F32))
            wk_ref[tau, 0] = k.astype(BF16)
        pr, pi = _cmul(pr, pi, abr, abi)
        pr_c, pi_c = _cmul(pr_c, pi_c, abr_c, abi_c)
    tr, ti = jnp.ones_like(abr), jnp.zeros_like(abr)
    for _ in range(S5_T):
        tr, ti = _cmul(tr, ti, abr, abi)
    qr, qi = jnp.ones_like(abr), jnp.zeros_like(abr)
    for i in range(S5_T + 1):
        apow_ref[0, i:i + 1, :S5_SL] = qr
        apow_ref[0, i:i + 1, S5_SL:] = qi
        qr, qi = _cmul(qr, qi, tr, ti)


def _s5_prep(lam_re, lam_im, log_dt, b_re, b_im, c_re, c_im):
    n = S5_NSG
    row = lambda a: a.reshape(n, 1, S5_SL)
    col = lambda a: a.reshape(n, S5_SL, 1)
    ldt = jnp.repeat(log_dt, STATE)
    bt = lambda b: jnp.tile(b.transpose(0, 2, 1).reshape(D_MODEL, STATE), (1, S5_SG))
    ct = lambda c: jnp.tile(c.reshape(n, LANES, STATE).transpose(0, 2, 1), (1, S5_SG, 1))
    rspec = pl.BlockSpec((1, 1, S5_SL), lambda s: (s, 0, 0))
    cspec = pl.BlockSpec((1, S5_SL, 1), lambda s: (s, 0, 0))
    t1 = S5_T + 1
    return pl.pallas_call(
        _s5_prep_kernel,
        grid=(n,),
        in_specs=[rspec, rspec, rspec,
                  pl.BlockSpec((LANES, S5_SL), lambda s: (s, 0)), pl.BlockSpec((LANES, S5_SL), lambda s: (s, 0)),
                  cspec, cspec, cspec,
                  pl.BlockSpec((1, S5_SL, LANES), lambda s: (s, 0, 0)), pl.BlockSpec((1, S5_SL, LANES), lambda s: (s, 0, 0))],
        out_specs=[pl.BlockSpec((S5_T, 1, LANES, LANES), lambda s: (0, s, 0, 0)),
                   pl.BlockSpec((S5_T, 1, LANES, 2 * S5_SL), lambda s: (0, s, 0, 0)),
                   pl.BlockSpec((t1, 1, 2 * S5_SL, LANES), lambda s: (0, s, 0, 0)),
                   pl.BlockSpec((1, t1 + 7, 2 * S5_SL), lambda s: (s, 0, 0)),
                   pl.BlockSpec((1, 1, 2 * S5_SL), lambda s: (s, 0, 0))],
        out_shape=[jax.ShapeDtypeStruct((S5_T, n, LANES, LANES), BF16),
                   jax.ShapeDtypeStruct((S5_T, n, LANES, 2 * S5_SL), BF16),
                   jax.ShapeDtypeStruct((t1, n, 2 * S5_SL, LANES), BF16),
                   jax.ShapeDtypeStruct((n, t1 + 7, 2 * S5_SL), F32),
                   jax.ShapeDtypeStruct((n, 1, 2 * S5_SL), F32)],
        compiler_params=_cp(("arbitrary",), 48),
        name="s5_prep",
    )(row(lam_re), row(lam_im), row(ldt), bt(b_re), bt(b_im),
      col(lam_re), col(lam_im), col(ldt), ct(c_re), ct(c_im))


def _rms(x, g):
    return x * lax.rsqrt(jnp.mean(x * x, axis=-1, keepdims=True) + EPS) * g


def _rmsnorm_kernel(x_ref, g_ref, o_ref):
    o_ref[...] = _rms(x_ref[...], g_ref[...]).astype(o_ref.dtype)


def _rmsnorm(x, g, tm, dtype=F32):
    n = x.shape[0]
    return pl.pallas_call(
        _rmsnorm_kernel,
        grid=(n // tm,),
        in_specs=[pl.BlockSpec((tm, D_MODEL), lambda i: (i, 0)), pl.BlockSpec((1, D_MODEL), lambda i: (0, 0))],
        out_specs=pl.BlockSpec((tm, D_MODEL), lambda i: (i, 0)),
        out_shape=jax.ShapeDtypeStruct((n, D_MODEL), dtype),
        compiler_params=_cp(("parallel",), 32),
        name="rmsnorm",
    )(x, g.reshape(1, D_MODEL))


def _s5_prompt_kernel(u_ref, d_ref, wk_ref, ws_ref, wv_ref, apow_ref, g_ref, sre_ref, sim_ref,
                      ub_ref, s_ref, p_ref, gc_ref, hs_ref, y_ref):
    length = u_ref.shape[0]
    r = length // S5_T
    ng = r // 8

    acc = jnp.zeros((r, 2 * S5_SL), F32)
    for s in range(S5_T):
        us = u_ref[pl.ds(s, r, stride=S5_T), :].astype(BF16)
        ub_ref[s] = us
        acc = acc + jnp.dot(us, ws_ref[s, 0], preferred_element_type=F32)
    for c in range(2 * S5_SL // LANES):
        s_ref[c] = acc[:, c * LANES:(c + 1) * LANES]

    nsl = S5_SL // LANES
    for c in range(nsl):
        lr = slice(c * LANES, (c + 1) * LANES)
        li = slice(S5_SL + c * LANES, S5_SL + (c + 1) * LANES)

        def apw(i):
            return apow_ref[0, i:i + 1, lr], apow_ref[0, i:i + 1, li]

        a1r, a1i = apw(1)
        pr = s_ref[c, pl.ds(0, ng, stride=8), :]
        pi = s_ref[nsl + c, pl.ds(0, ng, stride=8), :]
        p_ref[0, c] = pr
        p_ref[0, nsl + c] = pi
        for i in range(1, 8):
            qr, qi = _cmul(a1r, a1i, pr, pi)
            pr = qr + s_ref[c, pl.ds(i, ng, stride=8), :]
            pi = qi + s_ref[nsl + c, pl.ds(i, ng, stride=8), :]
            p_ref[i, c] = pr
            p_ref[i, nsl + c] = pi

        a8r, a8i = apw(8)

        def carry(m, g):
            gr, gi = g
            gc_ref[c, pl.ds(m, 1), :] = gr
            gc_ref[nsl + c, pl.ds(m, 1), :] = gi
            nr, ni = _cmul(a8r, a8i, gr, gi)
            return (nr + p_ref[7, c, pl.ds(m, 1), :], ni + p_ref[7, nsl + c, pl.ds(m, 1), :])

        z = jnp.zeros((1, LANES), F32)
        er, ei = lax.fori_loop(0, ng, carry, (z, z))
        sre_ref[0, :, lr] = er
        sim_ref[0, :, lr] = ei

        gr = gc_ref[c]
        gi = gc_ref[nsl + c]
        hs_ref[c, pl.ds(0, ng, stride=8), :] = gr
        hs_ref[nsl + c, pl.ds(0, ng, stride=8), :] = gi
        for i in range(1, 8):
            air, aii = apw(i)
            hr, hi = _cmul(air, aii, gr, gi)
            hs_ref[c, pl.ds(i, ng, stride=8), :] = hr + p_ref[i - 1, c]
            hs_ref[nsl + c, pl.ds(i, ng, stride=8), :] = hi + p_ref[i - 1, nsl + c]

    hs = jnp.concatenate([hs_ref[c] for c in range(2 * S5_SL // LANES)], axis=1).astype(BF16)
    for t in range(S5_T):
        o = jnp.dot(hs, wv_ref[t + 1, 0], preferred_element_type=F32)
        for s in range(t + 1):
            o = o + jnp.dot(ub_ref[s], wk_ref[t - s, 0], preferred_element_type=F32)
        y_ref[pl.ds(t, r, stride=S5_T), :] = o
    y = y_ref[...] + d_ref[...] * u_ref[...]
    g_ref[...] = jax.nn.gelu(y).astype(g_ref.dtype)


def _s5_prompt(u, d_skip, wk, ws, wv, apow, batch, length):
    r = length // S5_T
    t1 = S5_T + 1
    return pl.pallas_call(
        _s5_prompt_kernel,
        grid=(S5_NSG, batch),
        in_specs=[pl.BlockSpec((length, LANES), lambda s, b: (b, s)),
                  pl.BlockSpec((1, LANES), lambda s, b: (0, s)),
                  pl.BlockSpec((S5_T, 1, LANES, LANES), lambda s, b: (0, s, 0, 0)),
                  pl.BlockSpec((S5_T, 1, LANES, 2 * S5_SL), lambda s, b: (0, s, 0, 0)),
                  pl.BlockSpec((t1, 1, 2 * S5_SL, LANES), lambda s, b: (0, s, 0, 0)),
                  pl.BlockSpec((1, t1 + 7, 2 * S5_SL), lambda s, b: (s, 0, 0))],
        out_specs=[pl.BlockSpec((length, LANES), lambda s, b: (b, s)),
                   pl.BlockSpec((1, 1, S5_SL), lambda s, b: (b, 0, s)),
                   pl.BlockSpec((1, 1, S5_SL), lambda s, b: (b, 0, s))],
        out_shape=[jax.ShapeDtypeStruct((batch * length, D_MODEL), BF16),
                   jax.ShapeDtypeStruct((batch, 1, N_GROUPS * STATE), F32),
                   jax.ShapeDtypeStruct((batch, 1, N_GROUPS * STATE), F32)],
        scratch_shapes=[pltpu.VMEM((S5_T, r, LANES), BF16),
                        pltpu.VMEM((2 * S5_SL // LANES, r, LANES), F32),
                        pltpu.VMEM((8, 2 * S5_SL // LANES, r // 8, LANES), F32),
                        pltpu.VMEM((2 * S5_SL // LANES, r // 8, LANES), F32),
                        pltpu.VMEM((2 * S5_SL // LANES, r, LANES), F32),
                        pltpu.VMEM((length, LANES), F32)],
        compiler_params=_cp(("arbitrary", "arbitrary"), 56),
        name="s5_prompt",
    )(u, d_skip.reshape(1, D_MODEL), wk, ws, wv, apow)


def _resident(shape):
    nd = len(shape)
    return pl.BlockSpec(shape, lambda *_: (0,) * nd, pipeline_mode=pl.Buffered(1))


def _rows(tm, width):
    return pl.BlockSpec((tm, width), lambda i: (i, 0))


def _mm(a, b):
    return jnp.dot(a, b, preferred_element_type=F32)


def _glu_kernel(g_ref, x_ref, w_ref, gain_ref, y_ref, h_ref):
    p = _mm(g_ref[...], w_ref[...])
    y = x_ref[...] + p[:, :D_MODEL] * jax.nn.sigmoid(p[:, D_MODEL:])
    y_ref[...] = y
    h_ref[...] = _rms(y, gain_ref[...]).astype(h_ref.dtype)


def _glu(g, x, w_glu, gain, tm):
    n = g.shape[0]
    return pl.pallas_call(
        _glu_kernel,
        grid=(n // tm,),
        in_specs=[_rows(tm, D_MODEL), _rows(tm, D_MODEL), _resident((D_MODEL, 2 * D_MODEL)), _resident((1, D_MODEL))],
        out_specs=[_rows(tm, D_MODEL), _rows(tm, D_MODEL)],
        out_shape=[jax.ShapeDtypeStruct((n, D_MODEL), F32), jax.ShapeDtypeStruct((n, D_MODEL), g.dtype)],
        compiler_params=_cp(("parallel",), 48),
        name="s5_glu",
    )(g, x, w_glu, gain.reshape(1, D_MODEL))


def _chunks(total, size):
    return [(s, min(size, total - s)) for s in range(0, total, size)]


def _swiglu_acc(h, win_ref, wout_ref, ff, lead=()):
    acc = None
    for s, w in _chunks(ff, 256):
        a = jnp.dot(h, win_ref[lead + (slice(None), slice(s, s + w))], preferred_element_type=F32)
        b = jnp.dot(h, win_ref[lead + (slice(None), slice(ff + s, ff + s + w))], preferred_element_type=F32)
        act = (jax.nn.silu(a) * b).astype(BF16)
        z = jnp.dot(act, wout_ref[lead + (slice(s, s + w), slice(None))], preferred_element_type=F32)
        acc = z if acc is None else acc + z
    return acc


def _ffn_kernel(h_ref, y_ref, win_ref, wout_ref, gain_ref, yo_ref, ho_ref):
    y = y_ref[...] + _swiglu_acc(h_ref[...], win_ref, wout_ref, D_FF)
    yo_ref[...] = y
    ho_ref[...] = _rms(y, gain_ref[...]).astype(ho_ref.dtype)


def _ffn(h, y, w_in, w_out, gain, tm):
    n = h.shape[0]
    return pl.pallas_call(
        _ffn_kernel,
        grid=(n // tm,),
        in_specs=[_rows(tm, D_MODEL), _rows(tm, D_MODEL), _resident((D_MODEL, 2 * D_FF)), _resident((D_FF, D_MODEL)),
                  _resident((1, D_MODEL))],
        out_specs=[_rows(tm, D_MODEL), _rows(tm, D_MODEL)],
        out_shape=[jax.ShapeDtypeStruct((n, D_MODEL), F32), jax.ShapeDtypeStruct((n, D_MODEL), BF16)],
        compiler_params=_cp(("parallel",), 56),
        name="ffn_swiglu",
    )(h, y, w_in, w_out, gain.reshape(1, D_MODEL))


NSA_PAD = 2688
KV_OFF = Q_COLS


def _nsa_proj_kernel(h_ref, w_ref, q_ref, kvf_ref, winf_ref, kvb_ref, gate_ref):
    p = _mm(h_ref[...], w_ref[...])
    q_ref[...] = (p[:, :Q_COLS] * (HEAD_DIM ** -0.5)).astype(q_ref.dtype)
    kvf_ref[...] = p[:, KV_OFF:KV_OFF + 1024]
    winf_ref[...] = p[:, KV_OFF + 1024:KV_OFF + 1536]
    kvb_ref[...] = p[:, KV_OFF + 512:KV_OFF + 1536].astype(kvb_ref.dtype)
    gate_ref[...] = jax.nn.sigmoid(p[:, KV_OFF + 1536:])


def _nsa_proj(h, w_in_pad, tm):
    n = h.shape[0]
    return pl.pallas_call(
        _nsa_proj_kernel,
        grid=(n // tm,),
        in_specs=[_rows(tm, D_MODEL), _resident((D_MODEL, NSA_PAD))],
        out_specs=[_rows(tm, Q_COLS), _rows(tm, 1024), _rows(tm, 512), _rows(tm, 1024), _rows(tm, LANES)],
        out_shape=[jax.ShapeDtypeStruct((n, Q_COLS), h.dtype),
                   jax.ShapeDtypeStruct((n, 1024), F32),
                   jax.ShapeDtypeStruct((n, 512), F32),
                   jax.ShapeDtypeStruct((n, 1024), h.dtype),
                   jax.ShapeDtypeStruct((n, LANES), F32)],
        compiler_params=_cp(("parallel",), 48),
        name="nsa_proj",
    )(h, w_in_pad)


def _merge_kernel(oc_ref, os_ref, ow_ref, y_ref, wo_ref, gain_ref, router_ref, yo_ref, ho_ref, mg_ref):
    o = (oc_ref[...].astype(F32) + os_ref[...].astype(F32) + ow_ref[...].astype(F32)).astype(wo_ref.dtype)
    y = y_ref[...] + _mm(o, wo_ref[...])
    yo_ref[...] = y
    hf = _rms(y, gain_ref[...])
    ho_ref[...] = hf.astype(ho_ref.dtype)
    logits = _mm(hf.astype(BF16), router_ref[...])
    lane = lax.broadcasted_iota(jnp.int32, logits.shape, 1)
    logits = jnp.where(lane < N_EXPERTS, logits, -jnp.inf)
    m1 = jnp.max(logits, axis=-1, keepdims=True)
    i1 = jnp.min(jnp.where(logits == m1, lane, LANES), axis=-1, keepdims=True)
    rest = jnp.where(lane == i1, -jnp.inf, logits)
    m2 = jnp.max(rest, axis=-1, keepdims=True)
    i2 = jnp.min(jnp.where(rest == m2, lane, LANES), axis=-1, keepdims=True)
    e2 = jnp.exp(m2 - m1)
    den = 1.0 + e2
    mg_ref[...] = jnp.where(lane == i1, 1.0 / den, 0.0) + jnp.where(lane == i2, e2 / den, 0.0)


def _merge(oc, os_, ow, y, w_o, gain, router_pad, tm):
    n = y.shape[0]
    return pl.pallas_call(
        _merge_kernel,
        grid=(n // tm,),
        in_specs=[_rows(tm, D_MODEL)] * 4 + [_resident((D_MODEL, D_MODEL)), _resident((1, D_MODEL)),
                                             _resident((D_MODEL, LANES))],
        out_specs=[_rows(tm, D_MODEL), _rows(tm, D_MODEL), _rows(tm, LANES)],
        out_shape=[jax.ShapeDtypeStruct((n, D_MODEL), F32), jax.ShapeDtypeStruct((n, D_MODEL), BF16),
                   jax.ShapeDtypeStruct((n, LANES), F32)],
        compiler_params=_cp(("parallel",), 48),
        name="nsa_merge_router",
    )(oc, os_, ow, y, w_o, gain.reshape(1, D_MODEL), router_pad)


def _moe_kernel(h_ref, y_ref, mg_ref, win_ref, wout_ref, gain_ref, o_ref, acc_ref):
    e = pl.program_id(1)

    @pl.when(e == 0)
    def _():
        acc_ref[...] = jnp.zeros_like(acc_ref)

    mg = mg_ref[...]
    lane = lax.broadcasted_iota(jnp.int32, mg.shape, 1)
    gcol = jnp.sum(jnp.where(lane == e, mg, 0.0), axis=-1, keepdims=True)
    acc_ref[...] += gcol * _swiglu_acc(h_ref[...], win_ref, wout_ref, EXPERT_FF, lead=(0,))

    @pl.when(e == N_EXPERTS - 1)
    def _():
        o_ref[...] = _rms(y_ref[...] + acc_ref[...], gain_ref[...])


def _moe(h, y, mg, w_in, w_out, gain, tm):
    n = h.shape[0]
    rows = lambda w: pl.BlockSpec((tm, w), lambda i, e: (i, 0))
    return pl.pallas_call(
        _moe_kernel,
        grid=(n // tm, N_EXPERTS),
        in_specs=[rows(D_MODEL), rows(D_MODEL), rows(LANES),
                  pl.BlockSpec((1, D_MODEL, 2 * EXPERT_FF), lambda i, e: (e, 0, 0)),
                  pl.BlockSpec((1, EXPERT_FF, D_MODEL), lambda i, e: (e, 0, 0)),
                  pl.BlockSpec((1, D_MODEL), lambda i, e: (0, 0))],
        out_specs=rows(D_MODEL),
        out_shape=jax.ShapeDtypeStruct((n, D_MODEL), F32),
        scratch_shapes=[pltpu.VMEM((tm, D_MODEL), F32)],
        compiler_params=_cp(("parallel", "arbitrary"), 56),
        name="moe_final_norm",
    )(h, y, mg, w_in, w_out, gain.reshape(1, D_MODEL))


def _t5_bucket_np(dist):
    n = np.maximum(np.asarray(dist, np.int64), 0)
    exact = N_BUCKETS // 2
    logpart = exact + np.floor(np.log(np.maximum(n, 1) / exact) / math.log(MAX_DISTANCE / exact)
                               * (N_BUCKETS - exact)).astype(np.int64)
    return np.where(n < exact, n, np.minimum(logpart, N_BUCKETS - 1)).astype(np.int32)


def _bias_kernel(rb_ref, idx_ref, o_ref):
    h = pl.program_id(0)
    idx = idx_ref[...]
    far = rb_ref[N_BUCKETS - 1, h]
    acc = jnp.zeros(idx.shape, F32)
    for b in range(N_BUCKETS - 1):
        acc = jnp.where(idx == b, rb_ref[b, h] - far, acc)
    o_ref[0] = acc


def _bias_tables(rel_bias, idx):
    m = idx.shape[0]
    return pl.pallas_call(
        _bias_kernel,
        grid=(N_HEADS,),
        in_specs=[pl.BlockSpec(memory_space=pltpu.SMEM), pl.BlockSpec((m, LANES), lambda h: (0, 0))],
        out_specs=pl.BlockSpec((1, m, LANES), lambda h: (h, 0, 0)),
        out_shape=jax.ShapeDtypeStruct((N_HEADS, m, LANES), F32),
        compiler_params=_cp(("arbitrary",), 32),
        name="t5_bias_tables",
    )(rel_bias, idx)


CMP_NEAR = 32


def _prompt_bias_idx():
    a = np.arange(QT)[:, None]
    b = np.arange(QT)[None, :]
    t0 = _t5_bucket_np(a - b)
    t1 = _t5_bucket_np(QT + a - b)
    m = np.arange(LANES)[None, :]
    gc = np.where(m < CMP_NEAR, _t5_bucket_np(a + (CMP_NEAR // 2) * CMP_STRIDE - (CMP_BLOCK - 1) - CMP_STRIDE * m),
                  N_BUCKETS - 1)
    return np.concatenate([t0, t1, gc], axis=0).astype(np.int32)


def _flatten_pairs(load, nh):
    lane = lax.broadcasted_iota(jnp.int32, (nh, LANES), 1)
    ev, od = [], []
    for m in range(CMP_STRIDE // 2):
        sa = load(2 * m)
        sb = load(2 * m + 1)
        ev.append(jnp.where(lane < HEAD_DIM, sa, pltpu.roll(sb, HEAD_DIM, 1)))
        od.append(jnp.where(lane < HEAD_DIM, pltpu.roll(sa, HEAD_DIM, 1), sb))
    return jnp.concatenate(ev, axis=1), jnp.concatenate(od, axis=1)


def _compress_rows(xf, w1_ref, pe_ref, w2_ref, slot):
    rows = xf.shape[0]
    half = CMP_STRIDE * HEAD_DIM
    first = _mm((xf + pe_ref[slot, 0:1, :half]).astype(BF16), w1_ref[slot, :, :CMP_HIDDEN])
    second = _mm((xf + pe_ref[slot, 0:1, half:]).astype(BF16), w1_ref[slot, :, CMP_HIDDEN:])
    act = jax.nn.gelu(first + pltpu.roll(second, rows - 1, 0))
    return _mm(act.astype(BF16), w2_ref[slot])


def _compress_prompt_kernel(x0, x1, x2, x3, w1_ref, pe_ref, w2_ref, o_ref):
    nh = x0.shape[0] // CMP_STRIDE
    xs = (x0, x1, x2, x3)
    for slot in range(2):
        parts = []
        for c in range(2):
            x_ref = xs[2 * slot + c]
            ev, od = _flatten_pairs(lambda s: x_ref[pl.ds(s, nh, stride=CMP_STRIDE), :], nh)
            parts += [ev, od]
        xf = jnp.concatenate(parts, axis=0)
        o_ref[0, slot] = _compress_rows(xf, w1_ref, pe_ref, w2_ref, slot).astype(o_ref.dtype)


def _compress_weights(phi_pe, phi_w1, phi_w2):
    half = CMP_BLOCK // 2
    w1 = jnp.concatenate([phi_w1[:, :half].reshape(2, half * HEAD_DIM, CMP_HIDDEN),
                          phi_w1[:, half:].reshape(2, half * HEAD_DIM, CMP_HIDDEN)], axis=2)
    pe = jnp.broadcast_to(phi_pe.reshape(2, 1, CMP_BLOCK * HEAD_DIM), (2, 8, CMP_BLOCK * HEAD_DIM))
    return w1.astype(BF16), pe, phi_w2.astype(BF16)


def _compress_prompt(kvf, cw, batch, length):
    w1b, pe, w2b = cw
    nh = length // CMP_STRIDE
    xspec = lambda c: pl.BlockSpec((length, LANES), lambda b: (b, c))
    return pl.pallas_call(
        _compress_prompt_kernel,
        grid=(batch,),
        in_specs=[xspec(0), xspec(1), xspec(2), xspec(3), _resident(w1b.shape), _resident(pe.shape),
                  _resident(w2b.shape)],
        out_specs=pl.BlockSpec((1, 2, N_KV * nh, HEAD_DIM), lambda b: (b, 0, 0, 0)),
        out_shape=jax.ShapeDtypeStruct((batch, 2, N_KV * nh, HEAD_DIM), BF16),
        compiler_params=_cp(("parallel",), 48),
        name="nsa_compress_prompt",
    )(kvf, kvf, kvf, kvf, w1b, pe, w2b)


_NT = (((1,), (1,)), ((), ()))


def _cover_t_np(nc_pad, nb_pad):
    ratio = SEL_BLOCK // CMP_STRIDE
    span = CMP_BLOCK // CMP_STRIDE
    off = (np.arange(ratio)[:, None] - np.arange(span)[None, :]).reshape(-1)
    target = np.arange(nb_pad)[:, None] * ratio + off[None, :]
    cov = np.sum(np.arange(nc_pad)[:, None, None] == target[None], axis=-1)
    return np.ascontiguousarray(cov.T).astype(np.float32)


def _rank_select(score, jidx, nsel):
    nb = score.shape[0]
    cnt = jnp.zeros(score.shape, F32)
    for i in range(nb):
        ri = score[i:i + 1, :]
        cnt = cnt + jnp.where((ri > score) | ((ri == score) & (jidx > i)), 1.0, 0.0)
    return cnt < nsel


def _cmp_attn_kernel(q_ref, kcv_ref, gate_ref, ghi_ref, glo_ref, covt_ref, o_ref, sel_ref, bias_ref, selt_ref):
    i = pl.program_id(1)
    q0 = i * QT
    nc = kcv_ref.shape[2] // N_KV
    nb = covt_ref.shape[0]
    base = q0 // CMP_STRIDE - CMP_NEAR // 2
    rr = lax.broadcasted_iota(jnp.int32, (LANES, nc), 0)
    cc = lax.broadcasted_iota(jnp.int32, (LANES, nc), 1)
    place = ((cc == base + rr) & (rr < CMP_NEAR)).astype(BF16)
    bias_ref[...] = (jnp.dot(ghi_ref[...], place, preferred_element_type=F32)
                     + jnp.dot(glo_ref[...], place, preferred_element_type=F32))
    t = q0 + lax.broadcasted_iota(jnp.int32, (QT, nc), 0)
    cend = lax.broadcasted_iota(jnp.int32, (QT, nc), 1) * CMP_STRIDE + (CMP_BLOCK - 1)
    vis = t >= cend
    visf = vis.astype(F32)
    madd = jnp.where(vis, 0.0, NEG)
    gate = gate_ref[...]
    jidx = lax.broadcasted_iota(jnp.int32, (nb, QT), 0)
    tq = q0 + lax.broadcasted_iota(jnp.int32, (nb, QT), 1)
    jt = tq // SEL_BLOCK
    forced = (jidx == 0) | (jidx == jt) | (jidx == jt - 1)
    visb = jidx * SEL_BLOCK <= tq
    for g in range(N_KV):
        kc = kcv_ref[0, 0, g * nc:(g + 1) * nc, :]
        vc = kcv_ref[0, 1, g * nc:(g + 1) * nc, :]
        psum = jnp.zeros((QT, nc), F32)
        for r in range(GQ):
            h = g * GQ + r
            qh = q_ref[:, h * HEAD_DIM:(h + 1) * HEAD_DIM]
            s = lax.dot_general(qh, kc, _NT, preferred_element_type=F32)
            s = s + bias_ref[h * QT:(h + 1) * QT, :] + madd
            m = jnp.max(s, axis=-1, keepdims=True)
            e = jnp.exp(s - m)
            p = ((e * visf) * (1.0 / jnp.sum(e, axis=-1, keepdims=True))).astype(BF16)
            psum = psum + p.astype(F32)
            o = _mm(p, vc)
            o_ref[:, h * HEAD_DIM:(h + 1) * HEAD_DIM] = (o * gate[:, h:h + 1]).astype(o_ref.dtype)
        imp = lax.dot_general(covt_ref[...], psum, _NT, precision=HIGHEST, preferred_element_type=F32)
        score = jnp.where(visb, jnp.where(forced, FORCE, imp), NEG)
        sel = _rank_select(score, jidx, min(TOP_N, nb)) & visb
        selt_ref[g * nb:(g + 1) * nb, :] = sel.astype(F32)
    sel_ref[...] = selt_ref[...].T


def _cmp_attn(q, kcv, gate, ghi, glo, batch, length):
    nq = length // QT
    nc = kcv.shape[2] // N_KV
    nb = length // SEL_BLOCK
    covt = jnp.asarray(_cover_t_np(nc, nb))
    row = lambda w: pl.BlockSpec((QT, w), lambda b, i: (b * nq + i, 0))
    return pl.pallas_call(
        _cmp_attn_kernel,
        grid=(batch, nq),
        in_specs=[row(Q_COLS), pl.BlockSpec((1, 2, N_KV * nc, HEAD_DIM), lambda b, i: (b, 0, 0, 0)), row(LANES),
                  _resident(ghi.shape), _resident(glo.shape), _resident(covt.shape)],
        out_specs=[row(Q_COLS), row(N_KV * nb)],
        out_shape=[jax.ShapeDtypeStruct((batch * length, Q_COLS), BF16),
                   jax.ShapeDtypeStruct((batch * length, N_KV * nb), F32)],
        scratch_shapes=[pltpu.VMEM((N_HEADS * QT, nc), F32), pltpu.VMEM((N_KV * nb, QT), F32)],
        compiler_params=_cp(("parallel", "arbitrary"), 48),
        name="nsa_cmp_attn_select",
    )(q, kcv, gate, ghi, glo, covt)


SEL_TAIL = 3
SEL_MASKED = 3


def _selected_kernel(q_ref, kv_ref, gate_ref, stab_ref, sel_ref, ex_ref, o_ref,
                     s_ref, mx_ref, mb_ref, ls_ref, acc_ref, madd_ref):
    i = pl.program_id(1)
    length = kv_ref.shape[0]
    nb = ex_ref.shape[0]
    wide = 2 * QT
    tail = SEL_TAIL * QT
    nbulk = jnp.maximum(i - 1, 0) // 2
    tstart = jnp.minimum(2 * nbulk, length // QT - SEL_TAIL)
    trows = pl.ds(pl.multiple_of(tstart * QT, QT), tail)
    rep = lambda a: jnp.concatenate([a] * GQ, axis=0)
    gate = gate_ref[...]
    kinds = []
    for jj in range(SEL_TAIL):
        tile = tstart + jj
        kinds.append(jnp.where((tile < 2 * nbulk) | (tile > i), SEL_MASKED, i - tile))

    for pair in range(N_KV // 2):
        groups = (2 * pair, 2 * pair + 1)
        kcols = [slice(g * HEAD_DIM, (g + 1) * HEAD_DIM) for g in groups]
        vcols = [slice((N_KV + g) * HEAD_DIM, (N_KV + g + 1) * HEAD_DIM) for g in groups]
        q4 = [jnp.concatenate([q_ref[:, (g * GQ + r) * HEAD_DIM:(g * GQ + r + 1) * HEAD_DIM] for r in range(GQ)],
                              axis=0) for g in groups]
        for u, g in enumerate(groups):
            sel = sel_ref[:, g * nb:(g + 1) * nb].astype(BF16)
            madd_ref[u] = (jnp.dot(sel, ex_ref[...], preferred_element_type=F32) - 1.0) * (-NEG)
        mx_ref[...] = jnp.full(mx_ref.shape, -jnp.inf, F32)
        ls_ref[...] = jnp.zeros(ls_ref.shape, F32)
        acc_ref[...] = jnp.zeros(acc_ref.shape, F32)

        def bulk_logits(c, carry):
            cols = pl.ds(pl.multiple_of(c * wide, wide), wide)
            for u in range(2):
                s = (lax.dot_general(q4[u], kv_ref[cols, kcols[u]], _NT, preferred_element_type=F32)
                     + rep(madd_ref[u, :, cols]))
                s_ref[u, :, cols] = s
                mx_ref[u] = jnp.maximum(mx_ref[u], jnp.maximum(s[:, :QT], s[:, QT:]))
            return carry

        lax.fori_loop(0, nbulk, bulk_logits, 0)

        for u, g in enumerate(groups):
            s = lax.dot_general(q4[u], kv_ref[trows, kcols[u]], _NT, preferred_element_type=F32)
            adds = [jnp.concatenate([stab_ref[g * GQ + r, :, pl.ds(pl.multiple_of(kinds[jj] * QT, QT), QT)]
                                     for r in range(GQ)], axis=0) for jj in range(SEL_TAIL)]
            s = s + jnp.concatenate(adds, axis=1) + rep(madd_ref[u, :, trows])
            s_ref[u, :, length:] = s
            mx = mx_ref[u]
            for jj in range(SEL_TAIL):
                mx = jnp.maximum(mx, s[:, jj * QT:(jj + 1) * QT])
            mb_ref[u] = jnp.broadcast_to(jnp.max(mx, axis=-1, keepdims=True), mx.shape)

        def bulk_pv(c, carry):
            cols = pl.ds(pl.multiple_of(c * wide, wide), wide)
            for u in range(2):
                m = mb_ref[u]
                p0 = jnp.exp(s_ref[u, :, pl.ds(pl.multiple_of(c * wide, wide), QT)] - m)
                p1 = jnp.exp(s_ref[u, :, pl.ds(pl.multiple_of(c * wide + QT, QT), QT)] - m)
                ls_ref[u] += p0 + p1
                p = jnp.concatenate([p0, p1], axis=1).astype(BF16)
                acc_ref[u] += jnp.dot(p, kv_ref[cols, vcols[u]], preferred_element_type=F32)
            return carry

        lax.fori_loop(0, nbulk, bulk_pv, 0)

        for u, g in enumerate(groups):
            m = mb_ref[u]
            ps = [jnp.exp(s_ref[u, :, length + jj * QT:length + (jj + 1) * QT] - m) for jj in range(SEL_TAIL)]
            lsum = ls_ref[u]
            for p in ps:
                lsum = lsum + p
            acc = acc_ref[u] + jnp.dot(jnp.concatenate(ps, axis=1).astype(BF16), kv_ref[trows, vcols[u]],
                                       preferred_element_type=F32)
            l = jnp.sum(lsum, axis=-1, keepdims=True)
            for r in range(GQ):
                h = g * GQ + r
                scale = gate[:, N_HEADS + h:N_HEADS + h + 1] / l[r * QT:(r + 1) * QT]
                o_ref[:, h * HEAD_DIM:(h + 1) * HEAD_DIM] = (acc[r * QT:(r + 1) * QT, :] * scale).astype(o_ref.dtype)


WIN_TILES = WINDOW // QT + 1


def _window_kernel(q_ref, kv_ref, gate_ref, wtab_ref, o_ref):
    i = pl.program_id(1)
    first = jnp.maximum(i - (WIN_TILES - 1), 0)
    rows = pl.ds(pl.multiple_of(first * QT, QT), WIN_TILES * QT)
    gate = gate_ref[...]
    for g in range(N_KV):
        kcol = slice(g * HEAD_DIM, (g + 1) * HEAD_DIM)
        vcol = slice(N_KV * HEAD_DIM + g * HEAD_DIM, N_KV * HEAD_DIM + (g + 1) * HEAD_DIM)
        q4 = jnp.concatenate([q_ref[:, (g * GQ + r) * HEAD_DIM:(g * GQ + r + 1) * HEAD_DIM] for r in range(GQ)],
                             axis=0)
        s = lax.dot_general(q4, kv_ref[rows, kcol], _NT, preferred_element_type=F32)
        adds = []
        for jj in range(WIN_TILES):
            d = i - first - jj
            kind = jnp.where(d < 0, WIN_TILES, d)
            col = pl.ds(pl.multiple_of(kind * QT, QT), QT)
            adds.append(jnp.concatenate([wtab_ref[g * GQ + r, :, col] for r in range(GQ)], axis=0))
        s = s + jnp.concatenate(adds, axis=1)
        p = jnp.exp(s - jnp.max(s, axis=-1, keepdims=True))
        l = jnp.sum(p, axis=-1, keepdims=True)
        acc = jnp.dot(p.astype(BF16), kv_ref[rows, vcol], preferred_element_type=F32)
        for r in range(GQ):
            h = g * GQ + r
            scale = gate[:, 2 * N_HEADS + h:2 * N_HEADS + h + 1] / l[r * QT:(r + 1) * QT]
            o_ref[:, h * HEAD_DIM:(h + 1) * HEAD_DIM] = (acc[r * QT:(r + 1) * QT, :] * scale).astype(o_ref.dtype)


def _window_tables(tab):
    a = np.arange(QT)[:, None]
    b = np.arange(QT)[None, :]
    causal = np.where(a >= b, 0.0, NEG).astype(np.float32)
    edge = np.where(a <= b, 0.0, NEG).astype(np.float32)
    const = lambda m: jnp.broadcast_to(jnp.asarray(m, F32), (N_HEADS, QT, QT))
    zero = np.zeros((QT, QT), np.float32)
    return jnp.concatenate([tab[:, :QT] + causal, tab[:, QT:], const(zero), const(zero), const(edge),
                            const(np.full((QT, QT), NEG, np.float32))], axis=2)


def _window_attn(q, kvb, gate, wtab, batch, length):
    nq = length // QT
    row = lambda w: pl.BlockSpec((QT, w), lambda b, i: (b * nq + i, 0))
    return pl.pallas_call(
        _window_kernel,
        grid=(batch, nq),
        in_specs=[row(Q_COLS), pl.BlockSpec((length, 2 * N_KV * HEAD_DIM), lambda b, i: (b, 1)), row(LANES),
                  _resident(wtab.shape)],
        out_specs=row(Q_COLS),
        out_shape=jax.ShapeDtypeStruct((batch * length, Q_COLS), BF16),
        compiler_params=_cp(("parallel", "arbitrary"), 48),
        name="nsa_window_attn",
    )(q, kvb, gate, wtab)


def _ex_np(nb, length):
    return (np.arange(nb)[:, None] == (np.arange(length)[None, :] // SEL_BLOCK)).astype(np.float32)


def _selected_tables(tab):
    a = np.arange(QT)[:, None]
    b = np.arange(QT)[None, :]
    causal = np.where(a >= b, 0.0, NEG).astype(np.float32)
    const = lambda v: jnp.full((N_HEADS, QT, QT), v, F32)
    return jnp.concatenate([tab[:, :QT] + causal, tab[:, QT:], const(0.0), const(NEG)], axis=2)


def _selected_attn(q, kvb, gate, stab, selmask, batch, length):
    nq = length // QT
    nb = length // SEL_BLOCK
    assert nq >= SEL_TAIL
    row = lambda w: pl.BlockSpec((QT, w), lambda b, i: (b * nq + i, 0))
    ex = jnp.asarray(_ex_np(nb, length), BF16)
    rows4 = GQ * QT
    return pl.pallas_call(
        _selected_kernel,
        grid=(batch, nq),
        in_specs=[row(Q_COLS), pl.BlockSpec((length, 2 * N_KV * HEAD_DIM), lambda b, i: (b, 0)), row(LANES),
                  _resident(stab.shape), row(N_KV * nb), _resident(ex.shape)],
        out_specs=row(Q_COLS),
        out_shape=jax.ShapeDtypeStruct((batch * length, Q_COLS), BF16),
        scratch_shapes=[pltpu.VMEM((2, rows4, length + SEL_TAIL * QT), F32),
                        pltpu.VMEM((2, rows4, QT), F32),
                        pltpu.VMEM((2, rows4, QT), F32),
                        pltpu.VMEM((2, rows4, QT), F32),
                        pltpu.VMEM((2, rows4, HEAD_DIM), F32),
                        pltpu.VMEM((2, QT, length), F32)],
        compiler_params=_cp(("parallel", "arbitrary"), 56),
        name="nsa_selected_attn",
    )(q, kvb, gate, stab, selmask, ex)


def _s5_sample_kernel(u_ref, hr_ref, hi_ref, ws_ref, wv_ref, a1_ref, d_ref, g_ref, sre_ref, sim_ref):
    u = u_ref[...]
    x = _mm(u.astype(BF16), ws_ref[0, 0])
    ar, ai = a1_ref[0, :, :S5_SL], a1_ref[0, :, S5_SL:]
    h0r, h0i = hr_ref[...], hi_ref[...]
    hr = x[:, :S5_SL] + (ar * h0r - ai * h0i)
    hi = x[:, S5_SL:] + (ar * h0i + ai * h0r)
    sre_ref[...] = hr
    sim_ref[...] = hi
    y = _mm(jnp.concatenate([hr, hi], axis=1).astype(BF16), wv_ref[0, 0])
    g_ref[...] = jax.nn.gelu(y + d_ref[...] * u).astype(g_ref.dtype)


def _s5_sample(u, h0_re, h0_im, ws, wv, a1, d_skip):
    n = u.shape[0]
    st = pl.BlockSpec((n, S5_SL), lambda s: (0, s))
    return pl.pallas_call(
        _s5_sample_kernel,
        grid=(S5_NSG,),
        in_specs=[pl.BlockSpec((n, LANES), lambda s: (0, s)), st, st,
                  pl.BlockSpec((1, 1, LANES, 2 * S5_SL), lambda s: (S5_T - 1, s, 0, 0)),
                  pl.BlockSpec((1, 1, 2 * S5_SL, LANES), lambda s: (0, s, 0, 0)),
                  pl.BlockSpec((1, 1, 2 * S5_SL), lambda s: (s, 0, 0)),
                  pl.BlockSpec((1, LANES), lambda s: (0, s))],
        out_specs=[pl.BlockSpec((n, LANES), lambda s: (0, s)), st, st],
        out_shape=[jax.ShapeDtypeStruct((n, D_MODEL), BF16),
                   jax.ShapeDtypeStruct((n, N_GROUPS * STATE), F32),
                   jax.ShapeDtypeStruct((n, N_GROUPS * STATE), F32)],
        compiler_params=_cp(("parallel",), 32),
        name="s5_sample",
    )(u, h0_re, h0_im, ws, wv, a1, d_skip.reshape(1, D_MODEL))


def _softmax_lanes(s, valid):
    m = jnp.max(s, axis=-1, keepdims=True)
    e = jnp.exp(s - m)
    return (e * valid) * (1.0 / jnp.sum(e, axis=-1, keepdims=True))


def _own_group(o):
    rowg = lax.broadcasted_iota(jnp.int32, (N_HEADS, HEAD_DIM), 0) // GQ
    out = jnp.zeros((N_HEADS, HEAD_DIM), F32)
    for g in range(N_KV):
        out = out + jnp.where(rowg == g, o[:, g * HEAD_DIM:(g + 1) * HEAD_DIM], 0.0)
    return out


DECODE_PAGES_PER_STEP = 4


def _nsa_decode_kernel(pt_ref, *refs, past_len, n_cmp):
    x_refs = refs[:DECODE_PAGES_PER_STEP]
    (win_ref, qbd_ref, new_ref, gate_ref, w1_ref, pe_ref, w2_ref, cov_ref, bias_ref, ex_ref, o_ref,
     xt_ref, xf_ref, ks_ref, vs_ref, kw_ref, vw_ref) = refs[DECODE_PAGES_PER_STEP:]
    step = pl.program_id(1)
    nsteps = pl.num_programs(1)
    page = x_refs[0].shape[2]
    hp = page // CMP_STRIDE
    nh = past_len // CMP_STRIDE
    nkv = N_KV * HEAD_DIM
    sel_cols = ks_ref.shape[1]
    win_cols = kw_ref.shape[1]
    wb = win_ref.shape[2]

    def stash(ref, cols, x):
        ref[:, cols] = x.astype(BF16)

    for k, x_ref in enumerate(x_refs):
        p = step * len(x_refs) + k
        xt = x_ref[0, :2 * nkv, :].T
        for c in range(2 * nkv // LANES):
            xt_ref[k, c] = xt[:, c * LANES:(c + 1) * LANES]
        for slot in range(2):
            for c in range(2):
                blk = 2 * slot + c
                ev, od = _flatten_pairs(lambda s: xt_ref[k, blk, pl.ds(s, hp, stride=CMP_STRIDE), :], hp)
                for head, val in ((2 * c, ev), (2 * c + 1, od)):
                    xf_ref[slot, pl.ds(pl.multiple_of(head * nh + p * hp, hp), hp), :] = val
        cols = pl.ds(pl.multiple_of(p * page, page), page)
        stash(ks_ref, cols, x_ref[0, 2 * nkv:3 * nkv, :])
        stash(vs_ref, cols, x_ref[0, 3 * nkv:, :])

    @pl.when(step == nsteps - 1)
    def _():
        qbd = qbd_ref[0]
        new = new_ref[0]
        lane0 = lax.broadcasted_iota(jnp.int32, (nkv, LANES), 1) == 0
        newcol = lambda k: jnp.where(lane0, jnp.broadcast_to(new[:, k:k + 1], (nkv, LANES)), 0.0)
        stash(ks_ref, slice(past_len, sel_cols), newcol(0))
        stash(vs_ref, slice(past_len, sel_cols), newcol(1))
        stash(kw_ref, slice(0, wb), win_ref[0, :nkv, :])
        stash(vw_ref, slice(0, wb), win_ref[0, nkv:, :])
        stash(kw_ref, slice(wb, win_cols), newcol(2))
        stash(vw_ref, slice(wb, win_cols), newcol(3))
        rowh = lax.broadcasted_iota(jnp.int32, (N_HEADS, LANES), 0)

        kc = _compress_rows(xf_ref[0], w1_ref, pe_ref, w2_ref, 0).astype(BF16)
        vc = _compress_rows(xf_ref[1], w1_ref, pe_ref, w2_ref, 1).astype(BF16)
        s = jnp.zeros((N_HEADS, nh), F32)
        for g in range(N_KV):
            s = s + lax.dot_general(qbd[:, g * HEAD_DIM:(g + 1) * HEAD_DIM], kc[g * nh:(g + 1) * nh], _NT,
                                    preferred_element_type=F32)
        nidx = lax.broadcasted_iota(jnp.int32, (N_HEADS, nh), 1)
        s = s + bias_ref[:, :nh] + jnp.where(nidx < n_cmp, 0.0, NEG)
        pc = _softmax_lanes(s, (nidx < n_cmp).astype(F32)).astype(BF16)
        o_c = _own_group(jnp.concatenate([_mm(pc, vc[g * nh:(g + 1) * nh]) for g in range(N_KV)], axis=1))

        pcf = pc.astype(F32)
        psum = jnp.concatenate([jnp.sum(pcf[g * GQ:(g + 1) * GQ], axis=0, keepdims=True) for g in range(N_KV)]
                               + [jnp.zeros((8 - N_KV, nh), F32)], axis=0)
        imp = jnp.dot(psum, cov_ref[...], precision=HIGHEST, preferred_element_type=F32)
        jt = past_len // SEL_BLOCK
        nb = jt + 1
        lane = lax.broadcasted_iota(jnp.int32, (LANES, LANES), 1)
        subl = lax.broadcasted_iota(jnp.int32, (LANES, LANES), 0)

        def scored(x, j):
            forced = (j == 0) | (j == jt) | (j == jt - 1)
            return jnp.where(j < nb, jnp.where(forced, FORCE, x), NEG)

        imp_sq = jnp.concatenate([imp, jnp.zeros((LANES - 8, LANES), F32)], axis=0)
        sc_row = scored(imp_sq, lane)
        sc_col = scored(imp_sq.T, subl)
        selh = jnp.zeros((N_HEADS, LANES), F32)
        for g in range(N_KV):
            col = sc_col[:, g:g + 1]
            row = sc_row[g:g + 1, :]
            ahead = (col > row) | ((col == row) & (subl < lane))
            rank = jnp.sum(jnp.where(ahead, 1.0, 0.0), axis=0, keepdims=True)
            chosen = jnp.where((rank < min(TOP_N, nb)) & (lane[:1] < nb), 1.0, 0.0)
            selh = selh + jnp.where(rowh // GQ == g, chosen, 0.0)

        pos = lax.broadcasted_iota(jnp.int32, (N_HEADS, sel_cols), 1)
        inblk = jnp.dot(selh.astype(BF16), ex_ref[...], preferred_element_type=F32)
        ok_s = (inblk > 0.5) & (pos <= past_len)
        s = _mm(qbd, ks_ref[...]) + bias_ref[:, nh:nh + sel_cols]
        ps = _softmax_lanes(jnp.where(ok_s, s, NEG), ok_s.astype(F32)).astype(BF16)
        o_s = _own_group(lax.dot_general(ps, vs_ref[...], _NT, preferred_element_type=F32))

        posw = lax.broadcasted_iota(jnp.int32, (N_HEADS, win_cols), 1)
        ok_w = posw <= wb
        s = _mm(qbd, kw_ref[...]) + bias_ref[:, nh + sel_cols:nh + sel_cols + win_cols]
        pw = _softmax_lanes(jnp.where(ok_w, s, NEG), ok_w.astype(F32)).astype(BF16)
        o_w = _own_group(lax.dot_general(pw, vw_ref[...], _NT, preferred_element_type=F32))

        gate = gate_ref[0]
        o_ref[0] = gate[:, 0:1] * o_c + gate[:, 1:2] * o_s + gate[:, 2:3] * o_w


def _decode_bias_idx(past_len, wb, ncp, sel_cols, win_cols):
    n = np.arange(ncp)
    cmp_d = past_len - (n * CMP_STRIDE + CMP_BLOCK - 1)
    sel_d = past_len - np.arange(sel_cols)
    win_d = np.concatenate([wb - np.arange(wb), np.zeros(win_cols - wb, np.int64)])
    idx = _t5_bucket_np(np.concatenate([cmp_d, sel_d, win_d]))
    pad = -len(idx) % (8 * LANES)
    return np.concatenate([idx, np.full(pad, N_BUCKETS - 1)]).reshape(-1, LANES).astype(np.int32)


def _nsa_decode(cache_kv, cache_win, page_table, qbd, new_cols, gate3, cw, rel_bias):
    n_pool, page = cache_kv.shape[:2]
    nseq, npages = page_table.shape
    past_len = npages * page
    wb = cache_win.shape[1]
    w1b, pe, w2b = cw
    nh = past_len // CMP_STRIDE
    n_cmp = (past_len + 1 - CMP_BLOCK) // CMP_STRIDE + 1
    sel_cols = past_len + LANES
    win_cols = wb + LANES
    nkv = N_KV * HEAD_DIM
    kvt = cache_kv.transpose(0, 2, 3, 4, 1).reshape(n_pool, 4 * nkv, page)
    wint = cache_win.transpose(0, 2, 3, 4, 1).reshape(nseq, 2 * nkv, wb)
    cov = jnp.asarray(_cover_t_np(LANES, LANES).T)
    idx = _decode_bias_idx(past_len, wb, nh, sel_cols, win_cols)
    bias = _bias_tables(rel_bias, jnp.asarray(idx)).reshape(N_HEADS, idx.size)
    ex = jnp.asarray((np.arange(LANES)[:, None] == np.arange(sel_cols)[None, :] // SEL_BLOCK).astype(np.float32), BF16)
    const = lambda a: pl.BlockSpec(a.shape, lambda b, p, pt: (0,) * a.ndim)
    pps = DECODE_PAGES_PER_STEP
    page_spec = lambda k: pl.BlockSpec((1, 4 * nkv, page), lambda b, p, pt: (pt[b, p * pps + k], 0, 0))
    grid_spec = pltpu.PrefetchScalarGridSpec(
        num_scalar_prefetch=1,
        grid=(nseq, npages // pps),
        in_specs=[page_spec(k) for k in range(pps)] + [
                  pl.BlockSpec((1, 2 * nkv, wb), lambda b, p, pt: (b, 0, 0)),
                  pl.BlockSpec((1, N_HEADS, nkv), lambda b, p, pt: (b, 0, 0)),
                  pl.BlockSpec((1, nkv, LANES), lambda b, p, pt: (b, 0, 0)),
                  pl.BlockSpec((1, N_HEADS, LANES), lambda b, p, pt: (b, 0, 0)),
                  const(w1b), const(pe), const(w2b), const(cov), const(bias), const(ex)],
        out_specs=pl.BlockSpec((1, N_HEADS, HEAD_DIM), lambda b, p, pt: (b, 0, 0)),
        scratch_shapes=[pltpu.VMEM((pps, 2 * nkv // LANES, page, LANES), F32),
                        pltpu.VMEM((2, N_KV * nh, CMP_STRIDE * HEAD_DIM), F32),
                        pltpu.VMEM((nkv, sel_cols), BF16), pltpu.VMEM((nkv, sel_cols), BF16),
                        pltpu.VMEM((nkv, win_cols), BF16), pltpu.VMEM((nkv, win_cols), BF16)])
    out = pl.pallas_call(
        functools.partial(_nsa_decode_kernel, past_len=past_len, n_cmp=n_cmp),
        grid_spec=grid_spec,
        out_shape=jax.ShapeDtypeStruct((nseq, N_HEADS, HEAD_DIM), F32),
        compiler_params=_cp(("arbitrary", "arbitrary"), 56),
        name="nsa_decode",
    )(page_table, *([kvt] * pps), wint, qbd, new_cols, gate3, w1b, pe, w2b, cov, bias, ex)
    return out.reshape(nseq, Q_COLS)


def _decode_operands(qs, kvbs, gates):
    nseq = qs.shape[0]
    nkv = N_KV * HEAD_DIM
    own = (np.arange(N_HEADS)[:, None] // GQ) == (np.arange(nkv)[None, :] // HEAD_DIM)
    qbd = jnp.where(own[None], jnp.tile(qs.reshape(nseq, N_HEADS, HEAD_DIM), (1, 1, N_KV)), 0.0)
    new_cols = jnp.pad(kvbs.astype(F32).reshape(nseq, 4, nkv).transpose(0, 2, 1), ((0, 0), (0, 0), (0, LANES - 4)))
    gate3 = jnp.pad(gates[:, :GATE_COLS].reshape(nseq, 3, N_HEADS).transpose(0, 2, 1),
                    ((0, 0), (0, 0), (0, LANES - 3)))
    return qbd, new_cols, gate3


PROMPT_TM = 512


def kernel(x_prompt, x_sample, state_s5_re, state_s5_im, cache_kv, cache_win, page_table, rel_bias, norm_mix, norm_ffn, norm_final, s5_lam_re, s5_lam_im, s5_log_dt, s5_b_re, s5_b_im, s5_c_re, s5_c_im, s5_d, s5_w_glu, ffn_w_in, ffn_w_out, nsa_w_in, nsa_phi_pe, nsa_phi_w1, nsa_phi_w2, nsa_w_o, moe_router, moe_w_in, moe_w_out):
    batch, length, _ = x_prompt.shape
    nseq = x_sample.shape[0]
    xp = x_prompt.reshape(batch * length, D_MODEL)
    xs = x_sample.reshape(nseq, D_MODEL)
    nstate = N_GROUPS * STATE

    wk, ws, wv, apow, a1 = _s5_prep(s5_lam_re, s5_lam_im, s5_log_dt, s5_b_re, s5_b_im, s5_c_re, s5_c_im)
    w_glu = s5_w_glu.astype(BF16)
    ffn_in, ffn_out = ffn_w_in.astype(BF16), ffn_w_out.astype(BF16)
    nsa_in = jnp.pad(nsa_w_in, ((0, 0), (0, NSA_PAD - nsa_w_in.shape[1]))).astype(BF16)
    w_o = nsa_w_o.astype(BF16)
    router = jnp.pad(moe_router, ((0, 0), (0, LANES - N_EXPERTS))).astype(BF16)
    moe_in, moe_out = moe_w_in.astype(BF16), moe_w_out.astype(BF16)
    cw = _compress_weights(nsa_phi_pe, nsa_phi_w1, nsa_phi_w2)
    tabs = _bias_tables(rel_bias, jnp.asarray(_prompt_bias_idx()))
    tab = tabs[:, :2 * QT]
    gflat = tabs[:, 2 * QT:].reshape(N_HEADS * QT, LANES)
    ghi = gflat.astype(BF16)
    glo = (gflat - ghi.astype(F32)).astype(BF16)

    up = _rmsnorm(xp, norm_mix[0], PROMPT_TM)
    gp, sre_p, sim_p = _s5_prompt(up, s5_d, wk, ws, wv, apow, batch, length)
    y1p, h1p = _glu(gp, xp, w_glu, norm_ffn[0], PROMPT_TM)
    y2p, h2p = _ffn(h1p, y1p, ffn_in, ffn_out, norm_mix[1], PROMPT_TM)
    q, kvf, winf, kvb, gate = _nsa_proj(h2p, nsa_in, PROMPT_TM)
    kcv = _compress_prompt(kvf, cw, batch, length)
    oc, sel = _cmp_attn(q, kcv, gate, ghi, glo, batch, length)
    os_ = _selected_attn(q, kvb, gate, _selected_tables(tab), sel, batch, length)
    ow = _window_attn(q, kvb, gate, _window_tables(tab), batch, length)
    y3p, h3p, mgp = _merge(oc, os_, ow, y2p, w_o, norm_ffn[1], router, PROMPT_TM)
    yp = _moe(h3p, y3p, mgp, moe_in, moe_out, norm_final, PROMPT_TM)

    us = _rmsnorm(xs, norm_mix[0], nseq)
    gs, sre_s, sim_s = _s5_sample(us, state_s5_re.reshape(nseq, nstate), state_s5_im.reshape(nseq, nstate),
                                  ws, wv, a1, s5_d)
    y1s, h1s = _glu(gs, xs, w_glu, norm_ffn[0], nseq)
    y2s, h2s = _ffn(h1s, y1s, ffn_in, ffn_out, norm_mix[1], nseq)
    qs, kvfs, winfs, kvbs, gates = _nsa_proj(h2s, nsa_in, nseq)
    qbd, new_cols, gate3 = _decode_operands(qs, kvbs, gates)
    od = _nsa_decode(cache_kv, cache_win, page_table, qbd, new_cols, gate3, cw, rel_bias)
    zero = jnp.zeros((nseq, D_MODEL), F32)
    y3s, h3s, mgs = _merge(od, zero, zero, y2s, w_o, norm_ffn[1], router, nseq)
    ys = _moe(h3s, y3s, mgs, moe_in, moe_out, norm_final, nseq)

    wp = min(WINDOW, length)
    return (yp.reshape(batch, length, D_MODEL),
            ys.reshape(nseq, 1, D_MODEL),
            sre_p.reshape(batch, N_GROUPS, STATE), sim_p.reshape(batch, N_GROUPS, STATE),
            kvf.reshape(batch, length, 4, N_KV, HEAD_DIM),
            winf.reshape(batch, length, 2, N_KV, HEAD_DIM)[:, length - wp:],
            sre_s.reshape(nseq, N_GROUPS, STATE), sim_s.reshape(nseq, N_GROUPS, STATE),
            kvfs.reshape(nseq, 1, 4, N_KV, HEAD_DIM),
            winfs.reshape(nseq, 1, 2, N_KV, HEAD_DIM))
```

```python
import functools
import math

import numpy as np
import jax
import jax.numpy as jnp
from jax import lax
from jax.experimental import pallas as pl
from jax.experimental.pallas import tpu as pltpu

F32 = jnp.float32
BF16 = jnp.bfloat16
HIGHEST = lax.Precision.HIGHEST

D_MODEL = 1024
GROUP_CH = 16
N_GROUPS = D_MODEL // GROUP_CH
STATE = 64
N_HEADS = 16
HEAD_DIM = 64
N_KV = 4
GQ = N_HEADS // N_KV
CMP_BLOCK = 32
CMP_STRIDE = 16
CMP_HIDDEN = 2 * HEAD_DIM
SEL_BLOCK = 64
TOP_N = 16
WINDOW = 512
Q_COLS = N_HEADS * HEAD_DIM
KV_COLS = 6 * N_KV * HEAD_DIM
GATE_COLS = 3 * N_HEADS
N_BUCKETS = 32
MAX_DISTANCE = 128
D_FF = 2816
N_EXPERTS = 8
EXPERT_FF = 1408
EPS = 1e-6
NEG = -1e30
FORCE = 1e4

LANES = 128
S5_T = 8
S5_SG = LANES // GROUP_CH
S5_NSG = N_GROUPS // S5_SG
S5_SL = S5_SG * STATE
QT = 128


def _cp(sem, vmem_mb):
    return pltpu.CompilerParams(dimension_semantics=sem, vmem_limit_bytes=vmem_mb << 20)


def _cmul(ar, ai, br, bi):
    return ar * br - ai * bi, ar * bi + ai * br


def _s5_prep_kernel(lr_r, li_r, ldt_r, br_r, bi_r, lr_c, li_c, ldt_c, cr_c, ci_c,
                    wk_ref, ws_ref, wv_ref, apow_ref, a1_ref):
    def disc(lr, li, ldt):
        dt = jnp.exp(ldt)
        mag = jnp.exp(lr * dt)
        abr = mag * jnp.cos(li * dt)
        abi = mag * jnp.sin(li * dt)
        den = lr * lr + li * li
        nr = abr - 1.0
        fr = (nr * lr + abi * li) / den
        fi = (abi * lr - nr * li) / den
        return abr, abi, fr, fi

    abr, abi, fr, fi = disc(lr_r[0], li_r[0], ldt_r[0])
    rg = lax.broadcasted_iota(jnp.int32, (LANES, S5_SL), 0) // GROUP_CH
    cg = lax.broadcasted_iota(jnp.int32, (LANES, S5_SL), 1) // STATE
    mask_r = (rg == cg).astype(F32)
    b_r, b_i = br_r[...], bi_r[...]
    bbr = (fr * b_r - fi * b_i) * mask_r
    bbi = (fr * b_i + fi * b_r) * mask_r
    abr_c, abi_c, _, _ = disc(lr_c[0], li_c[0], ldt_c[0])
    rg2 = lax.broadcasted_iota(jnp.int32, (S5_SL, LANES), 0) // STATE
    cg2 = lax.broadcasted_iota(jnp.int32, (S5_SL, LANES), 1) // GROUP_CH
    mask_c = (rg2 == cg2).astype(F32)
    c_r = cr_c[0] * mask_c
    c_i = ci_c[0] * mask_c

    a1_ref[0, :, :S5_SL] = abr
    a1_ref[0, :, S5_SL:] = abi

    pr = jnp.ones_like(abr)
    pi = jnp.zeros_like(abr)
    pr_c = jnp.ones_like(abr_c)
    pi_c = jnp.zeros_like(abr_c)
    for tau in range(S5_T + 1):
        wv_ref[tau, 0, :S5_SL, :] = (c_r * pr_c - c_i * pi_c).astype(BF16)
        wv_ref[tau, 0, S5_SL:, :] = (-(c_r * pi_c + c_i * pr_c)).astype(BF16)
        if tau < S5_T:
            er, ei = _cmul(pr, pi, bbr, bbi)
            s = S5_T - 1 - tau
            ws_ref[s, 0, :, :S5_SL] = er.astype(BF16)
            ws_ref[s, 0, :, S5_SL:] = ei.astype(BF16)
            k = (jnp.dot(er, c_r, precision=HIGHEST, preferred_element_type=F32)
                 - jnp.dot(ei, c_i, precision=HIGHEST, preferred_element_type=F32))
            wk_ref[tau, 0] = k.astype(BF16)
        pr, pi = _cmul(pr, pi, abr, abi)
        pr_c, pi_c = _cmul(pr_c, pi_c, abr_c, abi_c)
    tr, ti = jnp.ones_like(abr), jnp.zeros_like(abr)
    for _ in range(S5_T):
        tr, ti = _cmul(tr, ti, abr, abi)
    qr, qi = jnp.ones_like(abr), jnp.zeros_like(abr)
    for i in range(S5_T + 1):
        apow_ref[0, i:i + 1, :S5_SL] = qr
        apow_ref[0, i:i + 1, S5_SL:] = qi
        qr, qi = _cmul(qr, qi, tr, ti)


def _s5_prep(lam_re, lam_im, log_dt, b_re, b_im, c_re, c_im):
    n = S5_NSG
    row = lambda a: a.reshape(n, 1, S5_SL)
    col = lambda a: a.reshape(n, S5_SL, 1)
    ldt = jnp.repeat(log_dt, STATE)
    bt = lambda b: jnp.tile(b.transpose(0, 2, 1).reshape(D_MODEL, STATE), (1, S5_SG))
    ct = lambda c: jnp.tile(c.reshape(n, LANES, STATE).transpose(0, 2, 1), (1, S5_SG, 1))
    rspec = pl.BlockSpec((1, 1, S5_SL), lambda s: (s, 0, 0))
    cspec = pl.BlockSpec((1, S5_SL, 1), lambda s: (s, 0, 0))
    t1 = S5_T + 1
    return pl.pallas_call(
        _s5_prep_kernel,
        grid=(n,),
        in_specs=[rspec, rspec, rspec,
                  pl.BlockSpec((LANES, S5_SL), lambda s: (s, 0)), pl.BlockSpec((LANES, S5_SL), lambda s: (s, 0)),
                  cspec, cspec, cspec,
                  pl.BlockSpec((1, S5_SL, LANES), lambda s: (s, 0, 0)), pl.BlockSpec((1, S5_SL, LANES), lambda s: (s, 0, 0))],
        out_specs=[pl.BlockSpec((S5_T, 1, LANES, LANES), lambda s: (0, s, 0, 0)),
                   pl.BlockSpec((S5_T, 1, LANES, 2 * S5_SL), lambda s: (0, s, 0, 0)),
                   pl.BlockSpec((t1, 1, 2 * S5_SL, LANES), lambda s: (0, s, 0, 0)),
                   pl.BlockSpec((1, t1 + 7, 2 * S5_SL), lambda s: (s, 0, 0)),
                   pl.BlockSpec((1, 1, 2 * S5_SL), lambda s: (s, 0, 0))],
        out_shape=[jax.ShapeDtypeStruct((S5_T, n, LANES, LANES), BF16),
                   jax.ShapeDtypeStruct((S5_T, n, LANES, 2 * S5_SL), BF16),
                   jax.ShapeDtypeStruct((t1, n, 2 * S5_SL, LANES), BF16),
                   jax.ShapeDtypeStruct((n, t1 + 7, 2 * S5_SL), F32),
                   jax.ShapeDtypeStruct((n, 1, 2 * S5_SL), F32)],
        compiler_params=_cp(("arbitrary",), 48),
        name="s5_prep",
    )(row(lam_re), row(lam_im), row(ldt), bt(b_re), bt(b_im),
      col(lam_re), col(lam_im), col(ldt), ct(c_re), ct(c_im))


def _rms(x, g):
    return x * lax.rsqrt(jnp.mean(x * x, axis=-1, keepdims=True) + EPS) * g


def _rmsnorm_kernel(x_ref, g_ref, o_ref):
    o_ref[...] = _rms(x_ref[...], g_ref[...]).astype(o_ref.dtype)


def _rmsnorm(x, g, tm, dtype=F32):
    n = x.shape[0]
    return pl.pallas_call(
        _rmsnorm_kernel,
        grid=(n // tm,),
        in_specs=[pl.BlockSpec((tm, D_MODEL), lambda i: (i, 0)), pl.BlockSpec((1, D_MODEL), lambda i: (0, 0))],
        out_specs=pl.BlockSpec((tm, D_MODEL), lambda i: (i, 0)),
        out_shape=jax.ShapeDtypeStruct((n, D_MODEL), dtype),
        compiler_params=_cp(("parallel",), 32),
        name="rmsnorm",
    )(x, g.reshape(1, D_MODEL))


def _s5_prompt_kernel(u_ref, d_ref, wk_ref, ws_ref, wv_ref, apow_ref, g_ref, sre_ref, sim_ref,
                      ub_ref, s_ref, p_ref, gc_ref, hs_ref, y_ref):
    length = u_ref.shape[0]
    r = length // S5_T
    ng = r // 8

    acc = jnp.zeros((r, 2 * S5_SL), F32)
    for s in range(S5_T):
        us = u_ref[pl.ds(s, r, stride=S5_T), :].astype(BF16)
        ub_ref[s] = us
        acc = acc + jnp.dot(us, ws_ref[s, 0], preferred_element_type=F32)
    for c in range(2 * S5_SL // LANES):
        s_ref[c] = acc[:, c * LANES:(c + 1) * LANES]

    nsl = S5_SL // LANES
    for c in range(nsl):
        lr = slice(c * LANES, (c + 1) * LANES)
        li = slice(S5_SL + c * LANES, S5_SL + (c + 1) * LANES)

        def apw(i):
            return apow_ref[0, i:i + 1, lr], apow_ref[0, i:i + 1, li]

        a1r, a1i = apw(1)
        pr = s_ref[c, pl.ds(0, ng, stride=8), :]
        pi = s_ref[nsl + c, pl.ds(0, ng, stride=8), :]
        p_ref[0, c] = pr
        p_ref[0, nsl + c] = pi
        for i in range(1, 8):
            qr, qi = _cmul(a1r, a1i, pr, pi)
            pr = qr + s_ref[c, pl.ds(i, ng, stride=8), :]
            pi = qi + s_ref[nsl + c, pl.ds(i, ng, stride=8), :]
            p_ref[i, c] = pr
            p_ref[i, nsl + c] = pi

        a8r, a8i = apw(8)

        def carry(m, g):
            gr, gi = g
            gc_ref[c, pl.ds(m, 1), :] = gr
            gc_ref[nsl + c, pl.ds(m, 1), :] = gi
            nr, ni = _cmul(a8r, a8i, gr, gi)
            return (nr + p_ref[7, c, pl.ds(m, 1), :], ni + p_ref[7, nsl + c, pl.ds(m, 1), :])

        z = jnp.zeros((1, LANES), F32)
        er, ei = lax.fori_loop(0, ng, carry, (z, z))
        sre_ref[0, :, lr] = er
        sim_ref[0, :, lr] = ei

        gr = gc_ref[c]
        gi = gc_ref[nsl + c]
        hs_ref[c, pl.ds(0, ng, stride=8), :] = gr
        hs_ref[nsl + c, pl.ds(0, ng, stride=8), :] = gi
        for i in range(1, 8):
            air, aii = apw(i)
            hr, hi = _cmul(air, aii, gr, gi)
            hs_ref[c, pl.ds(i, ng, stride=8), :] = hr + p_ref[i - 1, c]
            hs_ref[nsl + c, pl.ds(i, ng, stride=8), :] = hi + p_ref[i - 1, nsl + c]

    hs = jnp.concatenate([hs_ref[c] for c in range(2 * S5_SL // LANES)], axis=1).astype(BF16)
    for t in range(S5_T):
        o = jnp.dot(hs, wv_ref[t + 1, 0], preferred_element_type=F32)
        for s in range(t + 1):
            o = o + jnp.dot(ub_ref[s], wk_ref[t - s, 0], preferred_element_type=F32)
        y_ref[pl.ds(t, r, stride=S5_T), :] = o
    y = y_ref[...] + d_ref[...] * u_ref[...]
    g_ref[...] = jax.nn.gelu(y).astype(g_ref.dtype)


def _s5_prompt(u, d_skip, wk, ws, wv, apow, batch, length):
    r = length // S5_T
    t1 = S5_T + 1
    return pl.pallas_call(
        _s5_prompt_kernel,
        grid=(S5_NSG, batch),
        in_specs=[pl.BlockSpec((length, LANES), lambda s, b: (b, s)),
                  pl.BlockSpec((1, LANES), lambda s, b: (0, s)),
                  pl.BlockSpec((S5_T, 1, LANES, LANES), lambda s, b: (0, s, 0, 0)),
                  pl.BlockSpec((S5_T, 1, LANES, 2 * S5_SL), lambda s, b: (0, s, 0, 0)),
                  pl.BlockSpec((t1, 1, 2 * S5_SL, LANES), lambda s, b: (0, s, 0, 0)),
                  pl.BlockSpec((1, t1 + 7, 2 * S5_SL), lambda s, b: (s, 0, 0))],
        out_specs=[pl.BlockSpec((length, LANES), lambda s, b: (b, s)),
                   pl.BlockSpec((1, 1, S5_SL), lambda s, b: (b, 0, s)),
                   pl.BlockSpec((1, 1, S5_SL), lambda s, b: (b, 0, s))],
        out_shape=[jax.ShapeDtypeStruct((batch * length, D_MODEL), BF16),
                   jax.ShapeDtypeStruct((batch, 1, N_GROUPS * STATE), F32),
                   jax.ShapeDtypeStruct((batch, 1, N_GROUPS * STATE), F32)],
        scratch_shapes=[pltpu.VMEM((S5_T, r, LANES), BF16),
                        pltpu.VMEM((2 * S5_SL // LANES, r, LANES), F32),
                        pltpu.VMEM((8, 2 * S5_SL // LANES, r // 8, LANES), F32),
                        pltpu.VMEM((2 * S5_SL // LANES, r // 8, LANES), F32),
                        pltpu.VMEM((2 * S5_SL // LANES, r, LANES), F32),
                        pltpu.VMEM((length, LANES), F32)],
        compiler_params=_cp(("arbitrary", "arbitrary"), 56),
        name="s5_prompt",
    )(u, d_skip.reshape(1, D_MODEL), wk, ws, wv, apow)


def _resident(shape):
    nd = len(shape)
    return pl.BlockSpec(shape, lambda *_: (0,) * nd, pipeline_mode=pl.Buffered(1))


def _rows(tm, width):
    return pl.BlockSpec((tm, width), lambda i: (i, 0))


def _mm(a, b):
    return jnp.dot(a, b, preferred_element_type=F32)


def _glu_kernel(g_ref, x_ref, w_ref, gain_ref, y_ref, h_ref):
    p = _mm(g_ref[...], w_ref[...])
    y = x_ref[...] + p[:, :D_MODEL] * jax.nn.sigmoid(p[:, D_MODEL:])
    y_ref[...] = y
    h_ref[...] = _rms(y, gain_ref[...]).astype(h_ref.dtype)


def _glu(g, x, w_glu, gain, tm):
    n = g.shape[0]
    return pl.pallas_call(
        _glu_kernel,
        grid=(n // tm,),
        in_specs=[_rows(tm, D_MODEL), _rows(tm, D_MODEL), _resident((D_MODEL, 2 * D_MODEL)), _resident((1, D_MODEL))],
        out_specs=[_rows(tm, D_MODEL), _rows(tm, D_MODEL)],
        out_shape=[jax.ShapeDtypeStruct((n, D_MODEL), F32), jax.ShapeDtypeStruct((n, D_MODEL), g.dtype)],
        compiler_params=_cp(("parallel",), 48),
        name="s5_glu",
    )(g, x, w_glu, gain.reshape(1, D_MODEL))


def _chunks(total, size):
    return [(s, min(size, total - s)) for s in range(0, total, size)]


def _swiglu_acc(h, win_ref, wout_ref, ff, lead=()):
    acc = None
    for s, w in _chunks(ff, 256):
        a = jnp.dot(h, win_ref[lead + (slice(None), slice(s, s + w))], preferred_element_type=F32)
        b = jnp.dot(h, win_ref[lead + (slice(None), slice(ff + s, ff + s + w))], preferred_element_type=F32)
        act = (jax.nn.silu(a) * b).astype(BF16)
        z = jnp.dot(act, wout_ref[lead + (slice(s, s + w), slice(None))], preferred_element_type=F32)
        acc = z if acc is None else acc + z
    return acc


def _ffn_kernel(h_ref, y_ref, win_ref, wout_ref, gain_ref, yo_ref, ho_ref):
    y = y_ref[...] + _swiglu_acc(h_ref[...], win_ref, wout_ref, D_FF)
    yo_ref[...] = y
    ho_ref[...] = _rms(y, gain_ref[...]).astype(ho_ref.dtype)


def _ffn(h, y, w_in, w_out, gain, tm):
    n = h.shape[0]
    return pl.pallas_call(
        _ffn_kernel,
        grid=(n // tm,),
        in_specs=[_rows(tm, D_MODEL), _rows(tm, D_MODEL), _resident((D_MODEL, 2 * D_FF)), _resident((D_FF, D_MODEL)),
                  _resident((1, D_MODEL))],
        out_specs=[_rows(tm, D_MODEL), _rows(tm, D_MODEL)],
        out_shape=[jax.ShapeDtypeStruct((n, D_MODEL), F32), jax.ShapeDtypeStruct((n, D_MODEL), BF16)],
        compiler_params=_cp(("parallel",), 56),
        name="ffn_swiglu",
    )(h, y, w_in, w_out, gain.reshape(1, D_MODEL))


NSA_PAD = 2688
KV_OFF = Q_COLS


def _nsa_proj_kernel(h_ref, w_ref, q_ref, kvf_ref, winf_ref, kvb_ref, gate_ref):
    p = _mm(h_ref[...], w_ref[...])
    q_ref[...] = (p[:, :Q_COLS] * (HEAD_DIM ** -0.5)).astype(q_ref.dtype)
    kvf_ref[...] = p[:, KV_OFF:KV_OFF + 1024]
    winf_ref[...] = p[:, KV_OFF + 1024:KV_OFF + 1536]
    kvb_ref[...] = p[:, KV_OFF + 512:KV_OFF + 1536].astype(kvb_ref.dtype)
    gate_ref[...] = jax.nn.sigmoid(p[:, KV_OFF + 1536:])


def _nsa_proj(h, w_in_pad, tm):
    n = h.shape[0]
    return pl.pallas_call(
        _nsa_proj_kernel,
        grid=(n // tm,),
        in_specs=[_rows(tm, D_MODEL), _resident((D_MODEL, NSA_PAD))],
        out_specs=[_rows(tm, Q_COLS), _rows(tm, 1024), _rows(tm, 512), _rows(tm, 1024), _rows(tm, LANES)],
        out_shape=[jax.ShapeDtypeStruct((n, Q_COLS), h.dtype),
                   jax.ShapeDtypeStruct((n, 1024), F32),
                   jax.ShapeDtypeStruct((n, 512), F32),
                   jax.ShapeDtypeStruct((n, 1024), h.dtype),
                   jax.ShapeDtypeStruct((n, LANES), F32)],
        compiler_params=_cp(("parallel",), 48),
        name="nsa_proj",
    )(h, w_in_pad)


def _merge_kernel(oc_ref, os_ref, ow_ref, y_ref, wo_ref, gain_ref, router_ref, yo_ref, ho_ref, mg_ref):
    o = (oc_ref[...].astype(F32) + os_ref[...].astype(F32) + ow_ref[...].astype(F32)).astype(wo_ref.dtype)
    y = y_ref[...] + _mm(o, wo_ref[...])
    yo_ref[...] = y
    hf = _rms(y, gain_ref[...])
    ho_ref[...] = hf.astype(ho_ref.dtype)
    logits = _mm(hf.astype(BF16), router_ref[...])
    lane = lax.broadcasted_iota(jnp.int32, logits.shape, 1)
    logits = jnp.where(lane < N_EXPERTS, logits, -jnp.inf)
    m1 = jnp.max(logits, axis=-1, keepdims=True)
    i1 = jnp.min(jnp.where(logits == m1, lane, LANES), axis=-1, keepdims=True)
    rest = jnp.where(lane == i1, -jnp.inf, logits)
    m2 = jnp.max(rest, axis=-1, keepdims=True)
    i2 = jnp.min(jnp.where(rest == m2, lane, LANES), axis=-1, keepdims=True)
    e2 = jnp.exp(m2 - m1)
    den = 1.0 + e2
    mg_ref[...] = jnp.where(lane == i1, 1.0 / den, 0.0) + jnp.where(lane == i2, e2 / den, 0.0)


def _merge(oc, os_, ow, y, w_o, gain, router_pad, tm):
    n = y.shape[0]
    return pl.pallas_call(
        _merge_kernel,
        grid=(n // tm,),
        in_specs=[_rows(tm, D_MODEL)] * 4 + [_resident((D_MODEL, D_MODEL)), _resident((1, D_MODEL)),
                                             _resident((D_MODEL, LANES))],
        out_specs=[_rows(tm, D_MODEL), _rows(tm, D_MODEL), _rows(tm, LANES)],
        out_shape=[jax.ShapeDtypeStruct((n, D_MODEL), F32), jax.ShapeDtypeStruct((n, D_MODEL), BF16),
                   jax.ShapeDtypeStruct((n, LANES), F32)],
        compiler_params=_cp(("parallel",), 48),
        name="nsa_merge_router",
    )(oc, os_, ow, y, w_o, gain.reshape(1, D_MODEL), router_pad)


def _moe_kernel(h_ref, y_ref, mg_ref, win_ref, wout_ref, gain_ref, o_ref, acc_ref):
    e = pl.program_id(1)

    @pl.when(e == 0)
    def _():
        acc_ref[...] = jnp.zeros_like(acc_ref)

    mg = mg_ref[...]
    lane = lax.broadcasted_iota(jnp.int32, mg.shape, 1)
    gcol = jnp.sum(jnp.where(lane == e, mg, 0.0), axis=-1, keepdims=True)
    acc_ref[...] += gcol * _swiglu_acc(h_ref[...], win_ref, wout_ref, EXPERT_FF, lead=(0,))

    @pl.when(e == N_EXPERTS - 1)
    def _():
        o_ref[...] = _rms(y_ref[...] + acc_ref[...], gain_ref[...])


def _moe(h, y, mg, w_in, w_out, gain, tm):
    n = h.shape[0]
    rows = lambda w: pl.BlockSpec((tm, w), lambda i, e: (i, 0))
    return pl.pallas_call(
        _moe_kernel,
        grid=(n // tm, N_EXPERTS),
        in_specs=[rows(D_MODEL), rows(D_MODEL), rows(LANES),
                  pl.BlockSpec((1, D_MODEL, 2 * EXPERT_FF), lambda i, e: (e, 0, 0)),
                  pl.BlockSpec((1, EXPERT_FF, D_MODEL), lambda i, e: (e, 0, 0)),
                  pl.BlockSpec((1, D_MODEL), lambda i, e: (0, 0))],
        out_specs=rows(D_MODEL),
        out_shape=jax.ShapeDtypeStruct((n, D_MODEL), F32),
        scratch_shapes=[pltpu.VMEM((tm, D_MODEL), F32)],
        compiler_params=_cp(("parallel", "arbitrary"), 56),
        name="moe_final_norm",
    )(h, y, mg, w_in, w_out, gain.reshape(1, D_MODEL))


def _t5_bucket_np(dist):
    n = np.maximum(np.asarray(dist, np.int64), 0)
    exact = N_BUCKETS // 2
    logpart = exact + np.floor(np.log(np.maximum(n, 1) / exact) / math.log(MAX_DISTANCE / exact)
                               * (N_BUCKETS - exact)).astype(np.int64)
    return np.where(n < exact, n, np.minimum(logpart, N_BUCKETS - 1)).astype(np.int32)


def _bias_kernel(rb_ref, idx_ref, o_ref):
    h = pl.program_id(0)
    idx = idx_ref[...]
    far = rb_ref[N_BUCKETS - 1, h]
    acc = jnp.zeros(idx.shape, F32)
    for b in range(N_BUCKETS - 1):
        acc = jnp.where(idx == b, rb_ref[b, h] - far, acc)
    o_ref[0] = acc


def _bias_tables(rel_bias, idx):
    m = idx.shape[0]
    return pl.pallas_call(
        _bias_kernel,
        grid=(N_HEADS,),
        in_specs=[pl.BlockSpec(memory_space=pltpu.SMEM), pl.BlockSpec((m, LANES), lambda h: (0, 0))],
        out_specs=pl.BlockSpec((1, m, LANES), lambda h: (h, 0, 0)),
        out_shape=jax.ShapeDtypeStruct((N_HEADS, m, LANES), F32),
        compiler_params=_cp(("arbitrary",), 32),
        name="t5_bias_tables",
    )(rel_bias, idx)


CMP_NEAR = 32


def _prompt_bias_idx():
    a = np.arange(QT)[:, None]
    b = np.arange(QT)[None, :]
    t0 = _t5_bucket_np(a - b)
    t1 = _t5_bucket_np(QT + a - b)
    m = np.arange(LANES)[None, :]
    gc = np.where(m < CMP_NEAR, _t5_bucket_np(a + (CMP_NEAR // 2) * CMP_STRIDE - (CMP_BLOCK - 1) - CMP_STRIDE * m),
                  N_BUCKETS - 1)
    return np.concatenate([t0, t1, gc], axis=0).astype(np.int32)


def _flatten_pairs(load, nh):
    lane = lax.broadcasted_iota(jnp.int32, (nh, LANES), 1)
    ev, od = [], []
    for m in range(CMP_STRIDE // 2):
        sa = load(2 * m)
        sb = load(2 * m + 1)
        ev.append(jnp.where(lane < HEAD_DIM, sa, pltpu.roll(sb, HEAD_DIM, 1)))
        od.append(jnp.where(lane < HEAD_DIM, pltpu.roll(sa, HEAD_DIM, 1), sb))
    return jnp.concatenate(ev, axis=1), jnp.concatenate(od, axis=1)


def _compress_rows(xf, w1_ref, pe_ref, w2_ref, slot):
    rows = xf.shape[0]
    half = CMP_STRIDE * HEAD_DIM
    first = _mm((xf + pe_ref[slot, 0:1, :half]).astype(BF16), w1_ref[slot, :, :CMP_HIDDEN])
    second = _mm((xf + pe_ref[slot, 0:1, half:]).astype(BF16), w1_ref[slot, :, CMP_HIDDEN:])
    act = jax.nn.gelu(first + pltpu.roll(second, rows - 1, 0))
    return _mm(act.astype(BF16), w2_ref[slot])


def _compress_prompt_kernel(x0, x1, x2, x3, w1_ref, pe_ref, w2_ref, o_ref):
    nh = x0.shape[0] // CMP_STRIDE
    xs = (x0, x1, x2, x3)
    for slot in range(2):
        parts = []
        for c in range(2):
            x_ref = xs[2 * slot + c]
            ev, od = _flatten_pairs(lambda s: x_ref[pl.ds(s, nh, stride=CMP_STRIDE), :], nh)
            parts += [ev, od]
        xf = jnp.concatenate(parts, axis=0)
        o_ref[0, slot] = _compress_rows(xf, w1_ref, pe_ref, w2_ref, slot).astype(o_ref.dtype)


def _compress_weights(phi_pe, phi_w1, phi_w2):
    half = CMP_BLOCK // 2
    w1 = jnp.concatenate([phi_w1[:, :half].reshape(2, half * HEAD_DIM, CMP_HIDDEN),
                          phi_w1[:, half:].reshape(2, half * HEAD_DIM, CMP_HIDDEN)], axis=2)
    pe = jnp.broadcast_to(phi_pe.reshape(2, 1, CMP_BLOCK * HEAD_DIM), (2, 8, CMP_BLOCK * HEAD_DIM))
    return w1.astype(BF16), pe, phi_w2.astype(BF16)


def _compress_prompt(kvf, cw, batch, length):
    w1b, pe, w2b = cw
    nh = length // CMP_STRIDE
    xspec = lambda c: pl.BlockSpec((length, LANES), lambda b: (b, c))
    return pl.pallas_call(
        _compress_prompt_kernel,
        grid=(batch,),
        in_specs=[xspec(0), xspec(1), xspec(2), xspec(3), _resident(w1b.shape), _resident(pe.shape),
                  _resident(w2b.shape)],
        out_specs=pl.BlockSpec((1, 2, N_KV * nh, HEAD_DIM), lambda b: (b, 0, 0, 0)),
        out_shape=jax.ShapeDtypeStruct((batch, 2, N_KV * nh, HEAD_DIM), BF16),
        compiler_params=_cp(("parallel",), 48),
        name="nsa_compress_prompt",
    )(kvf, kvf, kvf, kvf, w1b, pe, w2b)


_NT = (((1,), (1,)), ((), ()))


def _cover_t_np(nc_pad, nb_pad):
    ratio = SEL_BLOCK // CMP_STRIDE
    span = CMP_BLOCK // CMP_STRIDE
    off = (np.arange(ratio)[:, None] - np.arange(span)[None, :]).reshape(-1)
    target = np.arange(nb_pad)[:, None] * ratio + off[None, :]
    cov = np.sum(np.arange(nc_pad)[:, None, None] == target[None], axis=-1)
    return np.ascontiguousarray(cov.T).astype(np.float32)


def _rank_select(score, jidx, nsel):
    nb = score.shape[0]
    cnt = jnp.zeros(score.shape, F32)
    for i in range(nb):
        ri = score[i:i + 1, :]
        cnt = cnt + jnp.where((ri > score) | ((ri == score) & (jidx > i)), 1.0, 0.0)
    return cnt < nsel


def _cmp_attn_kernel(q_ref, kcv_ref, gate_ref, ghi_ref, glo_ref, covt_ref, o_ref, sel_ref, bias_ref, selt_ref):
    i = pl.program_id(1)
    q0 = i * QT
    nc = kcv_ref.shape[2] // N_KV
    nb = covt_ref.shape[0]
    base = q0 // CMP_STRIDE - CMP_NEAR // 2
    rr = lax.broadcasted_iota(jnp.int32, (LANES, nc), 0)
    cc = lax.broadcasted_iota(jnp.int32, (LANES, nc), 1)
    place = ((cc == base + rr) & (rr < CMP_NEAR)).astype(BF16)
    bias_ref[...] = (jnp.dot(ghi_ref[...], place, preferred_element_type=F32)
                     + jnp.dot(glo_ref[...], place, preferred_element_type=F32))
    t = q0 + lax.broadcasted_iota(jnp.int32, (QT, nc), 0)
    cend = lax.broadcasted_iota(jnp.int32, (QT, nc), 1) * CMP_STRIDE + (CMP_BLOCK - 1)
    vis = t >= cend
    visf = vis.astype(F32)
    madd = jnp.where(vis, 0.0, NEG)
    gate = gate_ref[...]
    jidx = lax.broadcasted_iota(jnp.int32, (nb, QT), 0)
    tq = q0 + lax.broadcasted_iota(jnp.int32, (nb, QT), 1)
    jt = tq // SEL_BLOCK
    forced = (jidx == 0) | (jidx == jt) | (jidx == jt - 1)
    visb = jidx * SEL_BLOCK <= tq
    for g in range(N_KV):
        kc = kcv_ref[0, 0, g * nc:(g + 1) * nc, :]
        vc = kcv_ref[0, 1, g * nc:(g + 1) * nc, :]
        psum = jnp.zeros((QT, nc), F32)
        for r in range(GQ):
            h = g * GQ + r
            qh = q_ref[:, h * HEAD_DIM:(h + 1) * HEAD_DIM]
            s = lax.dot_general(qh, kc, _NT, preferred_element_type=F32)
            s = s + bias_ref[h * QT:(h + 1) * QT, :] + madd
            m = jnp.max(s, axis=-1, keepdims=True)
            e = jnp.exp(s - m)
            p = ((e * visf) * (1.0 / jnp.sum(e, axis=-1, keepdims=True))).astype(BF16)
            psum = psum + p.astype(F32)
            o = _mm(p, vc)
            o_ref[:, h * HEAD_DIM:(h + 1) * HEAD_DIM] = (o * gate[:, h:h + 1]).astype(o_ref.dtype)
        imp = lax.dot_general(covt_ref[...], psum, _NT, precision=HIGHEST, preferred_element_type=F32)
        score = jnp.where(visb, jnp.where(forced, FORCE, imp), NEG)
        sel = _rank_select(score, jidx, min(TOP_N, nb)) & visb
        selt_ref[g * nb:(g + 1) * nb, :] = sel.astype(F32)
    sel_ref[...] = selt_ref[...].T


def _cmp_attn(q, kcv, gate, ghi, glo, batch, length):
    nq = length // QT
    nc = kcv.shape[2] // N_KV
    nb = length // SEL_BLOCK
    covt = jnp.asarray(_cover_t_np(nc, nb))
    row = lambda w: pl.BlockSpec((QT, w), lambda b, i: (b * nq + i, 0))
    return pl.pallas_call(
        _cmp_attn_kernel,
        grid=(batch, nq),
        in_specs=[row(Q_COLS), pl.BlockSpec((1, 2, N_KV * nc, HEAD_DIM), lambda b, i: (b, 0, 0, 0)), row(LANES),
                  _resident(ghi.shape), _resident(glo.shape), _resident(covt.shape)],
        out_specs=[row(Q_COLS), row(N_KV * nb)],
        out_shape=[jax.ShapeDtypeStruct((batch * length, Q_COLS), BF16),
                   jax.ShapeDtypeStruct((batch * length, N_KV * nb), F32)],
        scratch_shapes=[pltpu.VMEM((N_HEADS * QT, nc), F32), pltpu.VMEM((N_KV * nb, QT), F32)],
        compiler_params=_cp(("parallel", "arbitrary"), 48),
        name="nsa_cmp_attn_select",
    )(q, kcv, gate, ghi, glo, covt)


SEL_CHUNK = 4
SEL_TAIL = SEL_CHUNK + 1
SEL_ZERO, SEL_MASKED = 2, 3


def _selected_kernel(q_ref, kv_ref, gate_ref, stab_ref, sel_ref, ex_ref, o_ref, *scratch):
    i = pl.program_id(1)
    length = kv_ref.shape[0]
    nb = ex_ref.shape[0]
    wide = SEL_CHUNK * QT
    tail = SEL_TAIL * QT
    nbulk = jnp.maximum(i - 1, 0) // SEL_CHUNK
    tstart = jnp.minimum(SEL_CHUNK * nbulk, length // QT - SEL_TAIL)
    trows = pl.ds(pl.multiple_of(tstart * QT, QT), tail)
    rep = lambda a: jnp.concatenate([a] * GQ, axis=0)
    gate = gate_ref[...]
    kinds = []
    for jj in range(SEL_TAIL):
        tile = tstart + jj
        kinds.append(jnp.where((tile < SEL_CHUNK * nbulk) | (tile > i), SEL_MASKED, jnp.minimum(i - tile, SEL_ZERO)))

    kcol = lambda g: slice(g * HEAD_DIM, (g + 1) * HEAD_DIM)
    vcol = lambda g: slice((N_KV + g) * HEAD_DIM, (N_KV + g + 1) * HEAD_DIM)
    q4 = lambda g: jnp.concatenate(
        [q_ref[:, (g * GQ + r) * HEAD_DIM:(g * GQ + r + 1) * HEAD_DIM] for r in range(GQ)], axis=0)

    per_set = len(scratch) // 2
    sets = (scratch[:per_set], scratch[per_set:])

    def prep(g):
        s_ref, mx_ref, mb_ref, ls_ref, acc_ref, madd_ref = sets[g % 2]
        sel = sel_ref[:, g * nb:(g + 1) * nb].astype(BF16)
        madd_ref[...] = (jnp.dot(sel, ex_ref[...], preferred_element_type=F32) - 1.0) * (-NEG)
        mx_ref[...] = jnp.full(mx_ref.shape, -jnp.inf, F32)
        ls_ref[...] = jnp.zeros(ls_ref.shape, F32)
        acc_ref[...] = jnp.zeros(acc_ref.shape, F32)

    def logits_chunk(g, c):
        s_ref, mx_ref, mb_ref, ls_ref, acc_ref, madd_ref = sets[g % 2]
        cols = pl.ds(pl.multiple_of(c * wide, wide), wide)
        s = (lax.dot_general(q4(g), kv_ref[cols, kcol(g)], _NT, preferred_element_type=F32)
             + rep(madd_ref[:, cols]))
        s_ref[:, cols] = s
        mx = mx_ref[...]
        for jj in range(SEL_CHUNK):
            mx = jnp.maximum(mx, s[:, jj * QT:(jj + 1) * QT])
        mx_ref[...] = mx

    def logits_tail(g):
        s_ref, mx_ref, mb_ref, ls_ref, acc_ref, madd_ref = sets[g % 2]
        s = lax.dot_general(q4(g), kv_ref[trows, kcol(g)], _NT, preferred_element_type=F32)
        adds = [jnp.concatenate([stab_ref[g * GQ + r, :, pl.ds(pl.multiple_of(kinds[jj] * QT, QT), QT)]
                                 for r in range(GQ)], axis=0) for jj in range(SEL_TAIL)]
        s = s + jnp.concatenate(adds, axis=1) + rep(madd_ref[:, trows])
        s_ref[:, length:] = s
        mx = mx_ref[...]
        for jj in range(SEL_TAIL):
            mx = jnp.maximum(mx, s[:, jj * QT:(jj + 1) * QT])
        mb_ref[...] = jnp.broadcast_to(jnp.max(mx, axis=-1, keepdims=True), mx.shape)

    def pv_chunk(g, c):
        s_ref, mx_ref, mb_ref, ls_ref, acc_ref, madd_ref = sets[g % 2]
        cols = pl.ds(pl.multiple_of(c * wide, wide), wide)
        m = mb_ref[...]
        ps = [jnp.exp(s_ref[:, pl.ds(pl.multiple_of(c * wide + jj * QT, QT), QT)] - m) for jj in range(SEL_CHUNK)]
        lsum = ls_ref[...]
        for p in ps:
            lsum = lsum + p
        ls_ref[...] = lsum
        acc_ref[...] += jnp.dot(jnp.concatenate(ps, axis=1).astype(BF16), kv_ref[cols, vcol(g)],
                                preferred_element_type=F32)

    def pv_tail(g):
        s_ref, mx_ref, mb_ref, ls_ref, acc_ref, madd_ref = sets[g % 2]
        m = mb_ref[...]
        ps = [jnp.exp(s_ref[:, length + jj * QT:length + (jj + 1) * QT] - m) for jj in range(SEL_TAIL)]
        lsum = ls_ref[...]
        for p in ps:
            lsum = lsum + p
        acc = acc_ref[...] + jnp.dot(jnp.concatenate(ps, axis=1).astype(BF16), kv_ref[trows, vcol(g)],
                                     preferred_element_type=F32)
        l = jnp.sum(lsum, axis=-1, keepdims=True)
        for r in range(GQ):
            h = g * GQ + r
            scale = gate[:, N_HEADS + h:N_HEADS + h + 1] / l[r * QT:(r + 1) * QT]
            o_ref[:, h * HEAD_DIM:(h + 1) * HEAD_DIM] = (acc[r * QT:(r + 1) * QT, :] * scale).astype(o_ref.dtype)

    def loop(*stages):
        def body(c, carry):
            for stage, g in stages:
                stage(g, c)
            return carry
        lax.fori_loop(0, nbulk, body, 0)

    prep(0)
    loop((logits_chunk, 0))
    logits_tail(0)
    for g in range(1, N_KV):
        prep(g)
        loop((logits_chunk, g), (pv_chunk, g - 1))
        pv_tail(g - 1)
        logits_tail(g)
    loop((pv_chunk, N_KV - 1))
    pv_tail(N_KV - 1)


WIN_TILES = WINDOW // QT + 1


def _window_kernel(q_ref, kv_ref, gate_ref, wtab_ref, o_ref):
    i = pl.program_id(1)
    first = jnp.maximum(i - (WIN_TILES - 1), 0)
    rows = pl.ds(pl.multiple_of(first * QT, QT), WIN_TILES * QT)
    gate = gate_ref[...]
    for g in range(N_KV):
        kcol = slice(g * HEAD_DIM, (g + 1) * HEAD_DIM)
        vcol = slice(N_KV * HEAD_DIM + g * HEAD_DIM, N_KV * HEAD_DIM + (g + 1) * HEAD_DIM)
        q4 = jnp.concatenate([q_ref[:, (g * GQ + r) * HEAD_DIM:(g * GQ + r + 1) * HEAD_DIM] for r in range(GQ)],
                             axis=0)
        s = lax.dot_general(q4, kv_ref[rows, kcol], _NT, preferred_element_type=F32)
        adds = []
        for jj in range(WIN_TILES):
            d = i - first - jj
            kind = jnp.where(d < 0, WIN_TILES, d)
            col = pl.ds(pl.multiple_of(kind * QT, QT), QT)
            adds.append(jnp.concatenate([wtab_ref[g * GQ + r, :, col] for r in range(GQ)], axis=0))
        s = s + jnp.concatenate(adds, axis=1)
        p = jnp.exp(s - jnp.max(s, axis=-1, keepdims=True))
        l = jnp.sum(p, axis=-1, keepdims=True)
        acc = jnp.dot(p.astype(BF16), kv_ref[rows, vcol], preferred_element_type=F32)
        for r in range(GQ):
            h = g * GQ + r
            scale = gate[:, 2 * N_HEADS + h:2 * N_HEADS + h + 1] / l[r * QT:(r + 1) * QT]
            o_ref[:, h * HEAD_DIM:(h + 1) * HEAD_DIM] = (acc[r * QT:(r + 1) * QT, :] * scale).astype(o_ref.dtype)


def _window_tables(tab):
    a = np.arange(QT)[:, None]
    b = np.arange(QT)[None, :]
    causal = np.where(a >= b, 0.0, NEG).astype(np.float32)
    edge = np.where(a <= b, 0.0, NEG).astype(np.float32)
    const = lambda m: jnp.broadcast_to(jnp.asarray(m, F32), (N_HEADS, QT, QT))
    zero = np.zeros((QT, QT), np.float32)
    return jnp.concatenate([tab[:, :QT] + causal, tab[:, QT:], const(zero), const(zero), const(edge),
                            const(np.full((QT, QT), NEG, np.float32))], axis=2)


def _window_attn(q, kvb, gate, wtab, batch, length):
    nq = length // QT
    row = lambda w: pl.BlockSpec((QT, w), lambda b, i: (b * nq + i, 0))
    return pl.pallas_call(
        _window_kernel,
        grid=(batch, nq),
        in_specs=[row(Q_COLS), pl.BlockSpec((length, 2 * N_KV * HEAD_DIM), lambda b, i: (b, 1)), row(LANES),
                  _resident(wtab.shape)],
        out_specs=row(Q_COLS),
        out_shape=jax.ShapeDtypeStruct((batch * length, Q_COLS), BF16),
        compiler_params=_cp(("parallel", "arbitrary"), 48),
        name="nsa_window_attn",
    )(q, kvb, gate, wtab)


def _ex_np(nb, length):
    return (np.arange(nb)[:, None] == (np.arange(length)[None, :] // SEL_BLOCK)).astype(np.float32)


def _selected_tables(tab):
    a = np.arange(QT)[:, None]
    b = np.arange(QT)[None, :]
    causal = np.where(a >= b, 0.0, NEG).astype(np.float32)
    const = lambda v: jnp.full((N_HEADS, QT, QT), v, F32)
    return jnp.concatenate([tab[:, :QT] + causal, tab[:, QT:], const(0.0), const(NEG)], axis=2)


def _selected_attn(q, kvb, gate, stab, selmask, batch, length):
    nq = length // QT
    nb = length // SEL_BLOCK
    assert nq >= SEL_TAIL
    row = lambda w: pl.BlockSpec((QT, w), lambda b, i: (b * nq + i, 0))
    ex = jnp.asarray(_ex_np(nb, length), BF16)
    rows4 = GQ * QT
    return pl.pallas_call(
        _selected_kernel,
        grid=(batch, nq),
        in_specs=[row(Q_COLS), pl.BlockSpec((length, 2 * N_KV * HEAD_DIM), lambda b, i: (b, 0)), row(LANES),
                  _resident(stab.shape), row(N_KV * nb), _resident(ex.shape)],
        out_specs=row(Q_COLS),
        out_shape=jax.ShapeDtypeStruct((batch * length, Q_COLS), BF16),
        scratch_shapes=2 * [pltpu.VMEM((rows4, length + SEL_TAIL * QT), F32),
                            pltpu.VMEM((rows4, QT), F32),
                            pltpu.VMEM((rows4, QT), F32),
                            pltpu.VMEM((rows4, QT), F32),
                            pltpu.VMEM((rows4, HEAD_DIM), F32),
                            pltpu.VMEM((QT, length), F32)],
        compiler_params=_cp(("parallel", "arbitrary"), 56),
        name="nsa_selected_attn",
    )(q, kvb, gate, stab, selmask, ex)


def _s5_sample_kernel(u_ref, hr_ref, hi_ref, ws_ref, wv_ref, a1_ref, d_ref, g_ref, sre_ref, sim_ref):
    u = u_ref[...]
    x = _mm(u.astype(BF16), ws_ref[0, 0])
    ar, ai = a1_ref[0, :, :S5_SL], a1_ref[0, :, S5_SL:]
    h0r, h0i = hr_ref[...], hi_ref[...]
    hr = x[:, :S5_SL] + (ar * h0r - ai * h0i)
    hi = x[:, S5_SL:] + (ar * h0i + ai * h0r)
    sre_ref[...] = hr
    sim_ref[...] = hi
    y = _mm(jnp.concatenate([hr, hi], axis=1).astype(BF16), wv_ref[0, 0])
    g_ref[...] = jax.nn.gelu(y + d_ref[...] * u).astype(g_ref.dtype)


def _s5_sample(u, h0_re, h0_im, ws, wv, a1, d_skip):
    n = u.shape[0]
    st = pl.BlockSpec((n, S5_SL), lambda s: (0, s))
    return pl.pallas_call(
        _s5_sample_kernel,
        grid=(S5_NSG,),
        in_specs=[pl.BlockSpec((n, LANES), lambda s: (0, s)), st, st,
                  pl.BlockSpec((1, 1, LANES, 2 * S5_SL), lambda s: (S5_T - 1, s, 0, 0)),
                  pl.BlockSpec((1, 1, 2 * S5_SL, LANES), lambda s: (0, s, 0, 0)),
                  pl.BlockSpec((1, 1, 2 * S5_SL), lambda s: (s, 0, 0)),
                  pl.BlockSpec((1, LANES), lambda s: (0, s))],
        out_specs=[pl.BlockSpec((n, LANES), lambda s: (0, s)), st, st],
        out_shape=[jax.ShapeDtypeStruct((n, D_MODEL), BF16),
                   jax.ShapeDtypeStruct((n, N_GROUPS * STATE), F32),
                   jax.ShapeDtypeStruct((n, N_GROUPS * STATE), F32)],
        compiler_params=_cp(("parallel",), 32),
        name="s5_sample",
    )(u, h0_re, h0_im, ws, wv, a1, d_skip.reshape(1, D_MODEL))


def _softmax_lanes(s, valid):
    m = jnp.max(s, axis=-1, keepdims=True)
    e = jnp.exp(s - m)
    return (e * valid) * (1.0 / jnp.sum(e, axis=-1, keepdims=True))


def _own_group(o):
    rowg = lax.broadcasted_iota(jnp.int32, (N_HEADS, HEAD_DIM), 0) // GQ
    out = jnp.zeros((N_HEADS, HEAD_DIM), F32)
    for g in range(N_KV):
        out = out + jnp.where(rowg == g, o[:, g * HEAD_DIM:(g + 1) * HEAD_DIM], 0.0)
    return out


DECODE_PAGES_PER_STEP = 4


def _nsa_decode_kernel(pt_ref, *refs, past_len, n_cmp):
    x_refs = refs[:DECODE_PAGES_PER_STEP]
    (win_ref, qbd_ref, new_ref, gate_ref, w1_ref, pe_ref, w2_ref, cov_ref, bias_ref, ex_ref, o_ref,
     xt_ref, xf_ref, ks_ref, vs_ref, kw_ref, vw_ref) = refs[DECODE_PAGES_PER_STEP:]
    step = pl.program_id(1)
    nsteps = pl.num_programs(1)
    page = x_refs[0].shape[2]
    hp = page // CMP_STRIDE
    nh = past_len // CMP_STRIDE
    nkv = N_KV * HEAD_DIM
    sel_cols = ks_ref.shape[1]
    win_cols = kw_ref.shape[1]
    wb = win_ref.shape[2]

    def stash(ref, cols, x):
        ref[:, cols] = x.astype(BF16)

    for k, x_ref in enumerate(x_refs):
        p = step * len(x_refs) + k
        xt = x_ref[0, :2 * nkv, :].T
        for c in range(2 * nkv // LANES):
            xt_ref[k, c] = xt[:, c * LANES:(c + 1) * LANES]
        for slot in range(2):
            for c in range(2):
                blk = 2 * slot + c
                ev, od = _flatten_pairs(lambda s: xt_ref[k, blk, pl.ds(s, hp, stride=CMP_STRIDE), :], hp)
                for head, val in ((2 * c, ev), (2 * c + 1, od)):
                    xf_ref[slot, pl.ds(pl.multiple_of(head * nh + p * hp, hp), hp), :] = val
        cols = pl.ds(pl.multiple_of(p * page, page), page)
        stash(ks_ref, cols, x_ref[0, 2 * nkv:3 * nkv, :])
        stash(vs_ref, cols, x_ref[0, 3 * nkv:, :])

    @pl.when(step == nsteps - 1)
    def _():
        qbd = qbd_ref[0]
        new = new_ref[0]
        lane0 = lax.broadcasted_iota(jnp.int32, (nkv, LANES), 1) == 0
        newcol = lambda k: jnp.where(lane0, jnp.broadcast_to(new[:, k:k + 1], (nkv, LANES)), 0.0)
        stash(ks_ref, slice(past_len, sel_cols), newcol(0))
        stash(vs_ref, slice(past_len, sel_cols), newcol(1))
        stash(kw_ref, slice(0, wb), win_ref[0, :nkv, :])
        stash(vw_ref, slice(0, wb), win_ref[0, nkv:, :])
        stash(kw_ref, slice(wb, win_cols), newcol(2))
        stash(vw_ref, slice(wb, win_cols), newcol(3))
        rowh = lax.broadcasted_iota(jnp.int32, (N_HEADS, LANES), 0)

        kc = _compress_rows(xf_ref[0], w1_ref, pe_ref, w2_ref, 0).astype(BF16)
        vc = _compress_rows(xf_ref[1], w1_ref, pe_ref, w2_ref, 1).astype(BF16)
        s = jnp.zeros((N_HEADS, nh), F32)
        for g in range(N_KV):
            s = s + lax.dot_general(qbd[:, g * HEAD_DIM:(g + 1) * HEAD_DIM], kc[g * nh:(g + 1) * nh], _NT,
                                    preferred_element_type=F32)
        nidx = lax.broadcasted_iota(jnp.int32, (N_HEADS, nh), 1)
        s = s + bias_ref[:, :nh] + jnp.where(nidx < n_cmp, 0.0, NEG)
        pc = _softmax_lanes(s, (nidx < n_cmp).astype(F32)).astype(BF16)
        o_c = _own_group(jnp.concatenate([_mm(pc, vc[g * nh:(g + 1) * nh]) for g in range(N_KV)], axis=1))

        pcf = pc.astype(F32)
        psum = jnp.concatenate([jnp.sum(pcf[g * GQ:(g + 1) * GQ], axis=0, keepdims=True) for g in range(N_KV)]
                               + [jnp.zeros((8 - N_KV, nh), F32)], axis=0)
        imp = jnp.dot(psum, cov_ref[...], precision=HIGHEST, preferred_element_type=F32)
        jt = past_len // SEL_BLOCK
        nb = jt + 1
        lane = lax.broadcasted_iota(jnp.int32, (LANES, LANES), 1)
        subl = lax.broadcasted_iota(jnp.int32, (LANES, LANES), 0)

        def scored(x, j):
            forced = (j == 0) | (j == jt) | (j == jt - 1)
            return jnp.where(j < nb, jnp.where(forced, FORCE, x), NEG)

        imp_sq = jnp.concatenate([imp, jnp.zeros((LANES - 8, LANES), F32)], axis=0)
        sc_row = scored(imp_sq, lane)
        sc_col = scored(imp_sq.T, subl)
        selh = jnp.zeros((N_HEADS, LANES), F32)
        for g in range(N_KV):
            col = sc_col[:, g:g + 1]
            row = sc_row[g:g + 1, :]
            ahead = (col > row) | ((col == row) & (subl < lane))
            rank = jnp.sum(jnp.where(ahead, 1.0, 0.0), axis=0, keepdims=True)
            chosen = jnp.where((rank < min(TOP_N, nb)) & (lane[:1] < nb), 1.0, 0.0)
            selh = selh + jnp.where(rowh // GQ == g, chosen, 0.0)

        pos = lax.broadcasted_iota(jnp.int32, (N_HEADS, sel_cols), 1)
        inblk = jnp.dot(selh.astype(BF16), ex_ref[...], preferred_element_type=F32)
        ok_s = (inblk > 0.5) & (pos <= past_len)
        s = _mm(qbd, ks_ref[...]) + bias_ref[:, nh:nh + sel_cols]
        ps = _softmax_lanes(jnp.where(ok_s, s, NEG), ok_s.astype(F32)).astype(BF16)
        o_s = _own_group(lax.dot_general(ps, vs_ref[...], _NT, preferred_element_type=F32))

        posw = lax.broadcasted_iota(jnp.int32, (N_HEADS, win_cols), 1)
        ok_w = posw <= wb
        s = _mm(qbd, kw_ref[...]) + bias_ref[:, nh + sel_cols:nh + sel_cols + win_cols]
        pw = _softmax_lanes(jnp.where(ok_w, s, NEG), ok_w.astype(F32)).astype(BF16)
        o_w = _own_group(lax.dot_general(pw, vw_ref[...], _NT, preferred_element_type=F32))

        gate = gate_ref[0]
        o_ref[0] = gate[:, 0:1] * o_c + gate[:, 1:2] * o_s + gate[:, 2:3] * o_w


def _decode_bias_idx(past_len, wb, ncp, sel_cols, win_cols):
    n = np.arange(ncp)
    cmp_d = past_len - (n * CMP_STRIDE + CMP_BLOCK - 1)
    sel_d = past_len - np.arange(sel_cols)
    win_d = np.concatenate([wb - np.arange(wb), np.zeros(win_cols - wb, np.int64)])
    idx = _t5_bucket_np(np.concatenate([cmp_d, sel_d, win_d]))
    pad = -len(idx) % (8 * LANES)
    return np.concatenate([idx, np.full(pad, N_BUCKETS - 1)]).reshape(-1, LANES).astype(np.int32)


def _nsa_decode(cache_kv, cache_win, page_table, qbd, new_cols, gate3, cw, rel_bias):
    n_pool, page = cache_kv.shape[:2]
    nseq, npages = page_table.shape
    past_len = npages * page
    wb = cache_win.shape[1]
    w1b, pe, w2b = cw
    nh = past_len // CMP_STRIDE
    n_cmp = (past_len + 1 - CMP_BLOCK) // CMP_STRIDE + 1
    sel_cols = past_len + LANES
    win_cols = wb + LANES
    nkv = N_KV * HEAD_DIM
    kvt = cache_kv.transpose(0, 2, 3, 4, 1).reshape(n_pool, 4 * nkv, page)
    wint = cache_win.transpose(0, 2, 3, 4, 1).reshape(nseq, 2 * nkv, wb)
    cov = jnp.asarray(_cover_t_np(LANES, LANES).T)
    idx = _decode_bias_idx(past_len, wb, nh, sel_cols, win_cols)
    bias = _bias_tables(rel_bias, jnp.asarray(idx)).reshape(N_HEADS, idx.size)
    ex = jnp.asarray((np.arange(LANES)[:, None] == np.arange(sel_cols)[None, :] // SEL_BLOCK).astype(np.float32), BF16)
    const = lambda a: pl.BlockSpec(a.shape, lambda b, p, pt: (0,) * a.ndim)
    pps = DECODE_PAGES_PER_STEP
    page_spec = lambda k: pl.BlockSpec((1, 4 * nkv, page), lambda b, p, pt: (pt[b, p * pps + k], 0, 0))
    grid_spec = pltpu.PrefetchScalarGridSpec(
        num_scalar_prefetch=1,
        grid=(nseq, npages // pps),
        in_specs=[page_spec(k) for k in range(pps)] + [
                  pl.BlockSpec((1, 2 * nkv, wb), lambda b, p, pt: (b, 0, 0)),
                  pl.BlockSpec((1, N_HEADS, nkv), lambda b, p, pt: (b, 0, 0)),
                  pl.BlockSpec((1, nkv, LANES), lambda b, p, pt: (b, 0, 0)),
                  pl.BlockSpec((1, N_HEADS, LANES), lambda b, p, pt: (b, 0, 0)),
                  const(w1b), const(pe), const(w2b), const(cov), const(bias), const(ex)],
        out_specs=pl.BlockSpec((1, N_HEADS, HEAD_DIM), lambda b, p, pt: (b, 0, 0)),
        scratch_shapes=[pltpu.VMEM((pps, 2 * nkv // LANES, page, LANES), F32),
                        pltpu.VMEM((2, N_KV * nh, CMP_STRIDE * HEAD_DIM), F32),
                        pltpu.VMEM((nkv, sel_cols), BF16), pltpu.VMEM((nkv, sel_cols), BF16),
                        pltpu.VMEM((nkv, win_cols), BF16), pltpu.VMEM((nkv, win_cols), BF16)])
    out = pl.pallas_call(
        functools.partial(_nsa_decode_kernel, past_len=past_len, n_cmp=n_cmp),
        grid_spec=grid_spec,
        out_shape=jax.ShapeDtypeStruct((nseq, N_HEADS, HEAD_DIM), F32),
        compiler_params=_cp(("arbitrary", "arbitrary"), 56),
        name="nsa_decode",
    )(page_table, *([kvt] * pps), wint, qbd, new_cols, gate3, w1b, pe, w2b, cov, bias, ex)
    return out.reshape(nseq, Q_COLS)


def _decode_operands(qs, kvbs, gates):
    nseq = qs.shape[0]
    nkv = N_KV * HEAD_DIM
    own = (np.arange(N_HEADS)[:, None] // GQ) == (np.arange(nkv)[None, :] // HEAD_DIM)
    qbd = jnp.where(own[None], jnp.tile(qs.reshape(nseq, N_HEADS, HEAD_DIM), (1, 1, N_KV)), 0.0)
    new_cols = jnp.pad(kvbs.astype(F32).reshape(nseq, 4, nkv).transpose(0, 2, 1), ((0, 0), (0, 0), (0, LANES - 4)))
    gate3 = jnp.pad(gates[:, :GATE_COLS].reshape(nseq, 3, N_HEADS).transpose(0, 2, 1),
                    ((0, 0), (0, 0), (0, LANES - 3)))
    return qbd, new_cols, gate3


PROMPT_TM = 512


def kernel(x_prompt, x_sample, state_s5_re, state_s5_im, cache_kv, cache_win, page_table, rel_bias, norm_mix, norm_ffn, norm_final, s5_lam_re, s5_lam_im, s5_log_dt, s5_b_re, s5_b_im, s5_c_re, s5_c_im, s5_d, s5_w_glu, ffn_w_in, ffn_w_out, nsa_w_in, nsa_phi_pe, nsa_phi_w1, nsa_phi_w2, nsa_w_o, moe_router, moe_w_in, moe_w_out):
    batch, length, _ = x_prompt.shape
    nseq = x_sample.shape[0]
    xp = x_prompt.reshape(batch * length, D_MODEL)
    xs = x_sample.reshape(nseq, D_MODEL)
    nstate = N_GROUPS * STATE

    wk, ws, wv, apow, a1 = _s5_prep(s5_lam_re, s5_lam_im, s5_log_dt, s5_b_re, s5_b_im, s5_c_re, s5_c_im)
    w_glu = s5_w_glu.astype(BF16)
    ffn_in, ffn_out = ffn_w_in.astype(BF16), ffn_w_out.astype(BF16)
    nsa_in = jnp.pad(nsa_w_in, ((0, 0), (0, NSA_PAD - nsa_w_in.shape[1]))).astype(BF16)
    w_o = nsa_w_o.astype(BF16)
    router = jnp.pad(moe_router, ((0, 0), (0, LANES - N_EXPERTS))).astype(BF16)
    moe_in, moe_out = moe_w_in.astype(BF16), moe_w_out.astype(BF16)
    cw = _compress_weights(nsa_phi_pe, nsa_phi_w1, nsa_phi_w2)
    tabs = _bias_tables(rel_bias, jnp.asarray(_prompt_bias_idx()))
    tab = tabs[:, :2 * QT]
    gflat = tabs[:, 2 * QT:].reshape(N_HEADS * QT, LANES)
    ghi = gflat.astype(BF16)
    glo = (gflat - ghi.astype(F32)).astype(BF16)

    up = _rmsnorm(xp, norm_mix[0], PROMPT_TM)
    gp, sre_p, sim_p = _s5_prompt(up, s5_d, wk, ws, wv, apow, batch, length)
    y1p, h1p = _glu(gp, xp, w_glu, norm_ffn[0], PROMPT_TM)
    y2p, h2p = _ffn(h1p, y1p, ffn_in, ffn_out, norm_mix[1], PROMPT_TM)
    q, kvf, winf, kvb, gate = _nsa_proj(h2p, nsa_in, PROMPT_TM)
    kcv = _compress_prompt(kvf, cw, batch, length)
    oc, sel = _cmp_attn(q, kcv, gate, ghi, glo, batch, length)
    os_ = _selected_attn(q, kvb, gate, _selected_tables(tab), sel, batch, length)
    ow = _window_attn(q, kvb, gate, _window_tables(tab), batch, length)
    y3p, h3p, mgp = _merge(oc, os_, ow, y2p, w_o, norm_ffn[1], router, PROMPT_TM)
    yp = _moe(h3p, y3p, mgp, moe_in, moe_out, norm_final, PROMPT_TM)

    us = _rmsnorm(xs, norm_mix[0], nseq)
    gs, sre_s, sim_s = _s5_sample(us, state_s5_re.reshape(nseq, nstate), state_s5_im.reshape(nseq, nstate),
                                  ws, wv, a1, s5_d)
    y1s, h1s = _glu(gs, xs, w_glu, norm_ffn[0], nseq)
    y2s, h2s = _ffn(h1s, y1s, ffn_in, ffn_out, norm_mix[1], nseq)
    qs, kvfs, winfs, kvbs, gates = _nsa_proj(h2s, nsa_in, nseq)
    qbd, new_cols, gate3 = _decode_operands(qs, kvbs, gates)
    od = _nsa_decode(cache_kv, cache_win, page_table, qbd, new_cols, gate3, cw, rel_bias)
    zero = jnp.zeros((nseq, D_MODEL), F32)
    y3s, h3s, mgs = _merge(od, zero, zero, y2s, w_o, norm_ffn[1], router, nseq)
    ys = _moe(h3s, y3s, mgs, moe_in, moe_out, norm_final, nseq)

    wp = min(WINDOW, length)
    return (yp.reshape(batch, length, D_MODEL),
            ys.reshape(nseq, 1, D_MODEL),
            sre_p.reshape(batch, N_GROUPS, STATE), sim_p.reshape(batch, N_GROUPS, STATE),
            kvf.reshape(batch, length, 4, N_KV, HEAD_DIM),
            winf.reshape(batch, length, 2, N_KV, HEAD_DIM)[:, length - wp:],
            sre_s.reshape(nseq, N_GROUPS, STATE), sim_s.reshape(nseq, N_GROUPS, STATE),
            kvfs.reshape(nseq, 1, 4, N_KV, HEAD_DIM),
            winfs.reshape(nseq, 1, 2, N_KV, HEAD_DIM))
```

```python
import functools
import math

import numpy as np
import jax
import jax.numpy as jnp
from jax import lax
from jax.experimental import pallas as pl
from jax.experimental.pallas import tpu as pltpu

F32 = jnp.float32
BF16 = jnp.bfloat16
HIGHEST = lax.Precision.HIGHEST

D_MODEL = 1024
GROUP_CH = 16
N_GROUPS = D_MODEL // GROUP_CH
STATE = 64
N_HEADS = 16
HEAD_DIM = 64
N_KV = 4
GQ = N_HEADS // N_KV
CMP_BLOCK = 32
CMP_STRIDE = 16
CMP_HIDDEN = 2 * HEAD_DIM
SEL_BLOCK = 64
TOP_N = 16
WINDOW = 512
Q_COLS = N_HEADS * HEAD_DIM
KV_COLS = 6 * N_KV * HEAD_DIM
GATE_COLS = 3 * N_HEADS
N_BUCKETS = 32
MAX_DISTANCE = 128
D_FF = 2816
N_EXPERTS = 8
EXPERT_FF = 1408
EPS = 1e-6
NEG = -1e30
FORCE = 1e4

LANES = 128
S5_T = 8
S5_SG = LANES // GROUP_CH
S5_NSG = N_GROUPS // S5_SG
S5_SL = S5_SG * STATE
QT = 128


def _cp(sem, vmem_mb):
    return pltpu.CompilerParams(dimension_semantics=sem, vmem_limit_bytes=vmem_mb << 20)


def _cmul(ar, ai, br, bi):
    return ar * br - ai * bi, ar * bi + ai * br


def _s5_prep_kernel(lr_r, li_r, ldt_r, br_r, bi_r, lr_c, li_c, ldt_c, cr_c, ci_c,
                    wk_ref, ws_ref, wv_ref, apow_ref, a1_ref):
    def disc(lr, li, ldt):
        dt = jnp.exp(ldt)
        mag = jnp.exp(lr * dt)
        abr = mag * jnp.cos(li * dt)
        abi = mag * jnp.sin(li * dt)
        den = lr * lr + li * li
        nr = abr - 1.0
        fr = (nr * lr + abi * li) / den
        fi = (abi * lr - nr * li) / den
        return abr, abi, fr, fi

    abr, abi, fr, fi = disc(lr_r[0], li_r[0], ldt_r[0])
    rg = lax.broadcasted_iota(jnp.int32, (LANES, S5_SL), 0) // GROUP_CH
    cg = lax.broadcasted_iota(jnp.int32, (LANES, S5_SL), 1) // STATE
    mask_r = (rg == cg).astype(F32)
    b_r, b_i = br_r[...], bi_r[...]
    bbr = (fr * b_r - fi * b_i) * mask_r
    bbi = (fr * b_i + fi * b_r) * mask_r
    abr_c, abi_c, _, _ = disc(lr_c[0], li_c[0], ldt_c[0])
    rg2 = lax.broadcasted_iota(jnp.int32, (S5_SL, LANES), 0) // STATE
    cg2 = lax.broadcasted_iota(jnp.int32, (S5_SL, LANES), 1) // GROUP_CH
    mask_c = (rg2 == cg2).astype(F32)
    c_r = cr_c[0] * mask_c
    c_i = ci_c[0] * mask_c

    a1_ref[0, :, :S5_SL] = abr
    a1_ref[0, :, S5_SL:] = abi

    pr = jnp.ones_like(abr)
    pi = jnp.zeros_like(abr)
    pr_c = jnp.ones_like(abr_c)
    pi_c = jnp.zeros_like(abr_c)
    for tau in range(S5_T + 1):
        wv_ref[tau, 0, :S5_SL, :] = (c_r * pr_c - c_i * pi_c).astype(BF16)
        wv_ref[tau, 0, S5_SL:, :] = (-(c_r * pi_c + c_i * pr_c)).astype(BF16)
        if tau < S5_T:
            er, ei = _cmul(pr, pi, bbr, bbi)
            s = S5_T - 1 - tau
            ws_ref[s, 0, :, :S5_SL] = er.astype(BF16)
            ws_ref[s, 0, :, S5_SL:] = ei.astype(BF16)
            k = (jnp.dot(er, c_r, precision=HIGHEST, preferred_element_type=F32)
                 - jnp.dot(ei, c_i, precision=HIGHEST, preferred_element_type=F32))
            wk_ref[tau, 0] = k.astype(BF16)
        pr, pi = _cmul(pr, pi, abr, abi)
        pr_c, pi_c = _cmul(pr_c, pi_c, abr_c, abi_c)
    tr, ti = jnp.ones_like(abr), jnp.zeros_like(abr)
    for _ in range(S5_T):
        tr, ti = _cmul(tr, ti, abr, abi)
    qr, qi = jnp.ones_like(abr), jnp.zeros_like(abr)
    for i in range(S5_T + 1):
        apow_ref[0, i:i + 1, :S5_SL] = qr
        apow_ref[0, i:i + 1, S5_SL:] = qi
        qr, qi = _cmul(qr, qi, tr, ti)


def _s5_prep(lam_re, lam_im, log_dt, b_re, b_im, c_re, c_im):
    n = S5_NSG
    row = lambda a: a.reshape(n, 1, S5_SL)
    col = lambda a: a.reshape(n, S5_SL, 1)
    ldt = jnp.repeat(log_dt, STATE)
    bt = lambda b: jnp.tile(b.transpose(0, 2, 1).reshape(D_MODEL, STATE), (1, S5_SG))
    ct = lambda c: jnp.tile(c.reshape(n, LANES, STATE).transpose(0, 2, 1), (1, S5_SG, 1))
    rspec = pl.BlockSpec((1, 1, S5_SL), lambda s: (s, 0, 0))
    cspec = pl.BlockSpec((1, S5_SL, 1), lambda s: (s, 0, 0))
    t1 = S5_T + 1
    return pl.pallas_call(
        _s5_prep_kernel,
        grid=(n,),
        in_specs=[rspec, rspec, rspec,
                  pl.BlockSpec((LANES, S5_SL), lambda s: (s, 0)), pl.BlockSpec((LANES, S5_SL), lambda s: (s, 0)),
                  cspec, cspec, cspec,
                  pl.BlockSpec((1, S5_SL, LANES), lambda s: (s, 0, 0)), pl.BlockSpec((1, S5_SL, LANES), lambda s: (s, 0, 0))],
        out_specs=[pl.BlockSpec((S5_T, 1, LANES, LANES), lambda s: (0, s, 0, 0)),
                   pl.BlockSpec((S5_T, 1, LANES, 2 * S5_SL), lambda s: (0, s, 0, 0)),
                   pl.BlockSpec((t1, 1, 2 * S5_SL, LANES), lambda s: (0, s, 0, 0)),
                   pl.BlockSpec((1, t1 + 7, 2 * S5_SL), lambda s: (s, 0, 0)),
                   pl.BlockSpec((1, 1, 2 * S5_SL), lambda s: (s, 0, 0))],
        out_shape=[jax.ShapeDtypeStruct((S5_T, n, LANES, LANES), BF16),
                   jax.ShapeDtypeStruct((S5_T, n, LANES, 2 * S5_SL), BF16),
                   jax.ShapeDtypeStruct((t1, n, 2 * S5_SL, LANES), BF16),
                   jax.ShapeDtypeStruct((n, t1 + 7, 2 * S5_SL), F32),
                   jax.ShapeDtypeStruct((n, 1, 2 * S5_SL), F32)],
        compiler_params=_cp(("arbitrary",), 48),
        name="s5_prep",
    )(row(lam_re), row(lam_im), row(ldt), bt(b_re), bt(b_im),
      col(lam_re), col(lam_im), col(ldt), ct(c_re), ct(c_im))


def _rms(x, g):
    return x * lax.rsqrt(jnp.mean(x * x, axis=-1, keepdims=True) + EPS) * g


def _rmsnorm_kernel(x_ref, g_ref, o_ref):
    o_ref[...] = _rms(x_ref[...], g_ref[...]).astype(o_ref.dtype)


def _rmsnorm(x, g, tm, dtype=F32):
    n = x.shape[0]
    return pl.pallas_call(
        _rmsnorm_kernel,
        grid=(n // tm,),
        in_specs=[pl.BlockSpec((tm, D_MODEL), lambda i: (i, 0)), pl.BlockSpec((1, D_MODEL), lambda i: (0, 0))],
        out_specs=pl.BlockSpec((tm, D_MODEL), lambda i: (i, 0)),
        out_shape=jax.ShapeDtypeStruct((n, D_MODEL), dtype),
        compiler_params=_cp(("parallel",), 32),
        name="rmsnorm",
    )(x, g.reshape(1, D_MODEL))


def _s5_prompt_kernel(u_ref, d_ref, wk_ref, ws_ref, wv_ref, apow_ref, g_ref, sre_ref, sim_ref,
                      ub_ref, s_ref, p_ref, gc_ref, hs_ref, y_ref):
    length = u_ref.shape[0]
    r = length // S5_T
    ng = r // 8

    acc = jnp.zeros((r, 2 * S5_SL), F32)
    for s in range(S5_T):
        us = u_ref[pl.ds(s, r, stride=S5_T), :].astype(BF16)
        ub_ref[s] = us
        acc = acc + jnp.dot(us, ws_ref[s, 0], preferred_element_type=F32)
    for c in range(2 * S5_SL // LANES):
        s_ref[c] = acc[:, c * LANES:(c + 1) * LANES]

    nsl = S5_SL // LANES
    for c in range(nsl):
        lr = slice(c * LANES, (c + 1) * LANES)
        li = slice(S5_SL + c * LANES, S5_SL + (c + 1) * LANES)

        def apw(i):
            return apow_ref[0, i:i + 1, lr], apow_ref[0, i:i + 1, li]

        a1r, a1i = apw(1)
        pr = s_ref[c, pl.ds(0, ng, stride=8), :]
        pi = s_ref[nsl + c, pl.ds(0, ng, stride=8), :]
        p_ref[0, c] = pr
        p_ref[0, nsl + c] = pi
        for i in range(1, 8):
            qr, qi = _cmul(a1r, a1i, pr, pi)
            pr = qr + s_ref[c, pl.ds(i, ng, stride=8), :]
            pi = qi + s_ref[nsl + c, pl.ds(i, ng, stride=8), :]
            p_ref[i, c] = pr
            p_ref[i, nsl + c] = pi

        a8r, a8i = apw(8)

        def carry(m, g):
            gr, gi = g
            gc_ref[c, pl.ds(m, 1), :] = gr
            gc_ref[nsl + c, pl.ds(m, 1), :] = gi
            nr, ni = _cmul(a8r, a8i, gr, gi)
            return (nr + p_ref[7, c, pl.ds(m, 1), :], ni + p_ref[7, nsl + c, pl.ds(m, 1), :])

        z = jnp.zeros((1, LANES), F32)
        er, ei = lax.fori_loop(0, ng, carry, (z, z))
        sre_ref[0, :, lr] = er
        sim_ref[0, :, lr] = ei

        gr = gc_ref[c]
        gi = gc_ref[nsl + c]
        hs_ref[c, pl.ds(0, ng, stride=8), :] = gr
        hs_ref[nsl + c, pl.ds(0, ng, stride=8), :] = gi
        for i in range(1, 8):
            air, aii = apw(i)
            hr, hi = _cmul(air, aii, gr, gi)
            hs_ref[c, pl.ds(i, ng, stride=8), :] = hr + p_ref[i - 1, c]
            hs_ref[nsl + c, pl.ds(i, ng, stride=8), :] = hi + p_ref[i - 1, nsl + c]

    hs = jnp.concatenate([hs_ref[c] for c in range(2 * S5_SL // LANES)], axis=1).astype(BF16)
    for t in range(S5_T):
        o = jnp.dot(hs, wv_ref[t + 1, 0], preferred_element_type=F32)
        for s in range(t + 1):
            o = o + jnp.dot(ub_ref[s], wk_ref[t - s, 0], preferred_element_type=F32)
        y_ref[pl.ds(t, r, stride=S5_T), :] = o
    y = y_ref[...] + d_ref[...] * u_ref[...]
    g_ref[...] = jax.nn.gelu(y).astype(g_ref.dtype)


def _s5_prompt(u, d_skip, wk, ws, wv, apow, batch, length):
    r = length // S5_T
    t1 = S5_T + 1
    return pl.pallas_call(
        _s5_prompt_kernel,
        grid=(S5_NSG, batch),
        in_specs=[pl.BlockSpec((length, LANES), lambda s, b: (b, s)),
                  pl.BlockSpec((1, LANES), lambda s, b: (0, s)),
                  pl.BlockSpec((S5_T, 1, LANES, LANES), lambda s, b: (0, s, 0, 0)),
                  pl.BlockSpec((S5_T, 1, LANES, 2 * S5_SL), lambda s, b: (0, s, 0, 0)),
                  pl.BlockSpec((t1, 1, 2 * S5_SL, LANES), lambda s, b: (0, s, 0, 0)),
                  pl.BlockSpec((1, t1 + 7, 2 * S5_SL), lambda s, b: (s, 0, 0))],
        out_specs=[pl.BlockSpec((length, LANES), lambda s, b: (b, s)),
                   pl.BlockSpec((1, 1, S5_SL), lambda s, b: (b, 0, s)),
                   pl.BlockSpec((1, 1, S5_SL), lambda s, b: (b, 0, s))],
        out_shape=[jax.ShapeDtypeStruct((batch * length, D_MODEL), BF16),
                   jax.ShapeDtypeStruct((batch, 1, N_GROUPS * STATE), F32),
                   jax.ShapeDtypeStruct((batch, 1, N_GROUPS * STATE), F32)],
        scratch_shapes=[pltpu.VMEM((S5_T, r, LANES), BF16),
                        pltpu.VMEM((2 * S5_SL // LANES, r, LANES), F32),
                        pltpu.VMEM((8, 2 * S5_SL // LANES, r // 8, LANES), F32),
                        pltpu.VMEM((2 * S5_SL // LANES, r // 8, LANES), F32),
                        pltpu.VMEM((2 * S5_SL // LANES, r, LANES), F32),
                        pltpu.VMEM((length, LANES), F32)],
        compiler_params=_cp(("arbitrary", "arbitrary"), 56),
        name="s5_prompt",
    )(u, d_skip.reshape(1, D_MODEL), wk, ws, wv, apow)


def _resident(shape):
    nd = len(shape)
    return pl.BlockSpec(shape, lambda *_: (0,) * nd, pipeline_mode=pl.Buffered(1))


def _rows(tm, width):
    return pl.BlockSpec((tm, width), lambda i: (i, 0))


def _mm(a, b):
    return jnp.dot(a, b, preferred_element_type=F32)


def _glu_kernel(g_ref, x_ref, w_ref, gain_ref, y_ref, h_ref):
    p = _mm(g_ref[...], w_ref[...])
    y = x_ref[...] + p[:, :D_MODEL] * jax.nn.sigmoid(p[:, D_MODEL:])
    y_ref[...] = y
    h_ref[...] = _rms(y, gain_ref[...]).astype(h_ref.dtype)


def _glu(g, x, w_glu, gain, tm):
    n = g.shape[0]
    return pl.pallas_call(
        _glu_kernel,
        grid=(n // tm,),
        in_specs=[_rows(tm, D_MODEL), _rows(tm, D_MODEL), _resident((D_MODEL, 2 * D_MODEL)), _resident((1, D_MODEL))],
        out_specs=[_rows(tm, D_MODEL), _rows(tm, D_MODEL)],
        out_shape=[jax.ShapeDtypeStruct((n, D_MODEL), F32), jax.ShapeDtypeStruct((n, D_MODEL), g.dtype)],
        compiler_params=_cp(("parallel",), 48),
        name="s5_glu",
    )(g, x, w_glu, gain.reshape(1, D_MODEL))


def _chunks(total, size):
    return [(s, min(size, total - s)) for s in range(0, total, size)]


def _swiglu_acc(h, win_ref, wout_ref, ff, lead=()):
    acc = None
    for s, w in _chunks(ff, 256):
        a = jnp.dot(h, win_ref[lead + (slice(None), slice(s, s + w))], preferred_element_type=F32)
        b = jnp.dot(h, win_ref[lead + (slice(None), slice(ff + s, ff + s + w))], preferred_element_type=F32)
        act = (jax.nn.silu(a) * b).astype(BF16)
        z = jnp.dot(act, wout_ref[lead + (slice(s, s + w), slice(None))], preferred_element_type=F32)
        acc = z if acc is None else acc + z
    return acc


def _ffn_kernel(h_ref, y_ref, win_ref, wout_ref, gain_ref, yo_ref, ho_ref):
    y = y_ref[...] + _swiglu_acc(h_ref[...], win_ref, wout_ref, D_FF)
    yo_ref[...] = y
    ho_ref[...] = _rms(y, gain_ref[...]).astype(ho_ref.dtype)


def _ffn(h, y, w_in, w_out, gain, tm):
    n = h.shape[0]
    return pl.pallas_call(
        _ffn_kernel,
        grid=(n // tm,),
        in_specs=[_rows(tm, D_MODEL), _rows(tm, D_MODEL), _resident((D_MODEL, 2 * D_FF)), _resident((D_FF, D_MODEL)),
                  _resident((1, D_MODEL))],
        out_specs=[_rows(tm, D_MODEL), _rows(tm, D_MODEL)],
        out_shape=[jax.ShapeDtypeStruct((n, D_MODEL), F32), jax.ShapeDtypeStruct((n, D_MODEL), BF16)],
        compiler_params=_cp(("parallel",), 56),
        name="ffn_swiglu",
    )(h, y, w_in, w_out, gain.reshape(1, D_MODEL))


NSA_PAD = 2688
KV_OFF = Q_COLS


def _nsa_proj_kernel(h_ref, w_ref, q_ref, kvf_ref, winf_ref, kvb_ref, gate_ref):
    p = _mm(h_ref[...], w_ref[...])
    q_ref[...] = (p[:, :Q_COLS] * (HEAD_DIM ** -0.5)).astype(q_ref.dtype)
    kvf_ref[...] = p[:, KV_OFF:KV_OFF + 1024]
    winf_ref[...] = p[:, KV_OFF + 1024:KV_OFF + 1536]
    kvb_ref[...] = p[:, KV_OFF + 512:KV_OFF + 1536].astype(kvb_ref.dtype)
    gate_ref[...] = jax.nn.sigmoid(p[:, KV_OFF + 1536:])


def _nsa_proj(h, w_in_pad, tm):
    n = h.shape[0]
    return pl.pallas_call(
        _nsa_proj_kernel,
        grid=(n // tm,),
        in_specs=[_rows(tm, D_MODEL), _resident((D_MODEL, NSA_PAD))],
        out_specs=[_rows(tm, Q_COLS), _rows(tm, 1024), _rows(tm, 512), _rows(tm, 1024), _rows(tm, LANES)],
        out_shape=[jax.ShapeDtypeStruct((n, Q_COLS), h.dtype),
                   jax.ShapeDtypeStruct((n, 1024), F32),
                   jax.ShapeDtypeStruct((n, 512), F32),
                   jax.ShapeDtypeStruct((n, 1024), h.dtype),
                   jax.ShapeDtypeStruct((n, LANES), F32)],
        compiler_params=_cp(("parallel",), 48),
        name="nsa_proj",
    )(h, w_in_pad)


def _merge_kernel(oc_ref, os_ref, ow_ref, y_ref, wo_ref, gain_ref, router_ref, yo_ref, ho_ref, mg_ref, rt_ref,
                  cnt_ref):
    o = (oc_ref[...].astype(F32) + os_ref[...].astype(F32) + ow_ref[...].astype(F32)).astype(wo_ref.dtype)
    y = y_ref[...] + _mm(o, wo_ref[...])
    yo_ref[...] = y
    hf = _rms(y, gain_ref[...])
    ho_ref[...] = hf.astype(ho_ref.dtype)
    logits = _mm(hf.astype(BF16), router_ref[...])
    lane = lax.broadcasted_iota(jnp.int32, logits.shape, 1)
    logits = jnp.where(lane < N_EXPERTS, logits, -jnp.inf)
    m1 = jnp.max(logits, axis=-1, keepdims=True)
    i1 = jnp.min(jnp.where(logits == m1, lane, LANES), axis=-1, keepdims=True)
    rest = jnp.where(lane == i1, -jnp.inf, logits)
    m2 = jnp.max(rest, axis=-1, keepdims=True)
    i2 = jnp.min(jnp.where(rest == m2, lane, LANES), axis=-1, keepdims=True)
    e2 = jnp.exp(m2 - m1)
    den = 1.0 + e2
    mg_ref[...] = jnp.where(lane == i1, 1.0 / den, 0.0) + jnp.where(lane == i2, e2 / den, 0.0)
    routed = jnp.where((lane == i1) | (lane == i2), 1.0, 0.0)
    rt_ref[...] = routed
    cnt_ref[0] = jnp.broadcast_to(jnp.sum(routed, axis=0, keepdims=True), cnt_ref.shape[1:])


def _merge(oc, os_, ow, y, w_o, gain, router_pad, tm):
    n = y.shape[0]
    return pl.pallas_call(
        _merge_kernel,
        grid=(n // tm,),
        in_specs=[_rows(tm, D_MODEL)] * 4 + [_resident((D_MODEL, D_MODEL)), _resident((1, D_MODEL)),
                                             _resident((D_MODEL, LANES))],
        out_specs=[_rows(tm, D_MODEL), _rows(tm, D_MODEL), _rows(tm, LANES), _rows(tm, LANES),
                   pl.BlockSpec((1, 8, LANES), lambda i: (i, 0, 0))],
        out_shape=[jax.ShapeDtypeStruct((n, D_MODEL), F32), jax.ShapeDtypeStruct((n, D_MODEL), BF16),
                   jax.ShapeDtypeStruct((n, LANES), F32), jax.ShapeDtypeStruct((n, LANES), F32),
                   jax.ShapeDtypeStruct((n // tm, 8, LANES), F32)],
        compiler_params=_cp(("parallel",), 56),
        name="nsa_merge_router",
    )(oc, os_, ow, y, w_o, gain.reshape(1, D_MODEL), router_pad)


def _moe_kernel(h_ref, y_ref, mg_ref, win_ref, wout_ref, gain_ref, o_ref, acc_ref):
    e = pl.program_id(1)

    @pl.when(e == 0)
    def _():
        acc_ref[...] = jnp.zeros_like(acc_ref)

    mg = mg_ref[...]
    lane = lax.broadcasted_iota(jnp.int32, mg.shape, 1)
    gcol = jnp.sum(jnp.where(lane == e, mg, 0.0), axis=-1, keepdims=True)
    acc_ref[...] += gcol * _swiglu_acc(h_ref[...], win_ref, wout_ref, EXPERT_FF, lead=(0,))

    @pl.when(e == N_EXPERTS - 1)
    def _():
        o_ref[...] = _rms(y_ref[...] + acc_ref[...], gain_ref[...])


def _moe(h, y, mg, w_in, w_out, gain, tm):
    n = h.shape[0]
    rows = lambda w: pl.BlockSpec((tm, w), lambda i, e: (i, 0))
    return pl.pallas_call(
        _moe_kernel,
        grid=(n // tm, N_EXPERTS),
        in_specs=[rows(D_MODEL), rows(D_MODEL), rows(LANES),
                  pl.BlockSpec((1, D_MODEL, 2 * EXPERT_FF), lambda i, e: (e, 0, 0)),
                  pl.BlockSpec((1, EXPERT_FF, D_MODEL), lambda i, e: (e, 0, 0)),
                  pl.BlockSpec((1, D_MODEL), lambda i, e: (0, 0))],
        out_specs=rows(D_MODEL),
        out_shape=jax.ShapeDtypeStruct((n, D_MODEL), F32),
        scratch_shapes=[pltpu.VMEM((tm, D_MODEL), F32)],
        compiler_params=_cp(("parallel", "arbitrary"), 56),
        name="moe_final_norm",
    )(h, y, mg, w_in, w_out, gain.reshape(1, D_MODEL))


MOE_TM = 1024
MOE_RB = 256


def _moe_routed_kernel(cnt_ref, h_ref, y_ref, mg_ref, rt_ref, lower_ref, win_ref, wout_ref, gain_ref, o_ref,
                       acc_ref, rcol_ref, rrow_ref, rtrow_ref):
    t = pl.program_id(0)
    e = pl.program_id(1)
    tm = h_ref.shape[0]

    @pl.when(e == 0)
    def _():
        acc_ref[...] = jnp.zeros_like(acc_ref)
        rt = rt_ref[...]
        rank = _mm(lower_ref[...], rt.astype(BF16))
        rcol_ref[...] = rank
        rrow_ref[...] = rank.T
        rtrow_ref[...] = rt.T

    n = cnt_ref[t * N_EXPERTS + e]
    lane = lax.broadcasted_iota(jnp.int32, (tm, LANES), 1)
    col = lambda ref: jnp.sum(jnp.where(lane == e, ref[...], 0.0), axis=-1, keepdims=True)
    gcol, rank_col, rt_col = col(mg_ref), col(rcol_ref), col(rt_ref)
    rank_row = rrow_ref[pl.ds(e, 1), :]
    rt_row = rtrow_ref[pl.ds(e, 1), :]
    for sb in range(tm // MOE_RB):
        @pl.when(sb * MOE_RB < n)
        def _():
            ridx = (sb * MOE_RB + lax.broadcasted_iota(jnp.int32, (MOE_RB, tm), 0)).astype(F32)
            take = jnp.where((rank_row == ridx) & (rt_row > 0.5), 1.0, 0.0).astype(BF16)
            x = _mm(take, h_ref[...]).astype(BF16)
            y = _swiglu_acc(x, win_ref, wout_ref, EXPERT_FF, lead=(0,))
            cidx = (sb * MOE_RB + lax.broadcasted_iota(jnp.int32, (tm, MOE_RB), 1)).astype(F32)
            put = jnp.where((rank_col == cidx) & (rt_col > 0.5), 1.0, 0.0).astype(BF16)
            y_hi = y.astype(BF16)
            y_lo = (y - y_hi.astype(F32)).astype(BF16)
            acc_ref[...] += gcol * (_mm(put, y_hi) + _mm(put, y_lo))

    @pl.when(e == N_EXPERTS - 1)
    def _():
        o_ref[...] = _rms(y_ref[...] + acc_ref[...], gain_ref[...])


def _moe_routed(h, y, mg, rt, cnt, w_in, w_out, gain):
    n = h.shape[0]
    tm = MOE_TM
    lower = jnp.asarray(np.tril(np.ones((tm, tm), np.float32), -1), BF16)
    rows = lambda w, **kw: pl.BlockSpec((tm, w), lambda i, e, c: (i, 0), **kw)
    grid_spec = pltpu.PrefetchScalarGridSpec(
        num_scalar_prefetch=1,
        grid=(n // tm, N_EXPERTS),
        in_specs=[rows(D_MODEL), rows(D_MODEL, pipeline_mode=pl.Buffered(1)), rows(LANES), rows(LANES),
                  pl.BlockSpec((tm, tm), lambda i, e, c: (0, 0), pipeline_mode=pl.Buffered(1)),
                  pl.BlockSpec((1, D_MODEL, 2 * EXPERT_FF), lambda i, e, c: (e, 0, 0)),
                  pl.BlockSpec((1, EXPERT_FF, D_MODEL), lambda i, e, c: (e, 0, 0)),
                  pl.BlockSpec((1, D_MODEL), lambda i, e, c: (0, 0))],
        out_specs=rows(D_MODEL),
        scratch_shapes=[pltpu.VMEM((tm, D_MODEL), F32),
                        pltpu.VMEM((tm, LANES), F32),
                        pltpu.VMEM((LANES, tm), F32),
                        pltpu.VMEM((LANES, tm), F32)])
    return pl.pallas_call(
        _moe_routed_kernel,
        grid_spec=grid_spec,
        out_shape=jax.ShapeDtypeStruct((n, D_MODEL), F32),
        compiler_params=_cp(("arbitrary", "arbitrary"), 58),
        name="moe_routed_final_norm",
    )(cnt, h, y, mg, rt, lower, w_in, w_out, gain.reshape(1, D_MODEL))


def _t5_bucket_np(dist):
    n = np.maximum(np.asarray(dist, np.int64), 0)
    exact = N_BUCKETS // 2
    logpart = exact + np.floor(np.log(np.maximum(n, 1) / exact) / math.log(MAX_DISTANCE / exact)
                               * (N_BUCKETS - exact)).astype(np.int64)
    return np.where(n < exact, n, np.minimum(logpart, N_BUCKETS - 1)).astype(np.int32)


def _bias_kernel(rb_ref, idx_ref, o_ref):
    h = pl.program_id(0)
    idx = idx_ref[...]
    far = rb_ref[N_BUCKETS - 1, h]
    acc = jnp.zeros(idx.shape, F32)
    for b in range(N_BUCKETS - 1):
        acc = jnp.where(idx == b, rb_ref[b, h] - far, acc)
    o_ref[0] = acc


def _bias_tables(rel_bias, idx):
    m = idx.shape[0]
    return pl.pallas_call(
        _bias_kernel,
        grid=(N_HEADS,),
        in_specs=[pl.BlockSpec(memory_space=pltpu.SMEM), pl.BlockSpec((m, LANES), lambda h: (0, 0))],
        out_specs=pl.BlockSpec((1, m, LANES), lambda h: (h, 0, 0)),
        out_shape=jax.ShapeDtypeStruct((N_HEADS, m, LANES), F32),
        compiler_params=_cp(("arbitrary",), 32),
        name="t5_bias_tables",
    )(rel_bias, idx)


CMP_NEAR = 32


def _prompt_bias_idx():
    a = np.arange(QT)[:, None]
    b = np.arange(QT)[None, :]
    t0 = _t5_bucket_np(a - b)
    t1 = _t5_bucket_np(QT + a - b)
    m = np.arange(LANES)[None, :]
    gc = np.where(m < CMP_NEAR, _t5_bucket_np(a + (CMP_NEAR // 2) * CMP_STRIDE - (CMP_BLOCK - 1) - CMP_STRIDE * m),
                  N_BUCKETS - 1)
    return np.concatenate([t0, t1, gc], axis=0).astype(np.int32)


def _flatten_pairs(load, nh):
    lane = lax.broadcasted_iota(jnp.int32, (nh, LANES), 1)
    ev, od = [], []
    for m in range(CMP_STRIDE // 2):
        sa = load(2 * m)
        sb = load(2 * m + 1)
        ev.append(jnp.where(lane < HEAD_DIM, sa, pltpu.roll(sb, HEAD_DIM, 1)))
        od.append(jnp.where(lane < HEAD_DIM, pltpu.roll(sa, HEAD_DIM, 1), sb))
    return jnp.concatenate(ev, axis=1), jnp.concatenate(od, axis=1)


def _compress_rows(xf, w1_ref, pe_ref, w2_ref, slot):
    rows = xf.shape[0]
    half = CMP_STRIDE * HEAD_DIM
    first = _mm((xf + pe_ref[slot, 0:1, :half]).astype(BF16), w1_ref[slot, :, :CMP_HIDDEN])
    second = _mm((xf + pe_ref[slot, 0:1, half:]).astype(BF16), w1_ref[slot, :, CMP_HIDDEN:])
    act = jax.nn.gelu(first + pltpu.roll(second, rows - 1, 0))
    return _mm(act.astype(BF16), w2_ref[slot])


def _compress_prompt_kernel(x0, x1, x2, x3, w1_ref, pe_ref, w2_ref, o_ref):
    nh = x0.shape[0] // CMP_STRIDE
    xs = (x0, x1, x2, x3)
    for slot in range(2):
        parts = []
        for c in range(2):
            x_ref = xs[2 * slot + c]
            ev, od = _flatten_pairs(lambda s: x_ref[pl.ds(s, nh, stride=CMP_STRIDE), :], nh)
            parts += [ev, od]
        xf = jnp.concatenate(parts, axis=0)
        o_ref[0, slot] = _compress_rows(xf, w1_ref, pe_ref, w2_ref, slot).astype(o_ref.dtype)


def _compress_weights(phi_pe, phi_w1, phi_w2):
    half = CMP_BLOCK // 2
    w1 = jnp.concatenate([phi_w1[:, :half].reshape(2, half * HEAD_DIM, CMP_HIDDEN),
                          phi_w1[:, half:].reshape(2, half * HEAD_DIM, CMP_HIDDEN)], axis=2)
    pe = jnp.broadcast_to(phi_pe.reshape(2, 1, CMP_BLOCK * HEAD_DIM), (2, 8, CMP_BLOCK * HEAD_DIM))
    return w1.astype(BF16), pe, phi_w2.astype(BF16)


def _compress_prompt(kvf, cw, batch, length):
    w1b, pe, w2b = cw
    nh = length // CMP_STRIDE
    xspec = lambda c: pl.BlockSpec((length, LANES), lambda b: (b, c))
    return pl.pallas_call(
        _compress_prompt_kernel,
        grid=(batch,),
        in_specs=[xspec(0), xspec(1), xspec(2), xspec(3), _resident(w1b.shape), _resident(pe.shape),
                  _resident(w2b.shape)],
        out_specs=pl.BlockSpec((1, 2, N_KV * nh, HEAD_DIM), lambda b: (b, 0, 0, 0)),
        out_shape=jax.ShapeDtypeStruct((batch, 2, N_KV * nh, HEAD_DIM), BF16),
        compiler_params=_cp(("parallel",), 48),
        name="nsa_compress_prompt",
    )(kvf, kvf, kvf, kvf, w1b, pe, w2b)


_NT = (((1,), (1,)), ((), ()))


def _cover_t_np(nc_pad, nb_pad):
    ratio = SEL_BLOCK // CMP_STRIDE
    span = CMP_BLOCK // CMP_STRIDE
    off = (np.arange(ratio)[:, None] - np.arange(span)[None, :]).reshape(-1)
    target = np.arange(nb_pad)[:, None] * ratio + off[None, :]
    cov = np.sum(np.arange(nc_pad)[:, None, None] == target[None], axis=-1)
    return np.ascontiguousarray(cov.T).astype(np.float32)


def _rank_select(score, jidx, nsel):
    nb = score.shape[0]
    cnt = jnp.zeros(score.shape, F32)
    for i in range(nb):
        ri = score[i:i + 1, :]
        cnt = cnt + jnp.where((ri > score) | ((ri == score) & (jidx > i)), 1.0, 0.0)
    return cnt < nsel


def _cmp_attn_kernel(q_ref, kcv_ref, gate_ref, ghi_ref, glo_ref, covt_ref, o_ref, sel_ref, bias_ref, selt_ref):
    i = pl.program_id(1)
    q0 = i * QT
    nc = kcv_ref.shape[2] // N_KV
    nb = covt_ref.shape[0]
    base = q0 // CMP_STRIDE - CMP_NEAR // 2
    rr = lax.broadcasted_iota(jnp.int32, (LANES, nc), 0)
    cc = lax.broadcasted_iota(jnp.int32, (LANES, nc), 1)
    place = ((cc == base + rr) & (rr < CMP_NEAR)).astype(BF16)
    bias_ref[...] = (jnp.dot(ghi_ref[...], place, preferred_element_type=F32)
                     + jnp.dot(glo_ref[...], place, preferred_element_type=F32))
    t = q0 + lax.broadcasted_iota(jnp.int32, (QT, nc), 0)
    cend = lax.broadcasted_iota(jnp.int32, (QT, nc), 1) * CMP_STRIDE + (CMP_BLOCK - 1)
    vis = t >= cend
    visf = vis.astype(F32)
    madd = jnp.where(vis, 0.0, NEG)
    gate = gate_ref[...]
    jidx = lax.broadcasted_iota(jnp.int32, (nb, QT), 0)
    tq = q0 + lax.broadcasted_iota(jnp.int32, (nb, QT), 1)
    jt = tq // SEL_BLOCK
    forced = (jidx == 0) | (jidx == jt) | (jidx == jt - 1)
    visb = jidx * SEL_BLOCK <= tq
    for g in range(N_KV):
        kc = kcv_ref[0, 0, g * nc:(g + 1) * nc, :]
        vc = kcv_ref[0, 1, g * nc:(g + 1) * nc, :]
        psum = jnp.zeros((QT, nc), F32)
        for r in range(GQ):
            h = g * GQ + r
            qh = q_ref[:, h * HEAD_DIM:(h + 1) * HEAD_DIM]
            s = lax.dot_general(qh, kc, _NT, preferred_element_type=F32)
            s = s + bias_ref[h * QT:(h + 1) * QT, :] + madd
            m = jnp.max(s, axis=-1, keepdims=True)
            e = jnp.exp(s - m)
            p = ((e * visf) * (1.0 / jnp.sum(e, axis=-1, keepdims=True))).astype(BF16)
            psum = psum + p.astype(F32)
            o = _mm(p, vc)
            o_ref[:, h * HEAD_DIM:(h + 1) * HEAD_DIM] = (o * gate[:, h:h + 1]).astype(o_ref.dtype)
        imp = lax.dot_general(covt_ref[...], psum, _NT, precision=HIGHEST, preferred_element_type=F32)
        score = jnp.where(visb, jnp.where(forced, FORCE, imp), NEG)
        sel = _rank_select(score, jidx, min(TOP_N, nb)) & visb
        selt_ref[g * nb:(g + 1) * nb, :] = sel.astype(F32)
    sel_ref[...] = selt_ref[...].T


def _cmp_attn(q, kcv, gate, ghi, glo, batch, length):
    nq = length // QT
    nc = kcv.shape[2] // N_KV
    nb = length // SEL_BLOCK
    covt = jnp.asarray(_cover_t_np(nc, nb))
    row = lambda w: pl.BlockSpec((QT, w), lambda b, i: (b * nq + i, 0))
    return pl.pallas_call(
        _cmp_attn_kernel,
        grid=(batch, nq),
        in_specs=[row(Q_COLS), pl.BlockSpec((1, 2, N_KV * nc, HEAD_DIM), lambda b, i: (b, 0, 0, 0)), row(LANES),
                  _resident(ghi.shape), _resident(glo.shape), _resident(covt.shape)],
        out_specs=[row(Q_COLS), row(N_KV * nb)],
        out_shape=[jax.ShapeDtypeStruct((batch * length, Q_COLS), BF16),
                   jax.ShapeDtypeStruct((batch * length, N_KV * nb), F32)],
        scratch_shapes=[pltpu.VMEM((N_HEADS * QT, nc), F32), pltpu.VMEM((N_KV * nb, QT), F32)],
        compiler_params=_cp(("parallel", "arbitrary"), 48),
        name="nsa_cmp_attn_select",
    )(q, kcv, gate, ghi, glo, covt)


SEL_CHUNK = 4
SEL_TAIL = SEL_CHUNK + 1
SEL_ZERO, SEL_MASKED = 2, 3


def _selected_kernel(q_ref, kv_ref, gate_ref, stab_ref, sel_ref, ex_ref, o_ref, *scratch):
    i = pl.program_id(1)
    length = kv_ref.shape[0]
    nb = ex_ref.shape[0]
    wide = SEL_CHUNK * QT
    tail = SEL_TAIL * QT
    nbulk = jnp.maximum(i - 1, 0) // SEL_CHUNK
    tstart = jnp.minimum(SEL_CHUNK * nbulk, length // QT - SEL_TAIL)
    trows = pl.ds(pl.multiple_of(tstart * QT, QT), tail)
    rep = lambda a: jnp.concatenate([a] * GQ, axis=0)
    gate = gate_ref[...]
    kinds = []
    for jj in range(SEL_TAIL):
        tile = tstart + jj
        kinds.append(jnp.where((tile < SEL_CHUNK * nbulk) | (tile > i), SEL_MASKED, jnp.minimum(i - tile, SEL_ZERO)))

    kcol = lambda g: slice(g * HEAD_DIM, (g + 1) * HEAD_DIM)
    vcol = lambda g: slice((N_KV + g) * HEAD_DIM, (N_KV + g + 1) * HEAD_DIM)
    q4 = lambda g: jnp.concatenate(
        [q_ref[:, (g * GQ + r) * HEAD_DIM:(g * GQ + r + 1) * HEAD_DIM] for r in range(GQ)], axis=0)

    per_set = len(scratch) // 2
    sets = (scratch[:per_set], scratch[per_set:])

    def prep(g):
        s_ref, mx_ref, mb_ref, ls_ref, acc_ref, madd_ref = sets[g % 2]
        sel = sel_ref[:, g * nb:(g + 1) * nb].astype(BF16)
        madd_ref[...] = (jnp.dot(sel, ex_ref[...], preferred_element_type=F32) - 1.0) * (-NEG)
        mx_ref[...] = jnp.full(mx_ref.shape, -jnp.inf, F32)
        ls_ref[...] = jnp.zeros(ls_ref.shape, F32)
        acc_ref[...] = jnp.zeros(acc_ref.shape, F32)

    def logits_chunk(g, c):
        s_ref, mx_ref, mb_ref, ls_ref, acc_ref, madd_ref = sets[g % 2]
        cols = pl.ds(pl.multiple_of(c * wide, wide), wide)
        s = (lax.dot_general(q4(g), kv_ref[cols, kcol(g)], _NT, preferred_element_type=F32)
             + rep(madd_ref[:, cols]))
        s_ref[:, cols] = s
        mx = mx_ref[...]
        for jj in range(SEL_CHUNK):
            mx = jnp.maximum(mx, s[:, jj * QT:(jj + 1) * QT])
        mx_ref[...] = mx

    def logits_tail(g):
        s_ref, mx_ref, mb_ref, ls_ref, acc_ref, madd_ref = sets[g % 2]
        s = lax.dot_general(q4(g), kv_ref[trows, kcol(g)], _NT, preferred_element_type=F32)
        adds = [jnp.concatenate([stab_ref[g * GQ + r, :, pl.ds(pl.multiple_of(kinds[jj] * QT, QT), QT)]
                                 for r in range(GQ)], axis=0) for jj in range(SEL_TAIL)]
        s = s + jnp.concatenate(adds, axis=1) + rep(madd_ref[:, trows])
        s_ref[:, length:] = s
        mx = mx_ref[...]
        for jj in range(SEL_TAIL):
            mx = jnp.maximum(mx, s[:, jj * QT:(jj + 1) * QT])
        mb_ref[...] = jnp.broadcast_to(jnp.max(mx, axis=-1, keepdims=True), mx.shape)

    def pv_chunk(g, c):
        s_ref, mx_ref, mb_ref, ls_ref, acc_ref, madd_ref = sets[g % 2]
        cols = pl.ds(pl.multiple_of(c * wide, wide), wide)
        m = mb_ref[...]
        ps = [jnp.exp(s_ref[:, pl.ds(pl.multiple_of(c * wide + jj * QT, QT), QT)] - m) for jj in range(SEL_CHUNK)]
        lsum = ls_ref[...]
        for p in ps:
            lsum = lsum + p
        ls_ref[...] = lsum
        acc_ref[...] += jnp.dot(jnp.concatenate(ps, axis=1).astype(BF16), kv_ref[cols, vcol(g)],
                                preferred_element_type=F32)

    def pv_tail(g):
        s_ref, mx_ref, mb_ref, ls_ref, acc_ref, madd_ref = sets[g % 2]
        m = mb_ref[...]
        ps = [jnp.exp(s_ref[:, length + jj * QT:length + (jj + 1) * QT] - m) for jj in range(SEL_TAIL)]
        lsum = ls_ref[...]
        for p in ps:
            lsum = lsum + p
        acc = acc_ref[...] + jnp.dot(jnp.concatenate(ps, axis=1).astype(BF16), kv_ref[trows, vcol(g)],
                                     preferred_element_type=F32)
        l = jnp.sum(lsum, axis=-1, keepdims=True)
        for r in range(GQ):
            h = g * GQ + r
            scale = gate[:, N_HEADS + h:N_HEADS + h + 1] / l[r * QT:(r + 1) * QT]
            o_ref[:, h * HEAD_DIM:(h + 1) * HEAD_DIM] = (acc[r * QT:(r + 1) * QT, :] * scale).astype(o_ref.dtype)

    def loop(*stages):
        def body(c, carry):
            for stage, g in stages:
                stage(g, c)
            return carry
        lax.fori_loop(0, nbulk, body, 0)

    prep(0)
    loop((logits_chunk, 0))
    logits_tail(0)
    for g in range(1, N_KV):
        prep(g)
        loop((logits_chunk, g), (pv_chunk, g - 1))
        pv_tail(g - 1)
        logits_tail(g)
    loop((pv_chunk, N_KV - 1))
    pv_tail(N_KV - 1)


WIN_TILES = WINDOW // QT + 1


def _window_kernel(q_ref, kv_ref, gate_ref, wtab_ref, o_ref):
    i = pl.program_id(1)
    first = jnp.maximum(i - (WIN_TILES - 1), 0)
    rows = pl.ds(pl.multiple_of(first * QT, QT), WIN_TILES * QT)
    gate = gate_ref[...]
    for g in range(N_KV):
        kcol = slice(g * HEAD_DIM, (g + 1) * HEAD_DIM)
        vcol = slice(N_KV * HEAD_DIM + g * HEAD_DIM, N_KV * HEAD_DIM + (g + 1) * HEAD_DIM)
        q4 = jnp.concatenate([q_ref[:, (g * GQ + r) * HEAD_DIM:(g * GQ + r + 1) * HEAD_DIM] for r in range(GQ)],
                             axis=0)
        s = lax.dot_general(q4, kv_ref[rows, kcol], _NT, preferred_element_type=F32)
        adds = []
        for jj in range(WIN_TILES):
            d = i - first - jj
            kind = jnp.where(d < 0, WIN_TILES, d)
            col = pl.ds(pl.multiple_of(kind * QT, QT), QT)
            adds.append(jnp.concatenate([wtab_ref[g * GQ + r, :, col] for r in range(GQ)], axis=0))
        s = s + jnp.concatenate(adds, axis=1)
        p = jnp.exp(s - jnp.max(s, axis=-1, keepdims=True))
        l = jnp.sum(p, axis=-1, keepdims=True)
        acc = jnp.dot(p.astype(BF16), kv_ref[rows, vcol], preferred_element_type=F32)
        for r in range(GQ):
            h = g * GQ + r
            scale = gate[:, 2 * N_HEADS + h:2 * N_HEADS + h + 1] / l[r * QT:(r + 1) * QT]
            o_ref[:, h * HEAD_DIM:(h + 1) * HEAD_DIM] = (acc[r * QT:(r + 1) * QT, :] * scale).astype(o_ref.dtype)


def _window_tables(tab):
    a = np.arange(QT)[:, None]
    b = np.arange(QT)[None, :]
    causal = np.where(a >= b, 0.0, NEG).astype(np.float32)
    edge = np.where(a <= b, 0.0, NEG).astype(np.float32)
    const = lambda m: jnp.broadcast_to(jnp.asarray(m, F32), (N_HEADS, QT, QT))
    zero = np.zeros((QT, QT), np.float32)
    return jnp.concatenate([tab[:, :QT] + causal, tab[:, QT:], const(zero), const(zero), const(edge),
                            const(np.full((QT, QT), NEG, np.float32))], axis=2)


def _window_attn(q, kvb, gate, wtab, batch, length):
    nq = length // QT
    row = lambda w: pl.BlockSpec((QT, w), lambda b, i: (b * nq + i, 0))
    return pl.pallas_call(
        _window_kernel,
        grid=(batch, nq),
        in_specs=[row(Q_COLS), pl.BlockSpec((length, 2 * N_KV * HEAD_DIM), lambda b, i: (b, 1)), row(LANES),
                  _resident(wtab.shape)],
        out_specs=row(Q_COLS),
        out_shape=jax.ShapeDtypeStruct((batch * length, Q_COLS), BF16),
        compiler_params=_cp(("parallel", "arbitrary"), 48),
        name="nsa_window_attn",
    )(q, kvb, gate, wtab)


def _ex_np(nb, length):
    return (np.arange(nb)[:, None] == (np.arange(length)[None, :] // SEL_BLOCK)).astype(np.float32)


def _selected_tables(tab):
    a = np.arange(QT)[:, None]
    b = np.arange(QT)[None, :]
    causal = np.where(a >= b, 0.0, NEG).astype(np.float32)
    const = lambda v: jnp.full((N_HEADS, QT, QT), v, F32)
    return jnp.concatenate([tab[:, :QT] + causal, tab[:, QT:], const(0.0), const(NEG)], axis=2)


def _selected_attn(q, kvb, gate, stab, selmask, batch, length):
    nq = length // QT
    nb = length // SEL_BLOCK
    assert nq >= SEL_TAIL
    row = lambda w: pl.BlockSpec((QT, w), lambda b, i: (b * nq + i, 0))
    ex = jnp.asarray(_ex_np(nb, length), BF16)
    rows4 = GQ * QT
    return pl.pallas_call(
        _selected_kernel,
        grid=(batch, nq),
        in_specs=[row(Q_COLS), pl.BlockSpec((length, 2 * N_KV * HEAD_DIM), lambda b, i: (b, 0)), row(LANES),
                  _resident(stab.shape), row(N_KV * nb), _resident(ex.shape)],
        out_specs=row(Q_COLS),
        out_shape=jax.ShapeDtypeStruct((batch * length, Q_COLS), BF16),
        scratch_shapes=2 * [pltpu.VMEM((rows4, length + SEL_TAIL * QT), F32),
                            pltpu.VMEM((rows4, QT), F32),
                            pltpu.VMEM((rows4, QT), F32),
                            pltpu.VMEM((rows4, QT), F32),
                            pltpu.VMEM((rows4, HEAD_DIM), F32),
                            pltpu.VMEM((QT, length), F32)],
        compiler_params=_cp(("parallel", "arbitrary"), 56),
        name="nsa_selected_attn",
    )(q, kvb, gate, stab, selmask, ex)


def _s5_sample_kernel(u_ref, hr_ref, hi_ref, ws_ref, wv_ref, a1_ref, d_ref, g_ref, sre_ref, sim_ref):
    u = u_ref[...]
    x = _mm(u.astype(BF16), ws_ref[0, 0])
    ar, ai = a1_ref[0, :, :S5_SL], a1_ref[0, :, S5_SL:]
    h0r, h0i = hr_ref[...], hi_ref[...]
    hr = x[:, :S5_SL] + (ar * h0r - ai * h0i)
    hi = x[:, S5_SL:] + (ar * h0i + ai * h0r)
    sre_ref[...] = hr
    sim_ref[...] = hi
    y = _mm(jnp.concatenate([hr, hi], axis=1).astype(BF16), wv_ref[0, 0])
    g_ref[...] = jax.nn.gelu(y + d_ref[...] * u).astype(g_ref.dtype)


def _s5_sample(u, h0_re, h0_im, ws, wv, a1, d_skip):
    n = u.shape[0]
    st = pl.BlockSpec((n, S5_SL), lambda s: (0, s))
    return pl.pallas_call(
        _s5_sample_kernel,
        grid=(S5_NSG,),
        in_specs=[pl.BlockSpec((n, LANES), lambda s: (0, s)), st, st,
                  pl.BlockSpec((1, 1, LANES, 2 * S5_SL), lambda s: (S5_T - 1, s, 0, 0)),
                  pl.BlockSpec((1, 1, 2 * S5_SL, LANES), lambda s: (0, s, 0, 0)),
                  pl.BlockSpec((1, 1, 2 * S5_SL), lambda s: (s, 0, 0)),
                  pl.BlockSpec((1, LANES), lambda s: (0, s))],
        out_specs=[pl.BlockSpec((n, LANES), lambda s: (0, s)), st, st],
        out_shape=[jax.ShapeDtypeStruct((n, D_MODEL), BF16),
                   jax.ShapeDtypeStruct((n, N_GROUPS * STATE), F32),
                   jax.ShapeDtypeStruct((n, N_GROUPS * STATE), F32)],
        compiler_params=_cp(("parallel",), 32),
        name="s5_sample",
    )(u, h0_re, h0_im, ws, wv, a1, d_skip.reshape(1, D_MODEL))


def _softmax_lanes(s, valid):
    m = jnp.max(s, axis=-1, keepdims=True)
    e = jnp.exp(s - m)
    return (e * valid) * (1.0 / jnp.sum(e, axis=-1, keepdims=True))


def _own_group(o):
    rowg = lax.broadcasted_iota(jnp.int32, (N_HEADS, HEAD_DIM), 0) // GQ
    out = jnp.zeros((N_HEADS, HEAD_DIM), F32)
    for g in range(N_KV):
        out = out + jnp.where(rowg == g, o[:, g * HEAD_DIM:(g + 1) * HEAD_DIM], 0.0)
    return out


DECODE_PAGES_PER_STEP = 4


def _nsa_decode_kernel(pt_ref, *refs, past_len, n_cmp):
    x_refs = refs[:DECODE_PAGES_PER_STEP]
    (win_ref, qbd_ref, new_ref, gate_ref, w1_ref, pe_ref, w2_ref, cov_ref, bias_ref, ex_ref, o_ref,
     xt_ref, xf_ref, ks_ref, vs_ref, kw_ref, vw_ref) = refs[DECODE_PAGES_PER_STEP:]
    step = pl.program_id(1)
    nsteps = pl.num_programs(1)
    page = x_refs[0].shape[2]
    hp = page // CMP_STRIDE
    nh = past_len // CMP_STRIDE
    nkv = N_KV * HEAD_DIM
    sel_cols = ks_ref.shape[1]
    win_cols = kw_ref.shape[1]
    wb = win_ref.shape[2]

    def stash(ref, cols, x):
        ref[:, cols] = x.astype(BF16)

    for k, x_ref in enumerate(x_refs):
        p = step * len(x_refs) + k
        xt = x_ref[0, :2 * nkv, :].T
        for c in range(2 * nkv // LANES):
            xt_ref[k, c] = xt[:, c * LANES:(c + 1) * LANES]
        for slot in range(2):
            for c in range(2):
                blk = 2 * slot + c
                ev, od = _flatten_pairs(lambda s: xt_ref[k, blk, pl.ds(s, hp, stride=CMP_STRIDE), :], hp)
                for head, val in ((2 * c, ev), (2 * c + 1, od)):
                    xf_ref[slot, pl.ds(pl.multiple_of(head * nh + p * hp, hp), hp), :] = val
        cols = pl.ds(pl.multiple_of(p * page, page), page)
        stash(ks_ref, cols, x_ref[0, 2 * nkv:3 * nkv, :])
        stash(vs_ref, cols, x_ref[0, 3 * nkv:, :])

    @pl.when(step == nsteps - 1)
    def _():
        qbd = qbd_ref[0]
        new = new_ref[0]
        lane0 = lax.broadcasted_iota(jnp.int32, (nkv, LANES), 1) == 0
        newcol = lambda k: jnp.where(lane0, jnp.broadcast_to(new[:, k:k + 1], (nkv, LANES)), 0.0)
        stash(ks_ref, slice(past_len, sel_cols), newcol(0))
        stash(vs_ref, slice(past_len, sel_cols), newcol(1))
        stash(kw_ref, slice(0, wb), win_ref[0, :nkv, :])
        stash(vw_ref, slice(0, wb), win_ref[0, nkv:, :])
        stash(kw_ref, slice(wb, win_cols), newcol(2))
        stash(vw_ref, slice(wb, win_cols), newcol(3))
        rowh = lax.broadcasted_iota(jnp.int32, (N_HEADS, LANES), 0)

        kc = _compress_rows(xf_ref[0], w1_ref, pe_ref, w2_ref, 0).astype(BF16)
        vc = _compress_rows(xf_ref[1], w1_ref, pe_ref, w2_ref, 1).astype(BF16)
        s = jnp.zeros((N_HEADS, nh), F32)
        for g in range(N_KV):
            s = s + lax.dot_general(qbd[:, g * HEAD_DIM:(g + 1) * HEAD_DIM], kc[g * nh:(g + 1) * nh], _NT,
                                    preferred_element_type=F32)
        nidx = lax.broadcasted_iota(jnp.int32, (N_HEADS, nh), 1)
        s = s + bias_ref[:, :nh] + jnp.where(nidx < n_cmp, 0.0, NEG)
        pc = _softmax_lanes(s, (nidx < n_cmp).astype(F32)).astype(BF16)
        o_c = _own_group(jnp.concatenate([_mm(pc, vc[g * nh:(g + 1) * nh]) for g in range(N_KV)], axis=1))

        pcf = pc.astype(F32)
        psum = jnp.concatenate([jnp.sum(pcf[g * GQ:(g + 1) * GQ], axis=0, keepdims=True) for g in range(N_KV)]
                               + [jnp.zeros((8 - N_KV, nh), F32)], axis=0)
        imp = jnp.dot(psum, cov_ref[...], precision=HIGHEST, preferred_element_type=F32)
        jt = past_len // SEL_BLOCK
        nb = jt + 1
        lane = lax.broadcasted_iota(jnp.int32, (LANES, LANES), 1)
        subl = lax.broadcasted_iota(jnp.int32, (LANES, LANES), 0)

        def scored(x, j):
            forced = (j == 0) | (j == jt) | (j == jt - 1)
            return jnp.where(j < nb, jnp.where(forced, FORCE, x), NEG)

        imp_sq = jnp.concatenate([imp, jnp.zeros((LANES - 8, LANES), F32)], axis=0)
        sc_row = scored(imp_sq, lane)
        sc_col = scored(imp_sq.T, subl)
        selh = jnp.zeros((N_HEADS, LANES), F32)
        for g in range(N_KV):
            col = sc_col[:, g:g + 1]
            row = sc_row[g:g + 1, :]
            ahead = (col > row) | ((col == row) & (subl < lane))
            rank = jnp.sum(jnp.where(ahead, 1.0, 0.0), axis=0, keepdims=True)
            chosen = jnp.where((rank < min(TOP_N, nb)) & (lane[:1] < nb), 1.0, 0.0)
            selh = selh + jnp.where(rowh // GQ == g, chosen, 0.0)

        pos = lax.broadcasted_iota(jnp.int32, (N_HEADS, sel_cols), 1)
        inblk = jnp.dot(selh.astype(BF16), ex_ref[...], preferred_element_type=F32)
        ok_s = (inblk > 0.5) & (pos <= past_len)
        s = _mm(qbd, ks_ref[...]) + bias_ref[:, nh:nh + sel_cols]
        ps = _softmax_lanes(jnp.where(ok_s, s, NEG), ok_s.astype(F32)).astype(BF16)
        o_s = _own_group(lax.dot_general(ps, vs_ref[...], _NT, preferred_element_type=F32))

        posw = lax.broadcasted_iota(jnp.int32, (N_HEADS, win_cols), 1)
        ok_w = posw <= wb
        s = _mm(qbd, kw_ref[...]) + bias_ref[:, nh + sel_cols:nh + sel_cols + win_cols]
        pw = _softmax_lanes(jnp.where(ok_w, s, NEG), ok_w.astype(F32)).astype(BF16)
        o_w = _own_group(lax.dot_general(pw, vw_ref[...], _NT, preferred_element_type=F32))

        gate = gate_ref[0]
        o_ref[0] = gate[:, 0:1] * o_c + gate[:, 1:2] * o_s + gate[:, 2:3] * o_w


def _decode_bias_idx(past_len, wb, ncp, sel_cols, win_cols):
    n = np.arange(ncp)
    cmp_d = past_len - (n * CMP_STRIDE + CMP_BLOCK - 1)
    sel_d = past_len - np.arange(sel_cols)
    win_d = np.concatenate([wb - np.arange(wb), np.zeros(win_cols - wb, np.int64)])
    idx = _t5_bucket_np(np.concatenate([cmp_d, sel_d, win_d]))
    pad = -len(idx) % (8 * LANES)
    return np.concatenate([idx, np.full(pad, N_BUCKETS - 1)]).reshape(-1, LANES).astype(np.int32)


def _nsa_decode(cache_kv, cache_win, page_table, qbd, new_cols, gate3, cw, rel_bias):
    n_pool, page = cache_kv.shape[:2]
    nseq, npages = page_table.shape
    past_len = npages * page
    wb = cache_win.shape[1]
    w1b, pe, w2b = cw
    nh = past_len // CMP_STRIDE
    n_cmp = (past_len + 1 - CMP_BLOCK) // CMP_STRIDE + 1
    sel_cols = past_len + LANES
    win_cols = wb + LANES
    nkv = N_KV * HEAD_DIM
    kvt = cache_kv.transpose(0, 2, 3, 4, 1).reshape(n_pool, 4 * nkv, page)
    wint = cache_win.transpose(0, 2, 3, 4, 1).reshape(nseq, 2 * nkv, wb)
    cov = jnp.asarray(_cover_t_np(LANES, LANES).T)
    idx = _decode_bias_idx(past_len, wb, nh, sel_cols, win_cols)
    bias = _bias_tables(rel_bias, jnp.asarray(idx)).reshape(N_HEADS, idx.size)
    ex = jnp.asarray((np.arange(LANES)[:, None] == np.arange(sel_cols)[None, :] // SEL_BLOCK).astype(np.float32), BF16)
    const = lambda a: pl.BlockSpec(a.shape, lambda b, p, pt: (0,) * a.ndim)
    pps = DECODE_PAGES_PER_STEP
    page_spec = lambda k: pl.BlockSpec((1, 4 * nkv, page), lambda b, p, pt: (pt[b, p * pps + k], 0, 0))
    grid_spec = pltpu.PrefetchScalarGridSpec(
        num_scalar_prefetch=1,
        grid=(nseq, npages // pps),
        in_specs=[page_spec(k) for k in range(pps)] + [
                  pl.BlockSpec((1, 2 * nkv, wb), lambda b, p, pt: (b, 0, 0)),
                  pl.BlockSpec((1, N_HEADS, nkv), lambda b, p, pt: (b, 0, 0)),
                  pl.BlockSpec((1, nkv, LANES), lambda b, p, pt: (b, 0, 0)),
                  pl.BlockSpec((1, N_HEADS, LANES), lambda b, p, pt: (b, 0, 0)),
                  const(w1b), const(pe), const(w2b), const(cov), const(bias), const(ex)],
        out_specs=pl.BlockSpec((1, N_HEADS, HEAD_DIM), lambda b, p, pt: (b, 0, 0)),
        scratch_shapes=[pltpu.VMEM((pps, 2 * nkv // LANES, page, LANES), F32),
                        pltpu.VMEM((2, N_KV * nh, CMP_STRIDE * HEAD_DIM), F32),
                        pltpu.VMEM((nkv, sel_cols), BF16), pltpu.VMEM((nkv, sel_cols), BF16),
                        pltpu.VMEM((nkv, win_cols), BF16), pltpu.VMEM((nkv, win_cols), BF16)])
    out = pl.pallas_call(
        functools.partial(_nsa_decode_kernel, past_len=past_len, n_cmp=n_cmp),
        grid_spec=grid_spec,
        out_shape=jax.ShapeDtypeStruct((nseq, N_HEADS, HEAD_DIM), F32),
        compiler_params=_cp(("arbitrary", "arbitrary"), 56),
        name="nsa_decode",
    )(page_table, *([kvt] * pps), wint, qbd, new_cols, gate3, w1b, pe, w2b, cov, bias, ex)
    return out.reshape(nseq, Q_COLS)


def _decode_operands(qs, kvbs, gates):
    nseq = qs.shape[0]
    nkv = N_KV * HEAD_DIM
    own = (np.arange(N_HEADS)[:, None] // GQ) == (np.arange(nkv)[None, :] // HEAD_DIM)
    qbd = jnp.where(own[None], jnp.tile(qs.reshape(nseq, N_HEADS, HEAD_DIM), (1, 1, N_KV)), 0.0)
    new_cols = jnp.pad(kvbs.astype(F32).reshape(nseq, 4, nkv).transpose(0, 2, 1), ((0, 0), (0, 0), (0, LANES - 4)))
    gate3 = jnp.pad(gates[:, :GATE_COLS].reshape(nseq, 3, N_HEADS).transpose(0, 2, 1),
                    ((0, 0), (0, 0), (0, LANES - 3)))
    return qbd, new_cols, gate3


PROMPT_TM = 512


def kernel(x_prompt, x_sample, state_s5_re, state_s5_im, cache_kv, cache_win, page_table, rel_bias, norm_mix, norm_ffn, norm_final, s5_lam_re, s5_lam_im, s5_log_dt, s5_b_re, s5_b_im, s5_c_re, s5_c_im, s5_d, s5_w_glu, ffn_w_in, ffn_w_out, nsa_w_in, nsa_phi_pe, nsa_phi_w1, nsa_phi_w2, nsa_w_o, moe_router, moe_w_in, moe_w_out):
    batch, length, _ = x_prompt.shape
    nseq = x_sample.shape[0]
    xp = x_prompt.reshape(batch * length, D_MODEL)
    xs = x_sample.reshape(nseq, D_MODEL)
    nstate = N_GROUPS * STATE

    wk, ws, wv, apow, a1 = _s5_prep(s5_lam_re, s5_lam_im, s5_log_dt, s5_b_re, s5_b_im, s5_c_re, s5_c_im)
    w_glu = s5_w_glu.astype(BF16)
    ffn_in, ffn_out = ffn_w_in.astype(BF16), ffn_w_out.astype(BF16)
    nsa_in = jnp.pad(nsa_w_in, ((0, 0), (0, NSA_PAD - nsa_w_in.shape[1]))).astype(BF16)
    w_o = nsa_w_o.astype(BF16)
    router = jnp.pad(moe_router, ((0, 0), (0, LANES - N_EXPERTS))).astype(BF16)
    moe_in, moe_out = moe_w_in.astype(BF16), moe_w_out.astype(BF16)
    cw = _compress_weights(nsa_phi_pe, nsa_phi_w1, nsa_phi_w2)
    tabs = _bias_tables(rel_bias, jnp.asarray(_prompt_bias_idx()))
    tab = tabs[:, :2 * QT]
    gflat = tabs[:, 2 * QT:].reshape(N_HEADS * QT, LANES)
    ghi = gflat.astype(BF16)
    glo = (gflat - ghi.astype(F32)).astype(BF16)

    up = _rmsnorm(xp, norm_mix[0], PROMPT_TM)
    gp, sre_p, sim_p = _s5_prompt(up, s5_d, wk, ws, wv, apow, batch, length)
    y1p, h1p = _glu(gp, xp, w_glu, norm_ffn[0], PROMPT_TM)
    y2p, h2p = _ffn(h1p, y1p, ffn_in, ffn_out, norm_mix[1], PROMPT_TM)
    q, kvf, winf, kvb, gate = _nsa_proj(h2p, nsa_in, PROMPT_TM)
    kcv = _compress_prompt(kvf, cw, batch, length)
    oc, sel = _cmp_attn(q, kcv, gate, ghi, glo, batch, length)
    os_ = _selected_attn(q, kvb, gate, _selected_tables(tab), sel, batch, length)
    ow = _window_attn(q, kvb, gate, _window_tables(tab), batch, length)
    y3p, h3p, mgp, rtp, cntp = _merge(oc, os_, ow, y2p, w_o, norm_ffn[1], router, MOE_TM)
    counts = cntp[:, 0, :N_EXPERTS].astype(jnp.int32).reshape(-1)
    yp = _moe_routed(h3p, y3p, mgp, rtp, counts, moe_in, moe_out, norm_final)

    us = _rmsnorm(xs, norm_mix[0], nseq)
    gs, sre_s, sim_s = _s5_sample(us, state_s5_re.reshape(nseq, nstate), state_s5_im.reshape(nseq, nstate),
                                  ws, wv, a1, s5_d)
    y1s, h1s = _glu(gs, xs, w_glu, norm_ffn[0], nseq)
    y2s, h2s = _ffn(h1s, y1s, ffn_in, ffn_out, norm_mix[1], nseq)
    qs, kvfs, winfs, kvbs, gates = _nsa_proj(h2s, nsa_in, nseq)
    qbd, new_cols, gate3 = _decode_operands(qs, kvbs, gates)
    od = _nsa_decode(cache_kv, cache_win, page_table, qbd, new_cols, gate3, cw, rel_bias)
    zero = jnp.zeros((nseq, D_MODEL), F32)
    y3s, h3s, mgs, _, _ = _merge(od, zero, zero, y2s, w_o, norm_ffn[1], router, nseq)
    ys = _moe(h3s, y3s, mgs, moe_in, moe_out, norm_final, nseq)

    wp = min(WINDOW, length)
    return (yp.reshape(batch, length, D_MODEL),
            ys.reshape(nseq, 1, D_MODEL),
            sre_p.reshape(batch, N_GROUPS, STATE), sim_p.reshape(batch, N_GROUPS, STATE),
            kvf.reshape(batch, length, 4, N_KV, HEAD_DIM),
            winf.reshape(batch, length, 2, N_KV, HEAD_DIM)[:, length - wp:],
            sre_s.reshape(nseq, N_GROUPS, STATE), sim_s.reshape(nseq, N_GROUPS, STATE),
            kvfs.reshape(nseq, 1, 4, N_KV, HEAD_DIM),
            winfs.reshape(nseq, 1, 2, N_KV, HEAD_DIM))
```

```python
import functools
import math

import numpy as np
import jax
import jax.numpy as jnp
from jax import lax
from jax.experimental import pallas as pl
from jax.experimental.pallas import tpu as pltpu

F32 = jnp.float32
BF16 = jnp.bfloat16
HIGHEST = lax.Precision.HIGHEST

D_MODEL = 1024
GROUP_CH = 16
N_GROUPS = D_MODEL // GROUP_CH
STATE = 64
N_HEADS = 16
HEAD_DIM = 64
N_KV = 4
GQ = N_HEADS // N_KV
CMP_BLOCK = 32
CMP_STRIDE = 16
CMP_HIDDEN = 2 * HEAD_DIM
SEL_BLOCK = 64
TOP_N = 16
WINDOW = 512
Q_COLS = N_HEADS * HEAD_DIM
KV_COLS = 6 * N_KV * HEAD_DIM
GATE_COLS = 3 * N_HEADS
N_BUCKETS = 32
MAX_DISTANCE = 128
D_FF = 2816
N_EXPERTS = 8
EXPERT_FF = 1408
EPS = 1e-6
NEG = -1e30
FORCE = 1e4

LANES = 128
S5_T = 8
S5_SG = LANES // GROUP_CH
S5_NSG = N_GROUPS // S5_SG
S5_SL = S5_SG * STATE
QT = 128


def _cp(sem, vmem_mb):
    return pltpu.CompilerParams(dimension_semantics=sem, vmem_limit_bytes=vmem_mb << 20)


def _cmul(ar, ai, br, bi):
    return ar * br - ai * bi, ar * bi + ai * br


def _s5_prep_kernel(lr_r, li_r, ldt_r, br_r, bi_r, lr_c, li_c, ldt_c, cr_c, ci_c,
                    wk_ref, ws_ref, wv_ref, apow_ref, a1_ref):
    def disc(lr, li, ldt):
        dt = jnp.exp(ldt)
        mag = jnp.exp(lr * dt)
        abr = mag * jnp.cos(li * dt)
        abi = mag * jnp.sin(li * dt)
        den = lr * lr + li * li
        nr = abr - 1.0
        fr = (nr * lr + abi * li) / den
        fi = (abi * lr - nr * li) / den
        return abr, abi, fr, fi

    abr, abi, fr, fi = disc(lr_r[0], li_r[0], ldt_r[0])
    rg = lax.broadcasted_iota(jnp.int32, (LANES, S5_SL), 0) // GROUP_CH
    cg = lax.broadcasted_iota(jnp.int32, (LANES, S5_SL), 1) // STATE
    mask_r = (rg == cg).astype(F32)
    b_r, b_i = br_r[...], bi_r[...]
    bbr = (fr * b_r - fi * b_i) * mask_r
    bbi = (fr * b_i + fi * b_r) * mask_r
    abr_c, abi_c, _, _ = disc(lr_c[0], li_c[0], ldt_c[0])
    rg2 = lax.broadcasted_iota(jnp.int32, (S5_SL, LANES), 0) // STATE
    cg2 = lax.broadcasted_iota(jnp.int32, (S5_SL, LANES), 1) // GROUP_CH
    mask_c = (rg2 == cg2).astype(F32)
    c_r = cr_c[0] * mask_c
    c_i = ci_c[0] * mask_c

    a1_ref[0, :, :S5_SL] = abr
    a1_ref[0, :, S5_SL:] = abi

    pr = jnp.ones_like(abr)
    pi = jnp.zeros_like(abr)
    pr_c = jnp.ones_like(abr_c)
    pi_c = jnp.zeros_like(abr_c)
    for tau in range(S5_T + 1):
        wv_ref[tau, 0, :S5_SL, :] = (c_r * pr_c - c_i * pi_c).astype(BF16)
        wv_ref[tau, 0, S5_SL:, :] = (-(c_r * pi_c + c_i * pr_c)).astype(BF16)
        if tau < S5_T:
            er, ei = _cmul(pr, pi, bbr, bbi)
            s = S5_T - 1 - tau
            ws_ref[s, 0, :, :S5_SL] = er.astype(BF16)
            ws_ref[s, 0, :, S5_SL:] = ei.astype(BF16)
            k = (jnp.dot(er, c_r, precision=HIGHEST, preferred_element_type=F32)
                 - jnp.dot(ei, c_i, precision=HIGHEST, preferred_element_type=F32))
            wk_ref[tau, 0] = k.astype(BF16)
        pr, pi = _cmul(pr, pi, abr, abi)
        pr_c, pi_c = _cmul(pr_c, pi_c, abr_c, abi_c)
    tr, ti = jnp.ones_like(abr), jnp.zeros_like(abr)
    for _ in range(S5_T):
        tr, ti = _cmul(tr, ti, abr, abi)
    qr, qi = jnp.ones_like(abr), jnp.zeros_like(abr)
    for i in range(S5_T + 1):
        apow_ref[0, i:i + 1, :S5_SL] = qr
        apow_ref[0, i:i + 1, S5_SL:] = qi
        qr, qi = _cmul(qr, qi, tr, ti)


def _s5_prep(lam_re, lam_im, log_dt, b_re, b_im, c_re, c_im):
    n = S5_NSG
    row = lambda a: a.reshape(n, 1, S5_SL)
    col = lambda a: a.reshape(n, S5_SL, 1)
    ldt = jnp.repeat(log_dt, STATE)
    bt = lambda b: jnp.tile(b.transpose(0, 2, 1).reshape(D_MODEL, STATE), (1, S5_SG))
    ct = lambda c: jnp.tile(c.reshape(n, LANES, STATE).transpose(0, 2, 1), (1, S5_SG, 1))
    rspec = pl.BlockSpec((1, 1, S5_SL), lambda s: (s, 0, 0))
    cspec = pl.BlockSpec((1, S5_SL, 1), lambda s: (s, 0, 0))
    t1 = S5_T + 1
    return pl.pallas_call(
        _s5_prep_kernel,
        grid=(n,),
        in_specs=[rspec, rspec, rspec,
                  pl.BlockSpec((LANES, S5_SL), lambda s: (s, 0)), pl.BlockSpec((LANES, S5_SL), lambda s: (s, 0)),
                  cspec, cspec, cspec,
                  pl.BlockSpec((1, S5_SL, LANES), lambda s: (s, 0, 0)), pl.BlockSpec((1, S5_SL, LANES), lambda s: (s, 0, 0))],
        out_specs=[pl.BlockSpec((S5_T, 1, LANES, LANES), lambda s: (0, s, 0, 0)),
                   pl.BlockSpec((S5_T, 1, LANES, 2 * S5_SL), lambda s: (0, s, 0, 0)),
                   pl.BlockSpec((t1, 1, 2 * S5_SL, LANES), lambda s: (0, s, 0, 0)),
                   pl.BlockSpec((1, t1 + 7, 2 * S5_SL), lambda s: (s, 0, 0)),
                   pl.BlockSpec((1, 1, 2 * S5_SL), lambda s: (s, 0, 0))],
        out_shape=[jax.ShapeDtypeStruct((S5_T, n, LANES, LANES), BF16),
                   jax.ShapeDtypeStruct((S5_T, n, LANES, 2 * S5_SL), BF16),
                   jax.ShapeDtypeStruct((t1, n, 2 * S5_SL, LANES), BF16),
                   jax.ShapeDtypeStruct((n, t1 + 7, 2 * S5_SL), F32),
                   jax.ShapeDtypeStruct((n, 1, 2 * S5_SL), F32)],
        compiler_params=_cp(("arbitrary",), 48),
        name="s5_prep",
    )(row(lam_re), row(lam_im), row(ldt), bt(b_re), bt(b_im),
      col(lam_re), col(lam_im), col(ldt), ct(c_re), ct(c_im))


def _rms(x, g):
    return x * lax.rsqrt(jnp.mean(x * x, axis=-1, keepdims=True) + EPS) * g


def _rmsnorm_kernel(x_ref, g_ref, o_ref):
    o_ref[...] = _rms(x_ref[...], g_ref[...]).astype(o_ref.dtype)


def _rmsnorm(x, g, tm, dtype=F32):
    n = x.shape[0]
    return pl.pallas_call(
        _rmsnorm_kernel,
        grid=(n // tm,),
        in_specs=[pl.BlockSpec((tm, D_MODEL), lambda i: (i, 0)), pl.BlockSpec((1, D_MODEL), lambda i: (0, 0))],
        out_specs=pl.BlockSpec((tm, D_MODEL), lambda i: (i, 0)),
        out_shape=jax.ShapeDtypeStruct((n, D_MODEL), dtype),
        compiler_params=_cp(("parallel",), 32),
        name="rmsnorm",
    )(x, g.reshape(1, D_MODEL))


def _s5_prompt_kernel(u_ref, d_ref, wk_ref, ws_ref, wv_ref, apow_ref, g_ref, sre_ref, sim_ref,
                      ub_ref, s_ref, p_ref, gc_ref, hs_ref, y_ref):
    length = u_ref.shape[0]
    r = length // S5_T
    ng = r // 8

    acc = jnp.zeros((r, 2 * S5_SL), F32)
    for s in range(S5_T):
        us = u_ref[pl.ds(s, r, stride=S5_T), :].astype(BF16)
        ub_ref[s] = us
        acc = acc + jnp.dot(us, ws_ref[s, 0], preferred_element_type=F32)
    for c in range(2 * S5_SL // LANES):
        s_ref[c] = acc[:, c * LANES:(c + 1) * LANES]

    nsl = S5_SL // LANES
    for c in range(nsl):
        lr = slice(c * LANES, (c + 1) * LANES)
        li = slice(S5_SL + c * LANES, S5_SL + (c + 1) * LANES)

        def apw(i):
            return apow_ref[0, i:i + 1, lr], apow_ref[0, i:i + 1, li]

        a1r, a1i = apw(1)
        pr = s_ref[c, pl.ds(0, ng, stride=8), :]
        pi = s_ref[nsl + c, pl.ds(0, ng, stride=8), :]
        p_ref[0, c] = pr
        p_ref[0, nsl + c] = pi
        for i in range(1, 8):
            qr, qi = _cmul(a1r, a1i, pr, pi)
            pr = qr + s_ref[c, pl.ds(i, ng, stride=8), :]
            pi = qi + s_ref[nsl + c, pl.ds(i, ng, stride=8), :]
            p_ref[i, c] = pr
            p_ref[i, nsl + c] = pi

        a8r, a8i = apw(8)

        def carry(m, g):
            gr, gi = g
            gc_ref[c, pl.ds(m, 1), :] = gr
            gc_ref[nsl + c, pl.ds(m, 1), :] = gi
            nr, ni = _cmul(a8r, a8i, gr, gi)
            return (nr + p_ref[7, c, pl.ds(m, 1), :], ni + p_ref[7, nsl + c, pl.ds(m, 1), :])

        z = jnp.zeros((1, LANES), F32)
        er, ei = lax.fori_loop(0, ng, carry, (z, z))
        sre_ref[0, :, lr] = er
        sim_ref[0, :, lr] = ei

        gr = gc_ref[c]
        gi = gc_ref[nsl + c]
        hs_ref[c, pl.ds(0, ng, stride=8), :] = gr
        hs_ref[nsl + c, pl.ds(0, ng, stride=8), :] = gi
        for i in range(1, 8):
            air, aii = apw(i)
            hr, hi = _cmul(air, aii, gr, gi)
            hs_ref[c, pl.ds(i, ng, stride=8), :] = hr + p_ref[i - 1, c]
            hs_ref[nsl + c, pl.ds(i, ng, stride=8), :] = hi + p_ref[i - 1, nsl + c]

    hs = jnp.concatenate([hs_ref[c] for c in range(2 * S5_SL // LANES)], axis=1).astype(BF16)
    for t in range(S5_T):
        o = jnp.dot(hs, wv_ref[t + 1, 0], preferred_element_type=F32)
        for s in range(t + 1):
            o = o + jnp.dot(ub_ref[s], wk_ref[t - s, 0], preferred_element_type=F32)
        y_ref[pl.ds(t, r, stride=S5_T), :] = o
    y = y_ref[...] + d_ref[...] * u_ref[...]
    g_ref[...] = jax.nn.gelu(y).astype(g_ref.dtype)


def _s5_prompt(u, d_skip, wk, ws, wv, apow, batch, length):
    r = length // S5_T
    t1 = S5_T + 1
    return pl.pallas_call(
        _s5_prompt_kernel,
        grid=(S5_NSG, batch),
        in_specs=[pl.BlockSpec((length, LANES), lambda s, b: (b, s)),
                  pl.BlockSpec((1, LANES), lambda s, b: (0, s)),
                  pl.BlockSpec((S5_T, 1, LANES, LANES), lambda s, b: (0, s, 0, 0)),
                  pl.BlockSpec((S5_T, 1, LANES, 2 * S5_SL), lambda s, b: (0, s, 0, 0)),
                  pl.BlockSpec((t1, 1, 2 * S5_SL, LANES), lambda s, b: (0, s, 0, 0)),
                  pl.BlockSpec((1, t1 + 7, 2 * S5_SL), lambda s, b: (s, 0, 0))],
        out_specs=[pl.BlockSpec((length, LANES), lambda s, b: (b, s)),
                   pl.BlockSpec((1, 1, S5_SL), lambda s, b: (b, 0, s)),
                   pl.BlockSpec((1, 1, S5_SL), lambda s, b: (b, 0, s))],
        out_shape=[jax.ShapeDtypeStruct((batch * length, D_MODEL), BF16),
                   jax.ShapeDtypeStruct((batch, 1, N_GROUPS * STATE), F32),
                   jax.ShapeDtypeStruct((batch, 1, N_GROUPS * STATE), F32)],
        scratch_shapes=[pltpu.VMEM((S5_T, r, LANES), BF16),
                        pltpu.VMEM((2 * S5_SL // LANES, r, LANES), F32),
                        pltpu.VMEM((8, 2 * S5_SL // LANES, r // 8, LANES), F32),
                        pltpu.VMEM((2 * S5_SL // LANES, r // 8, LANES), F32),
                        pltpu.VMEM((2 * S5_SL // LANES, r, LANES), F32),
                        pltpu.VMEM((length, LANES), F32)],
        compiler_params=_cp(("arbitrary", "arbitrary"), 56),
        name="s5_prompt",
    )(u, d_skip.reshape(1, D_MODEL), wk, ws, wv, apow)


def _resident(shape):
    nd = len(shape)
    return pl.BlockSpec(shape, lambda *_: (0,) * nd, pipeline_mode=pl.Buffered(1))


def _rows(tm, width):
    return pl.BlockSpec((tm, width), lambda i: (i, 0))


def _mm(a, b):
    return jnp.dot(a, b, preferred_element_type=F32)


def _glu_kernel(g_ref, x_ref, w_ref, gain_ref, y_ref, h_ref):
    p = _mm(g_ref[...], w_ref[...])
    y = x_ref[...] + p[:, :D_MODEL] * jax.nn.sigmoid(p[:, D_MODEL:])
    y_ref[...] = y
    h_ref[...] = _rms(y, gain_ref[...]).astype(h_ref.dtype)


def _glu(g, x, w_glu, gain, tm):
    n = g.shape[0]
    return pl.pallas_call(
        _glu_kernel,
        grid=(n // tm,),
        in_specs=[_rows(tm, D_MODEL), _rows(tm, D_MODEL), _resident((D_MODEL, 2 * D_MODEL)), _resident((1, D_MODEL))],
        out_specs=[_rows(tm, D_MODEL), _rows(tm, D_MODEL)],
        out_shape=[jax.ShapeDtypeStruct((n, D_MODEL), F32), jax.ShapeDtypeStruct((n, D_MODEL), g.dtype)],
        compiler_params=_cp(("parallel",), 48),
        name="s5_glu",
    )(g, x, w_glu, gain.reshape(1, D_MODEL))


def _chunks(total, size):
    return [(s, min(size, total - s)) for s in range(0, total, size)]


def _swiglu_acc(h, win_ref, wout_ref, ff, lead=()):
    acc = None
    for s, w in _chunks(ff, 256):
        a = jnp.dot(h, win_ref[lead + (slice(None), slice(s, s + w))], preferred_element_type=F32)
        b = jnp.dot(h, win_ref[lead + (slice(None), slice(ff + s, ff + s + w))], preferred_element_type=F32)
        act = (jax.nn.silu(a) * b).astype(BF16)
        z = jnp.dot(act, wout_ref[lead + (slice(s, s + w), slice(None))], preferred_element_type=F32)
        acc = z if acc is None else acc + z
    return acc


def _ffn_kernel(h_ref, y_ref, win_ref, wout_ref, gain_ref, yo_ref, ho_ref):
    y = y_ref[...] + _swiglu_acc(h_ref[...], win_ref, wout_ref, D_FF)
    yo_ref[...] = y
    ho_ref[...] = _rms(y, gain_ref[...]).astype(ho_ref.dtype)


def _ffn(h, y, w_in, w_out, gain, tm):
    n = h.shape[0]
    return pl.pallas_call(
        _ffn_kernel,
        grid=(n // tm,),
        in_specs=[_rows(tm, D_MODEL), _rows(tm, D_MODEL), _resident((D_MODEL, 2 * D_FF)), _resident((D_FF, D_MODEL)),
                  _resident((1, D_MODEL))],
        out_specs=[_rows(tm, D_MODEL), _rows(tm, D_MODEL)],
        out_shape=[jax.ShapeDtypeStruct((n, D_MODEL), F32), jax.ShapeDtypeStruct((n, D_MODEL), BF16)],
        compiler_params=_cp(("parallel",), 56),
        name="ffn_swiglu",
    )(h, y, w_in, w_out, gain.reshape(1, D_MODEL))


NSA_PAD = 2688
KV_OFF = Q_COLS


def _nsa_proj_kernel(h_ref, w_ref, q_ref, kvf_ref, winf_ref, kvb_ref, gate_ref):
    p = _mm(h_ref[...], w_ref[...])
    q_ref[...] = (p[:, :Q_COLS] * (HEAD_DIM ** -0.5)).astype(q_ref.dtype)
    kvf_ref[...] = p[:, KV_OFF:KV_OFF + 1024]
    winf_ref[...] = p[:, KV_OFF + 1024:KV_OFF + 1536]
    kvb_ref[...] = p[:, KV_OFF + 512:KV_OFF + 1536].astype(kvb_ref.dtype)
    gate_ref[...] = jax.nn.sigmoid(p[:, KV_OFF + 1536:])


def _nsa_proj(h, w_in_pad, tm):
    n = h.shape[0]
    return pl.pallas_call(
        _nsa_proj_kernel,
        grid=(n // tm,),
        in_specs=[_rows(tm, D_MODEL), _resident((D_MODEL, NSA_PAD))],
        out_specs=[_rows(tm, Q_COLS), _rows(tm, 1024), _rows(tm, 512), _rows(tm, 1024), _rows(tm, LANES)],
        out_shape=[jax.ShapeDtypeStruct((n, Q_COLS), h.dtype),
                   jax.ShapeDtypeStruct((n, 1024), F32),
                   jax.ShapeDtypeStruct((n, 512), F32),
                   jax.ShapeDtypeStruct((n, 1024), h.dtype),
                   jax.ShapeDtypeStruct((n, LANES), F32)],
        compiler_params=_cp(("parallel",), 48),
        name="nsa_proj",
    )(h, w_in_pad)


def _merge_kernel(oc_ref, os_ref, ow_ref, y_ref, wo_ref, gain_ref, router_ref, yo_ref, ho_ref, mg_ref, rt_ref,
                  cnt_ref):
    o = (oc_ref[...].astype(F32) + os_ref[...].astype(F32) + ow_ref[...].astype(F32)).astype(wo_ref.dtype)
    y = y_ref[...] + _mm(o, wo_ref[...])
    yo_ref[...] = y
    hf = _rms(y, gain_ref[...])
    ho_ref[...] = hf.astype(ho_ref.dtype)
    logits = _mm(hf.astype(BF16), router_ref[...])
    lane = lax.broadcasted_iota(jnp.int32, logits.shape, 1)
    logits = jnp.where(lane < N_EXPERTS, logits, -jnp.inf)
    m1 = jnp.max(logits, axis=-1, keepdims=True)
    i1 = jnp.min(jnp.where(logits == m1, lane, LANES), axis=-1, keepdims=True)
    rest = jnp.where(lane == i1, -jnp.inf, logits)
    m2 = jnp.max(rest, axis=-1, keepdims=True)
    i2 = jnp.min(jnp.where(rest == m2, lane, LANES), axis=-1, keepdims=True)
    e2 = jnp.exp(m2 - m1)
    den = 1.0 + e2
    mg_ref[...] = jnp.where(lane == i1, 1.0 / den, 0.0) + jnp.where(lane == i2, e2 / den, 0.0)
    routed = jnp.where((lane == i1) | (lane == i2), 1.0, 0.0)
    rt_ref[...] = routed
    cnt_ref[0] = jnp.broadcast_to(jnp.sum(routed, axis=0, keepdims=True), cnt_ref.shape[1:])


def _merge(oc, os_, ow, y, w_o, gain, router_pad, tm):
    n = y.shape[0]
    return pl.pallas_call(
        _merge_kernel,
        grid=(n // tm,),
        in_specs=[_rows(tm, D_MODEL)] * 4 + [_resident((D_MODEL, D_MODEL)), _resident((1, D_MODEL)),
                                             _resident((D_MODEL, LANES))],
        out_specs=[_rows(tm, D_MODEL), _rows(tm, D_MODEL), _rows(tm, LANES), _rows(tm, LANES),
                   pl.BlockSpec((1, 8, LANES), lambda i: (i, 0, 0))],
        out_shape=[jax.ShapeDtypeStruct((n, D_MODEL), F32), jax.ShapeDtypeStruct((n, D_MODEL), BF16),
                   jax.ShapeDtypeStruct((n, LANES), F32), jax.ShapeDtypeStruct((n, LANES), F32),
                   jax.ShapeDtypeStruct((n // tm, 8, LANES), F32)],
        compiler_params=_cp(("parallel",), 56),
        name="nsa_merge_router",
    )(oc, os_, ow, y, w_o, gain.reshape(1, D_MODEL), router_pad)


def _moe_kernel(h_ref, y_ref, mg_ref, win_ref, wout_ref, gain_ref, o_ref, acc_ref):
    e = pl.program_id(1)

    @pl.when(e == 0)
    def _():
        acc_ref[...] = jnp.zeros_like(acc_ref)

    mg = mg_ref[...]
    lane = lax.broadcasted_iota(jnp.int32, mg.shape, 1)
    gcol = jnp.sum(jnp.where(lane == e, mg, 0.0), axis=-1, keepdims=True)
    acc_ref[...] += gcol * _swiglu_acc(h_ref[...], win_ref, wout_ref, EXPERT_FF, lead=(0,))

    @pl.when(e == N_EXPERTS - 1)
    def _():
        o_ref[...] = _rms(y_ref[...] + acc_ref[...], gain_ref[...])


def _moe(h, y, mg, w_in, w_out, gain, tm):
    n = h.shape[0]
    rows = lambda w: pl.BlockSpec((tm, w), lambda i, e: (i, 0))
    return pl.pallas_call(
        _moe_kernel,
        grid=(n // tm, N_EXPERTS),
        in_specs=[rows(D_MODEL), rows(D_MODEL), rows(LANES),
                  pl.BlockSpec((1, D_MODEL, 2 * EXPERT_FF), lambda i, e: (e, 0, 0)),
                  pl.BlockSpec((1, EXPERT_FF, D_MODEL), lambda i, e: (e, 0, 0)),
                  pl.BlockSpec((1, D_MODEL), lambda i, e: (0, 0))],
        out_specs=rows(D_MODEL),
        out_shape=jax.ShapeDtypeStruct((n, D_MODEL), F32),
        scratch_shapes=[pltpu.VMEM((tm, D_MODEL), F32)],
        compiler_params=_cp(("parallel", "arbitrary"), 56),
        name="moe_final_norm",
    )(h, y, mg, w_in, w_out, gain.reshape(1, D_MODEL))


MOE_TM = 1024


def _moe_passes(tm):
    first = tm // 4
    return [first] + [tm // 8] * ((tm - first) // (tm // 8))


def _moe_routed_kernel(cnt_ref, h_ref, y_ref, mg_ref, rt_ref, lower_ref, win_ref, wout_ref, gain_ref, o_ref,
                       acc_ref, rcol_ref, rrow_ref, rtrow_ref):
    t = pl.program_id(0)
    e = pl.program_id(1)
    tm = h_ref.shape[0]

    @pl.when(e == 0)
    def _():
        acc_ref[...] = jnp.zeros_like(acc_ref)
        rt = rt_ref[...]
        rank = _mm(lower_ref[...], rt.astype(BF16))
        rcol_ref[...] = rank
        rrow_ref[...] = rank.T
        rtrow_ref[...] = rt.T

    n = cnt_ref[t * N_EXPERTS + e]
    lane = lax.broadcasted_iota(jnp.int32, (tm, LANES), 1)
    col = lambda ref: jnp.sum(jnp.where(lane == e, ref[...], 0.0), axis=-1, keepdims=True)
    gcol, rank_col, rt_col = col(mg_ref), col(rcol_ref), col(rt_ref)
    rank_row = rrow_ref[pl.ds(e, 1), :]
    rt_row = rtrow_ref[pl.ds(e, 1), :]
    start = 0
    for rb in _moe_passes(tm):
        @pl.when(start < n)
        def _():
            ridx = (start + lax.broadcasted_iota(jnp.int32, (rb, tm), 0)).astype(F32)
            take = jnp.where((rank_row == ridx) & (rt_row > 0.5), 1.0, 0.0).astype(BF16)
            x = _mm(take, h_ref[...]).astype(BF16)
            y = _swiglu_acc(x, win_ref, wout_ref, EXPERT_FF, lead=(0,))
            cidx = (start + lax.broadcasted_iota(jnp.int32, (tm, rb), 1)).astype(F32)
            put = jnp.where((rank_col == cidx) & (rt_col > 0.5), 1.0, 0.0).astype(BF16)
            acc_ref[...] += gcol * _mm(put, y.astype(BF16))
        start += rb

    @pl.when(e == N_EXPERTS - 1)
    def _():
        o_ref[...] = _rms(y_ref[...] + acc_ref[...], gain_ref[...])


def _moe_routed(h, y, mg, rt, cnt, w_in, w_out, gain):
    n = h.shape[0]
    tm = MOE_TM
    lower = jnp.asarray(np.tril(np.ones((tm, tm), np.float32), -1), BF16)
    rows = lambda w, **kw: pl.BlockSpec((tm, w), lambda i, e, c: (i, 0), **kw)
    grid_spec = pltpu.PrefetchScalarGridSpec(
        num_scalar_prefetch=1,
        grid=(n // tm, N_EXPERTS),
        in_specs=[rows(D_MODEL), rows(D_MODEL, pipeline_mode=pl.Buffered(1)), rows(LANES), rows(LANES),
                  pl.BlockSpec((tm, tm), lambda i, e, c: (0, 0), pipeline_mode=pl.Buffered(1)),
                  pl.BlockSpec((1, D_MODEL, 2 * EXPERT_FF), lambda i, e, c: (e, 0, 0)),
                  pl.BlockSpec((1, EXPERT_FF, D_MODEL), lambda i, e, c: (e, 0, 0)),
                  pl.BlockSpec((1, D_MODEL), lambda i, e, c: (0, 0))],
        out_specs=rows(D_MODEL),
        scratch_shapes=[pltpu.VMEM((tm, D_MODEL), F32),
                        pltpu.VMEM((tm, LANES), F32),
                        pltpu.VMEM((LANES, tm), F32),
                        pltpu.VMEM((LANES, tm), F32)])
    return pl.pallas_call(
        _moe_routed_kernel,
        grid_spec=grid_spec,
        out_shape=jax.ShapeDtypeStruct((n, D_MODEL), F32),
        compiler_params=_cp(("arbitrary", "arbitrary"), 58),
        name="moe_routed_final_norm",
    )(cnt, h, y, mg, rt, lower, w_in, w_out, gain.reshape(1, D_MODEL))


def _t5_bucket_np(dist):
    n = np.maximum(np.asarray(dist, np.int64), 0)
    exact = N_BUCKETS // 2
    logpart = exact + np.floor(np.log(np.maximum(n, 1) / exact) / math.log(MAX_DISTANCE / exact)
                               * (N_BUCKETS - exact)).astype(np.int64)
    return np.where(n < exact, n, np.minimum(logpart, N_BUCKETS - 1)).astype(np.int32)


def _bias_kernel(rb_ref, idx_ref, o_ref):
    h = pl.program_id(0)
    idx = idx_ref[...]
    far = rb_ref[N_BUCKETS - 1, h]
    acc = jnp.zeros(idx.shape, F32)
    for b in range(N_BUCKETS - 1):
        acc = jnp.where(idx == b, rb_ref[b, h] - far, acc)
    o_ref[0] = acc


def _bias_tables(rel_bias, idx):
    m = idx.shape[0]
    return pl.pallas_call(
        _bias_kernel,
        grid=(N_HEADS,),
        in_specs=[pl.BlockSpec(memory_space=pltpu.SMEM), pl.BlockSpec((m, LANES), lambda h: (0, 0))],
        out_specs=pl.BlockSpec((1, m, LANES), lambda h: (h, 0, 0)),
        out_shape=jax.ShapeDtypeStruct((N_HEADS, m, LANES), F32),
        compiler_params=_cp(("arbitrary",), 32),
        name="t5_bias_tables",
    )(rel_bias, idx)


CMP_NEAR = 32


def _prompt_bias_idx():
    a = np.arange(QT)[:, None]
    b = np.arange(QT)[None, :]
    t0 = _t5_bucket_np(a - b)
    t1 = _t5_bucket_np(QT + a - b)
    m = np.arange(LANES)[None, :]
    gc = np.where(m < CMP_NEAR, _t5_bucket_np(a + (CMP_NEAR // 2) * CMP_STRIDE - (CMP_BLOCK - 1) - CMP_STRIDE * m),
                  N_BUCKETS - 1)
    return np.concatenate([t0, t1, gc], axis=0).astype(np.int32)


def _flatten_pairs(load, nh):
    lane = lax.broadcasted_iota(jnp.int32, (nh, LANES), 1)
    ev, od = [], []
    for m in range(CMP_STRIDE // 2):
        sa = load(2 * m)
        sb = load(2 * m + 1)
        ev.append(jnp.where(lane < HEAD_DIM, sa, pltpu.roll(sb, HEAD_DIM, 1)))
        od.append(jnp.where(lane < HEAD_DIM, pltpu.roll(sa, HEAD_DIM, 1), sb))
    return jnp.concatenate(ev, axis=1), jnp.concatenate(od, axis=1)


def _compress_rows(xf, w1_ref, pe_ref, w2_ref, slot):
    rows = xf.shape[0]
    half = CMP_STRIDE * HEAD_DIM
    first = _mm((xf + pe_ref[slot, 0:1, :half]).astype(BF16), w1_ref[slot, :, :CMP_HIDDEN])
    second = _mm((xf + pe_ref[slot, 0:1, half:]).astype(BF16), w1_ref[slot, :, CMP_HIDDEN:])
    act = jax.nn.gelu(first + pltpu.roll(second, rows - 1, 0))
    return _mm(act.astype(BF16), w2_ref[slot])


def _compress_prompt_kernel(x0, x1, x2, x3, w1_ref, pe_ref, w2_ref, o_ref):
    nh = x0.shape[0] // CMP_STRIDE
    xs = (x0, x1, x2, x3)
    for slot in range(2):
        parts = []
        for c in range(2):
            x_ref = xs[2 * slot + c]
            ev, od = _flatten_pairs(lambda s: x_ref[pl.ds(s, nh, stride=CMP_STRIDE), :], nh)
            parts += [ev, od]
        xf = jnp.concatenate(parts, axis=0)
        o_ref[0, slot] = _compress_rows(xf, w1_ref, pe_ref, w2_ref, slot).astype(o_ref.dtype)


def _compress_weights(phi_pe, phi_w1, phi_w2):
    half = CMP_BLOCK // 2
    w1 = jnp.concatenate([phi_w1[:, :half].reshape(2, half * HEAD_DIM, CMP_HIDDEN),
                          phi_w1[:, half:].reshape(2, half * HEAD_DIM, CMP_HIDDEN)], axis=2)
    pe = jnp.broadcast_to(phi_pe.reshape(2, 1, CMP_BLOCK * HEAD_DIM), (2, 8, CMP_BLOCK * HEAD_DIM))
    return w1.astype(BF16), pe, phi_w2.astype(BF16)


def _compress_prompt(kvf, cw, batch, length):
    w1b, pe, w2b = cw
    nh = length // CMP_STRIDE
    xspec = lambda c: pl.BlockSpec((length, LANES), lambda b: (b, c))
    return pl.pallas_call(
        _compress_prompt_kernel,
        grid=(batch,),
        in_specs=[xspec(0), xspec(1), xspec(2), xspec(3), _resident(w1b.shape), _resident(pe.shape),
                  _resident(w2b.shape)],
        out_specs=pl.BlockSpec((1, 2, N_KV * nh, HEAD_DIM), lambda b: (b, 0, 0, 0)),
        out_shape=jax.ShapeDtypeStruct((batch, 2, N_KV * nh, HEAD_DIM), BF16),
        compiler_params=_cp(("parallel",), 48),
        name="nsa_compress_prompt",
    )(kvf, kvf, kvf, kvf, w1b, pe, w2b)


_NT = (((1,), (1,)), ((), ()))


def _cover_t_np(nc_pad, nb_pad):
    ratio = SEL_BLOCK // CMP_STRIDE
    span = CMP_BLOCK // CMP_STRIDE
    off = (np.arange(ratio)[:, None] - np.arange(span)[None, :]).reshape(-1)
    target = np.arange(nb_pad)[:, None] * ratio + off[None, :]
    cov = np.sum(np.arange(nc_pad)[:, None, None] == target[None], axis=-1)
    return np.ascontiguousarray(cov.T).astype(np.float32)


def _rank_select(score, jidx, nsel):
    nb = score.shape[0]
    cnt = jnp.zeros(score.shape, F32)
    for i in range(nb):
        ri = score[i:i + 1, :]
        cnt = cnt + jnp.where((ri > score) | ((ri == score) & (jidx > i)), 1.0, 0.0)
    return cnt < nsel


def _cmp_attn_kernel(q_ref, kcv_ref, gate_ref, ghi_ref, glo_ref, covt_ref, o_ref, sel_ref, selt_out_ref, bias_ref,
                     selt_ref):
    i = pl.program_id(1)
    q0 = i * QT
    nc = kcv_ref.shape[2] // N_KV
    nb = covt_ref.shape[0]
    base = q0 // CMP_STRIDE - CMP_NEAR // 2
    rr = lax.broadcasted_iota(jnp.int32, (LANES, nc), 0)
    cc = lax.broadcasted_iota(jnp.int32, (LANES, nc), 1)
    place = ((cc == base + rr) & (rr < CMP_NEAR)).astype(BF16)
    bias_ref[...] = (jnp.dot(ghi_ref[...], place, preferred_element_type=F32)
                     + jnp.dot(glo_ref[...], place, preferred_element_type=F32))
    t = q0 + lax.broadcasted_iota(jnp.int32, (QT, nc), 0)
    cend = lax.broadcasted_iota(jnp.int32, (QT, nc), 1) * CMP_STRIDE + (CMP_BLOCK - 1)
    vis = t >= cend
    visf = vis.astype(F32)
    madd = jnp.where(vis, 0.0, NEG)
    gate = gate_ref[...]
    jidx = lax.broadcasted_iota(jnp.int32, (nb, QT), 0)
    tq = q0 + lax.broadcasted_iota(jnp.int32, (nb, QT), 1)
    jt = tq // SEL_BLOCK
    forced = (jidx == 0) | (jidx == jt) | (jidx == jt - 1)
    visb = jidx * SEL_BLOCK <= tq
    for g in range(N_KV):
        kc = kcv_ref[0, 0, g * nc:(g + 1) * nc, :]
        vc = kcv_ref[0, 1, g * nc:(g + 1) * nc, :]
        psum = jnp.zeros((QT, nc), F32)
        for r in range(GQ):
            h = g * GQ + r
            qh = q_ref[:, h * HEAD_DIM:(h + 1) * HEAD_DIM]
            s = lax.dot_general(qh, kc, _NT, preferred_element_type=F32)
            s = s + bias_ref[h * QT:(h + 1) * QT, :] + madd
            m = jnp.max(s, axis=-1, keepdims=True)
            e = jnp.exp(s - m)
            p = ((e * visf) * (1.0 / jnp.sum(e, axis=-1, keepdims=True))).astype(BF16)
            psum = psum + p.astype(F32)
            o = _mm(p, vc)
            o_ref[:, h * HEAD_DIM:(h + 1) * HEAD_DIM] = (o * gate[:, h:h + 1]).astype(o_ref.dtype)
        imp = lax.dot_general(covt_ref[...], psum, _NT, precision=HIGHEST, preferred_element_type=F32)
        score = jnp.where(visb, jnp.where(forced, FORCE, imp), NEG)
        sel = _rank_select(score, jidx, min(TOP_N, nb)) & visb
        selt_ref[g * nb:(g + 1) * nb, :] = sel.astype(F32)
    sel_ref[...] = selt_ref[...].T
    selt_out_ref[0] = selt_ref[...]


def _cmp_attn(q, kcv, gate, ghi, glo, batch, length):
    nq = length // QT
    nc = kcv.shape[2] // N_KV
    nb = length // SEL_BLOCK
    covt = jnp.asarray(_cover_t_np(nc, nb))
    row = lambda w: pl.BlockSpec((QT, w), lambda b, i: (b * nq + i, 0))
    return pl.pallas_call(
        _cmp_attn_kernel,
        grid=(batch, nq),
        in_specs=[row(Q_COLS), pl.BlockSpec((1, 2, N_KV * nc, HEAD_DIM), lambda b, i: (b, 0, 0, 0)), row(LANES),
                  _resident(ghi.shape), _resident(glo.shape), _resident(covt.shape)],
        out_specs=[row(Q_COLS), row(N_KV * nb), pl.BlockSpec((1, N_KV * nb, QT), lambda b, i: (b * nq + i, 0, 0))],
        out_shape=[jax.ShapeDtypeStruct((batch * length, Q_COLS), BF16),
                   jax.ShapeDtypeStruct((batch * length, N_KV * nb), F32),
                   jax.ShapeDtypeStruct((batch * nq, N_KV * nb, QT), F32)],
        scratch_shapes=[pltpu.VMEM((N_HEADS * QT, nc), F32), pltpu.VMEM((N_KV * nb, QT), F32)],
        compiler_params=_cp(("parallel", "arbitrary"), 48),
        name="nsa_cmp_attn_select",
    )(q, kcv, gate, ghi, glo, covt)


SEL_CHUNK = 4
SEL_TAIL = SEL_CHUNK + 1
SEL_ZERO, SEL_MASKED = 2, 3


def _selected_kernel(q_ref, kv_ref, gate_ref, stab_ref, sel_ref, ex_ref, o_ref, *scratch):
    i = pl.program_id(1)
    length = kv_ref.shape[0]
    nb = ex_ref.shape[0]
    wide = SEL_CHUNK * QT
    tail = SEL_TAIL * QT
    nbulk = jnp.maximum(i - 1, 0) // SEL_CHUNK
    tstart = jnp.minimum(SEL_CHUNK * nbulk, length // QT - SEL_TAIL)
    trows = pl.ds(pl.multiple_of(tstart * QT, QT), tail)
    rep = lambda a: jnp.concatenate([a] * GQ, axis=0)
    gate = gate_ref[...]
    kinds = []
    for jj in range(SEL_TAIL):
        tile = tstart + jj
        kinds.append(jnp.where((tile < SEL_CHUNK * nbulk) | (tile > i), SEL_MASKED, jnp.minimum(i - tile, SEL_ZERO)))

    kcol = lambda g: slice(g * HEAD_DIM, (g + 1) * HEAD_DIM)
    vcol = lambda g: slice((N_KV + g) * HEAD_DIM, (N_KV + g + 1) * HEAD_DIM)
    q4 = lambda g: jnp.concatenate(
        [q_ref[:, (g * GQ + r) * HEAD_DIM:(g * GQ + r + 1) * HEAD_DIM] for r in range(GQ)], axis=0)

    per_set = len(scratch) // 2
    sets = (scratch[:per_set], scratch[per_set:])

    def prep(g):
        s_ref, mx_ref, mb_ref, ls_ref, acc_ref, madd_ref = sets[g % 2]
        sel = sel_ref[:, g * nb:(g + 1) * nb].astype(BF16)
        madd_ref[...] = (jnp.dot(sel, ex_ref[...], preferred_element_type=F32) - 1.0) * (-NEG)
        mx_ref[...] = jnp.full(mx_ref.shape, -jnp.inf, F32)
        ls_ref[...] = jnp.zeros(ls_ref.shape, F32)
        acc_ref[...] = jnp.zeros(acc_ref.shape, F32)

    def logits_chunk(g, c):
        s_ref, mx_ref, mb_ref, ls_ref, acc_ref, madd_ref = sets[g % 2]
        cols = pl.ds(pl.multiple_of(c * wide, wide), wide)
        s = (lax.dot_general(q4(g), kv_ref[cols, kcol(g)], _NT, preferred_element_type=F32)
             + rep(madd_ref[:, cols]))
        s_ref[:, cols] = s
        mx = mx_ref[...]
        for jj in range(SEL_CHUNK):
            mx = jnp.maximum(mx, s[:, jj * QT:(jj + 1) * QT])
        mx_ref[...] = mx

    def logits_tail(g):
        s_ref, mx_ref, mb_ref, ls_ref, acc_ref, madd_ref = sets[g % 2]
        s = lax.dot_general(q4(g), kv_ref[trows, kcol(g)], _NT, preferred_element_type=F32)
        adds = [jnp.concatenate([stab_ref[g * GQ + r, :, pl.ds(pl.multiple_of(kinds[jj] * QT, QT), QT)]
                                 for r in range(GQ)], axis=0) for jj in range(SEL_TAIL)]
        s = s + jnp.concatenate(adds, axis=1) + rep(madd_ref[:, trows])
        s_ref[:, length:] = s
        mx = mx_ref[...]
        for jj in range(SEL_TAIL):
            mx = jnp.maximum(mx, s[:, jj * QT:(jj + 1) * QT])
        mb_ref[...] = jnp.broadcast_to(jnp.max(mx, axis=-1, keepdims=True), mx.shape)

    def pv_chunk(g, c):
        s_ref, mx_ref, mb_ref, ls_ref, acc_ref, madd_ref = sets[g % 2]
        cols = pl.ds(pl.multiple_of(c * wide, wide), wide)
        m = mb_ref[...]
        ps = [jnp.exp(s_ref[:, pl.ds(pl.multiple_of(c * wide + jj * QT, QT), QT)] - m) for jj in range(SEL_CHUNK)]
        lsum = ls_ref[...]
        for p in ps:
            lsum = lsum + p
        ls_ref[...] = lsum
        acc_ref[...] += jnp.dot(jnp.concatenate(ps, axis=1).astype(BF16), kv_ref[cols, vcol(g)],
                                preferred_element_type=F32)

    def pv_tail(g):
        s_ref, mx_ref, mb_ref, ls_ref, acc_ref, madd_ref = sets[g % 2]
        m = mb_ref[...]
        ps = [jnp.exp(s_ref[:, length + jj * QT:length + (jj + 1) * QT] - m) for jj in range(SEL_TAIL)]
        lsum = ls_ref[...]
        for p in ps:
            lsum = lsum + p
        acc = acc_ref[...] + jnp.dot(jnp.concatenate(ps, axis=1).astype(BF16), kv_ref[trows, vcol(g)],
                                     preferred_element_type=F32)
        l = jnp.sum(lsum, axis=-1, keepdims=True)
        for r in range(GQ):
            h = g * GQ + r
            scale = gate[:, N_HEADS + h:N_HEADS + h + 1] / l[r * QT:(r + 1) * QT]
            o_ref[:, h * HEAD_DIM:(h + 1) * HEAD_DIM] = (acc[r * QT:(r + 1) * QT, :] * scale).astype(o_ref.dtype)

    def loop(*stages):
        def body(c, carry):
            for stage, g in stages:
                stage(g, c)
            return carry
        lax.fori_loop(0, nbulk, body, 0)

    prep(0)
    loop((logits_chunk, 0))
    logits_tail(0)
    for g in range(1, N_KV):
        prep(g)
        loop((logits_chunk, g), (pv_chunk, g - 1))
        pv_tail(g - 1)
        logits_tail(g)
    loop((pv_chunk, N_KV - 1))
    pv_tail(N_KV - 1)


_TN = (((0,), (0,)), ((), ()))


def _fold(x, op):
    return op(x.reshape(x.shape[0] // 8, 8, x.shape[1]), axis=0)


def _selected_t_kernel(q_ref, kv_ref, gate_ref, stab_ref, selt_ref, ext_ref, o_ref, *scratch):
    i = pl.program_id(1)
    length = kv_ref.shape[0]
    nb = ext_ref.shape[1]
    wide = SEL_CHUNK * QT
    tail = SEL_TAIL * QT
    nbulk = jnp.maximum(i - 1, 0) // SEL_CHUNK
    tstart = jnp.minimum(SEL_CHUNK * nbulk, length // QT - SEL_TAIL)
    trows = pl.ds(pl.multiple_of(tstart * QT, QT), tail)
    rep = lambda a: jnp.concatenate([a] * GQ, axis=1)
    gate_t = gate_ref[...].T
    kinds = []
    for jj in range(SEL_TAIL):
        tile = tstart + jj
        kinds.append(jnp.where((tile < SEL_CHUNK * nbulk) | (tile > i), SEL_MASKED, jnp.minimum(i - tile, SEL_ZERO)))

    kcol = lambda g: slice(g * HEAD_DIM, (g + 1) * HEAD_DIM)
    vcol = lambda g: slice((N_KV + g) * HEAD_DIM, (N_KV + g + 1) * HEAD_DIM)
    q4 = lambda g: jnp.concatenate(
        [q_ref[:, (g * GQ + r) * HEAD_DIM:(g * GQ + r + 1) * HEAD_DIM] for r in range(GQ)], axis=0)

    per_set = len(scratch) // 2
    sets = (scratch[:per_set], scratch[per_set:])

    def prep(g):
        s_ref, mx_ref, mb_ref, ls_ref, acc_ref, madd_ref = sets[g % 2]
        selt = selt_ref[0, g * nb:(g + 1) * nb, :].astype(BF16)
        madd_ref[...] = (jnp.dot(ext_ref[...], selt, preferred_element_type=F32) - 1.0) * (-NEG)
        mx_ref[...] = jnp.full(mx_ref.shape, -jnp.inf, F32)
        ls_ref[...] = jnp.zeros(ls_ref.shape, F32)
        acc_ref[...] = jnp.zeros(acc_ref.shape, F32)

    def logits_chunk(g, c):
        s_ref, mx_ref, mb_ref, ls_ref, acc_ref, madd_ref = sets[g % 2]
        rows = pl.ds(pl.multiple_of(c * wide, wide), wide)
        st = (lax.dot_general(kv_ref[rows, kcol(g)], q4(g), _NT, preferred_element_type=F32)
              + rep(madd_ref[rows, :]))
        s_ref[rows, :] = st
        mx_ref[...] = jnp.maximum(mx_ref[...], _fold(st, jnp.max))

    def logits_tail(g):
        s_ref, mx_ref, mb_ref, ls_ref, acc_ref, madd_ref = sets[g % 2]
        st = lax.dot_general(kv_ref[trows, kcol(g)], q4(g), _NT, preferred_element_type=F32)
        adds = [jnp.concatenate([stab_ref[g * GQ + r, pl.ds(pl.multiple_of(kinds[jj] * QT, QT), QT), :]
                                 for r in range(GQ)], axis=1) for jj in range(SEL_TAIL)]
        st = st + jnp.concatenate(adds, axis=0) + rep(madd_ref[trows, :])
        s_ref[length:, :] = st
        mx = jnp.maximum(mx_ref[...], _fold(st, jnp.max))
        mb_ref[...] = jnp.broadcast_to(jnp.max(mx, axis=0, keepdims=True), mx.shape)

    def probs(st, m8):
        return jnp.exp(st.reshape(st.shape[0] // 8, 8, st.shape[1]) - m8[None]).reshape(st.shape)

    def pv_chunk(g, c):
        s_ref, mx_ref, mb_ref, ls_ref, acc_ref, madd_ref = sets[g % 2]
        rows = pl.ds(pl.multiple_of(c * wide, wide), wide)
        p = probs(s_ref[rows, :], mb_ref[...])
        ls_ref[...] += _fold(p, jnp.sum)
        acc_ref[...] += lax.dot_general(kv_ref[rows, vcol(g)], p.astype(BF16), _TN, preferred_element_type=F32)

    def pv_tail(g):
        s_ref, mx_ref, mb_ref, ls_ref, acc_ref, madd_ref = sets[g % 2]
        p = probs(s_ref[length:, :], mb_ref[...])
        l = jnp.sum(ls_ref[...] + _fold(p, jnp.sum), axis=0, keepdims=True)
        acc = acc_ref[...] + lax.dot_general(kv_ref[trows, vcol(g)], p.astype(BF16), _TN,
                                             preferred_element_type=F32)
        first = N_HEADS + g * GQ
        gates = jnp.concatenate([gate_t[first + r:first + r + 1, :] for r in range(GQ)], axis=1)
        o4 = (acc * (gates / l)).T
        for r in range(GQ):
            h = g * GQ + r
            o_ref[:, h * HEAD_DIM:(h + 1) * HEAD_DIM] = o4[r * QT:(r + 1) * QT, :].astype(o_ref.dtype)

    def loop(*stages):
        def body(c, carry):
            for stage, g in stages:
                stage(g, c)
            return carry
        lax.fori_loop(0, nbulk, body, 0)

    prep(0)
    loop((logits_chunk, 0))
    logits_tail(0)
    for g in range(1, N_KV):
        prep(g)
        loop((logits_chunk, g), (pv_chunk, g - 1))
        pv_tail(g - 1)
        logits_tail(g)
    loop((pv_chunk, N_KV - 1))
    pv_tail(N_KV - 1)


def _selected_t_attn(q, kvb, gate, stab, selt, batch, length):
    nq = length // QT
    nb = length // SEL_BLOCK
    assert nq >= SEL_TAIL
    row = lambda w: pl.BlockSpec((QT, w), lambda b, i: (b * nq + i, 0))
    ext = jnp.asarray(_ex_np(nb, length).T, BF16)
    stab_t = stab.reshape(N_HEADS, QT, -1, QT).transpose(0, 2, 3, 1).reshape(N_HEADS, -1, QT)
    lanes4 = GQ * QT
    return pl.pallas_call(
        _selected_t_kernel,
        grid=(batch, nq),
        in_specs=[row(Q_COLS), pl.BlockSpec((length, 2 * N_KV * HEAD_DIM), lambda b, i: (b, 0)), row(LANES),
                  _resident(stab_t.shape), pl.BlockSpec((1, N_KV * nb, QT), lambda b, i: (b * nq + i, 0, 0)),
                  _resident(ext.shape)],
        out_specs=row(Q_COLS),
        out_shape=jax.ShapeDtypeStruct((batch * length, Q_COLS), BF16),
        scratch_shapes=2 * [pltpu.VMEM((length + SEL_TAIL * QT, lanes4), F32),
                            pltpu.VMEM((8, lanes4), F32),
                            pltpu.VMEM((8, lanes4), F32),
                            pltpu.VMEM((8, lanes4), F32),
                            pltpu.VMEM((HEAD_DIM, lanes4), F32),
                            pltpu.VMEM((length, QT), F32)],
        compiler_params=_cp(("parallel", "arbitrary"), 56),
        name="nsa_selected_attn",
    )(q, kvb, gate, stab_t, selt, ext)


WIN_TILES = WINDOW // QT + 1


def _window_kernel(q_ref, kv_ref, gate_ref, wtab_ref, o_ref):
    i = pl.program_id(1)
    first = jnp.maximum(i - (WIN_TILES - 1), 0)
    rows = pl.ds(pl.multiple_of(first * QT, QT), WIN_TILES * QT)
    gate = gate_ref[...]
    for g in range(N_KV):
        kcol = slice(g * HEAD_DIM, (g + 1) * HEAD_DIM)
        vcol = slice(N_KV * HEAD_DIM + g * HEAD_DIM, N_KV * HEAD_DIM + (g + 1) * HEAD_DIM)
        q4 = jnp.concatenate([q_ref[:, (g * GQ + r) * HEAD_DIM:(g * GQ + r + 1) * HEAD_DIM] for r in range(GQ)],
                             axis=0)
        s = lax.dot_general(q4, kv_ref[rows, kcol], _NT, preferred_element_type=F32)
        adds = []
        for jj in range(WIN_TILES):
            d = i - first - jj
            kind = jnp.where(d < 0, WIN_TILES, d)
            col = pl.ds(pl.multiple_of(kind * QT, QT), QT)
            adds.append(jnp.concatenate([wtab_ref[g * GQ + r, :, col] for r in range(GQ)], axis=0))
        s = s + jnp.concatenate(adds, axis=1)
        p = jnp.exp(s - jnp.max(s, axis=-1, keepdims=True))
        l = jnp.sum(p, axis=-1, keepdims=True)
        acc = jnp.dot(p.astype(BF16), kv_ref[rows, vcol], preferred_element_type=F32)
        for r in range(GQ):
            h = g * GQ + r
            scale = gate[:, 2 * N_HEADS + h:2 * N_HEADS + h + 1] / l[r * QT:(r + 1) * QT]
            o_ref[:, h * HEAD_DIM:(h + 1) * HEAD_DIM] = (acc[r * QT:(r + 1) * QT, :] * scale).astype(o_ref.dtype)


def _window_tables(tab):
    a = np.arange(QT)[:, None]
    b = np.arange(QT)[None, :]
    causal = np.where(a >= b, 0.0, NEG).astype(np.float32)
    edge = np.where(a <= b, 0.0, NEG).astype(np.float32)
    const = lambda m: jnp.broadcast_to(jnp.asarray(m, F32), (N_HEADS, QT, QT))
    zero = np.zeros((QT, QT), np.float32)
    return jnp.concatenate([tab[:, :QT] + causal, tab[:, QT:], const(zero), const(zero), const(edge),
                            const(np.full((QT, QT), NEG, np.float32))], axis=2)


def _window_attn(q, kvb, gate, wtab, batch, length):
    nq = length // QT
    row = lambda w: pl.BlockSpec((QT, w), lambda b, i: (b * nq + i, 0))
    return pl.pallas_call(
        _window_kernel,
        grid=(batch, nq),
        in_specs=[row(Q_COLS), pl.BlockSpec((length, 2 * N_KV * HEAD_DIM), lambda b, i: (b, 1)), row(LANES),
                  _resident(wtab.shape)],
        out_specs=row(Q_COLS),
        out_shape=jax.ShapeDtypeStruct((batch * length, Q_COLS), BF16),
        compiler_params=_cp(("parallel", "arbitrary"), 48),
        name="nsa_window_attn",
    )(q, kvb, gate, wtab)


def _ex_np(nb, length):
    return (np.arange(nb)[:, None] == (np.arange(length)[None, :] // SEL_BLOCK)).astype(np.float32)


def _selected_tables(tab):
    a = np.arange(QT)[:, None]
    b = np.arange(QT)[None, :]
    causal = np.where(a >= b, 0.0, NEG).astype(np.float32)
    const = lambda v: jnp.full((N_HEADS, QT, QT), v, F32)
    return jnp.concatenate([tab[:, :QT] + causal, tab[:, QT:], const(0.0), const(NEG)], axis=2)


def _selected_attn(q, kvb, gate, stab, selmask, batch, length):
    nq = length // QT
    nb = length // SEL_BLOCK
    assert nq >= SEL_TAIL
    row = lambda w: pl.BlockSpec((QT, w), lambda b, i: (b * nq + i, 0))
    ex = jnp.asarray(_ex_np(nb, length), BF16)
    rows4 = GQ * QT
    return pl.pallas_call(
        _selected_kernel,
        grid=(batch, nq),
        in_specs=[row(Q_COLS), pl.BlockSpec((length, 2 * N_KV * HEAD_DIM), lambda b, i: (b, 0)), row(LANES),
                  _resident(stab.shape), row(N_KV * nb), _resident(ex.shape)],
        out_specs=row(Q_COLS),
        out_shape=jax.ShapeDtypeStruct((batch * length, Q_COLS), BF16),
        scratch_shapes=2 * [pltpu.VMEM((rows4, length + SEL_TAIL * QT), F32),
                            pltpu.VMEM((rows4, QT), F32),
                            pltpu.VMEM((rows4, QT), F32),
                            pltpu.VMEM((rows4, QT), F32),
                            pltpu.VMEM((rows4, HEAD_DIM), F32),
                            pltpu.VMEM((QT, length), F32)],
        compiler_params=_cp(("parallel", "arbitrary"), 56),
        name="nsa_selected_attn",
    )(q, kvb, gate, stab, selmask, ex)


def _s5_sample_kernel(u_ref, hr_ref, hi_ref, ws_ref, wv_ref, a1_ref, d_ref, g_ref, sre_ref, sim_ref):
    u = u_ref[...]
    x = _mm(u.astype(BF16), ws_ref[0, 0])
    ar, ai = a1_ref[0, :, :S5_SL], a1_ref[0, :, S5_SL:]
    h0r, h0i = hr_ref[...], hi_ref[...]
    hr = x[:, :S5_SL] + (ar * h0r - ai * h0i)
    hi = x[:, S5_SL:] + (ar * h0i + ai * h0r)
    sre_ref[...] = hr
    sim_ref[...] = hi
    y = _mm(jnp.concatenate([hr, hi], axis=1).astype(BF16), wv_ref[0, 0])
    g_ref[...] = jax.nn.gelu(y + d_ref[...] * u).astype(g_ref.dtype)


def _s5_sample(u, h0_re, h0_im, ws, wv, a1, d_skip):
    n = u.shape[0]
    st = pl.BlockSpec((n, S5_SL), lambda s: (0, s))
    return pl.pallas_call(
        _s5_sample_kernel,
        grid=(S5_NSG,),
        in_specs=[pl.BlockSpec((n, LANES), lambda s: (0, s)), st, st,
                  pl.BlockSpec((1, 1, LANES, 2 * S5_SL), lambda s: (S5_T - 1, s, 0, 0)),
                  pl.BlockSpec((1, 1, 2 * S5_SL, LANES), lambda s: (0, s, 0, 0)),
                  pl.BlockSpec((1, 1, 2 * S5_SL), lambda s: (s, 0, 0)),
                  pl.BlockSpec((1, LANES), lambda s: (0, s))],
        out_specs=[pl.BlockSpec((n, LANES), lambda s: (0, s)), st, st],
        out_shape=[jax.ShapeDtypeStruct((n, D_MODEL), BF16),
                   jax.ShapeDtypeStruct((n, N_GROUPS * STATE), F32),
                   jax.ShapeDtypeStruct((n, N_GROUPS * STATE), F32)],
        compiler_params=_cp(("parallel",), 32),
        name="s5_sample",
    )(u, h0_re, h0_im, ws, wv, a1, d_skip.reshape(1, D_MODEL))


def _softmax_lanes(s, valid):
    m = jnp.max(s, axis=-1, keepdims=True)
    e = jnp.exp(s - m)
    return (e * valid) * (1.0 / jnp.sum(e, axis=-1, keepdims=True))


def _own_group(o):
    rowg = lax.broadcasted_iota(jnp.int32, (N_HEADS, HEAD_DIM), 0) // GQ
    out = jnp.zeros((N_HEADS, HEAD_DIM), F32)
    for g in range(N_KV):
        out = out + jnp.where(rowg == g, o[:, g * HEAD_DIM:(g + 1) * HEAD_DIM], 0.0)
    return out


DECODE_PAGES_PER_STEP = 4


def _nsa_decode_kernel(pt_ref, *refs, past_len, n_cmp):
    x_refs = refs[:DECODE_PAGES_PER_STEP]
    (win_ref, qbd_ref, new_ref, gate_ref, w1_ref, pe_ref, w2_ref, cov_ref, bias_ref, ex_ref, o_ref,
     xt_ref, xf_ref, ks_ref, vs_ref, kw_ref, vw_ref) = refs[DECODE_PAGES_PER_STEP:]
    step = pl.program_id(1)
    nsteps = pl.num_programs(1)
    page = x_refs[0].shape[2]
    hp = page // CMP_STRIDE
    nh = past_len // CMP_STRIDE
    nkv = N_KV * HEAD_DIM
    sel_cols = ks_ref.shape[1]
    win_cols = kw_ref.shape[1]
    wb = win_ref.shape[2]

    def stash(ref, cols, x):
        ref[:, cols] = x.astype(BF16)

    for k, x_ref in enumerate(x_refs):
        p = step * len(x_refs) + k
        xt = x_ref[0, :2 * nkv, :].T
        for c in range(2 * nkv // LANES):
            xt_ref[k, c] = xt[:, c * LANES:(c + 1) * LANES]
        for slot in range(2):
            for c in range(2):
                blk = 2 * slot + c
                ev, od = _flatten_pairs(lambda s: xt_ref[k, blk, pl.ds(s, hp, stride=CMP_STRIDE), :], hp)
                for head, val in ((2 * c, ev), (2 * c + 1, od)):
                    xf_ref[slot, pl.ds(pl.multiple_of(head * nh + p * hp, hp), hp), :] = val
        cols = pl.ds(pl.multiple_of(p * page, page), page)
        stash(ks_ref, cols, x_ref[0, 2 * nkv:3 * nkv, :])
        stash(vs_ref, cols, x_ref[0, 3 * nkv:, :])

    @pl.when(step == nsteps - 1)
    def _():
        qbd = qbd_ref[0]
        new = new_ref[0]
        lane0 = lax.broadcasted_iota(jnp.int32, (nkv, LANES), 1) == 0
        newcol = lambda k: jnp.where(lane0, jnp.broadcast_to(new[:, k:k + 1], (nkv, LANES)), 0.0)
        stash(ks_ref, slice(past_len, sel_cols), newcol(0))
        stash(vs_ref, slice(past_len, sel_cols), newcol(1))
        stash(kw_ref, slice(0, wb), win_ref[0, :nkv, :])
        stash(vw_ref, slice(0, wb), win_ref[0, nkv:, :])
        stash(kw_ref, slice(wb, win_cols), newcol(2))
        stash(vw_ref, slice(wb, win_cols), newcol(3))
        rowh = lax.broadcasted_iota(jnp.int32, (N_HEADS, LANES), 0)

        kc = _compress_rows(xf_ref[0], w1_ref, pe_ref, w2_ref, 0).astype(BF16)
        vc = _compress_rows(xf_ref[1], w1_ref, pe_ref, w2_ref, 1).astype(BF16)
        s = jnp.zeros((N_HEADS, nh), F32)
        for g in range(N_KV):
            s = s + lax.dot_general(qbd[:, g * HEAD_DIM:(g + 1) * HEAD_DIM], kc[g * nh:(g + 1) * nh], _NT,
                                    preferred_element_type=F32)
        nidx = lax.broadcasted_iota(jnp.int32, (N_HEADS, nh), 1)
        s = s + bias_ref[:, :nh] + jnp.where(nidx < n_cmp, 0.0, NEG)
        pc = _softmax_lanes(s, (nidx < n_cmp).astype(F32)).astype(BF16)
        o_c = _own_group(jnp.concatenate([_mm(pc, vc[g * nh:(g + 1) * nh]) for g in range(N_KV)], axis=1))

        pcf = pc.astype(F32)
        psum = jnp.concatenate([jnp.sum(pcf[g * GQ:(g + 1) * GQ], axis=0, keepdims=True) for g in range(N_KV)]
                               + [jnp.zeros((8 - N_KV, nh), F32)], axis=0)
        imp = jnp.dot(psum, cov_ref[...], precision=HIGHEST, preferred_element_type=F32)
        jt = past_len // SEL_BLOCK
        nb = jt + 1
        lane = lax.broadcasted_iota(jnp.int32, (LANES, LANES), 1)
        subl = lax.broadcasted_iota(jnp.int32, (LANES, LANES), 0)

        def scored(x, j):
            forced = (j == 0) | (j == jt) | (j == jt - 1)
            return jnp.where(j < nb, jnp.where(forced, FORCE, x), NEG)

        imp_sq = jnp.concatenate([imp, jnp.zeros((LANES - 8, LANES), F32)], axis=0)
        sc_row = scored(imp_sq, lane)
        sc_col = scored(imp_sq.T, subl)
        selh = jnp.zeros((N_HEADS, LANES), F32)
        for g in range(N_KV):
            col = sc_col[:, g:g + 1]
            row = sc_row[g:g + 1, :]
            ahead = (col > row) | ((col == row) & (subl < lane))
            rank = jnp.sum(jnp.where(ahead, 1.0, 0.0), axis=0, keepdims=True)
            chosen = jnp.where((rank < min(TOP_N, nb)) & (lane[:1] < nb), 1.0, 0.0)
            selh = selh + jnp.where(rowh // GQ == g, chosen, 0.0)

        pos = lax.broadcasted_iota(jnp.int32, (N_HEADS, sel_cols), 1)
        inblk = jnp.dot(selh.astype(BF16), ex_ref[...], preferred_element_type=F32)
        ok_s = (inblk > 0.5) & (pos <= past_len)
        s = _mm(qbd, ks_ref[...]) + bias_ref[:, nh:nh + sel_cols]
        ps = _softmax_lanes(jnp.where(ok_s, s, NEG), ok_s.astype(F32)).astype(BF16)
        o_s = _own_group(lax.dot_general(ps, vs_ref[...], _NT, preferred_element_type=F32))

        posw = lax.broadcasted_iota(jnp.int32, (N_HEADS, win_cols), 1)
        ok_w = posw <= wb
        s = _mm(qbd, kw_ref[...]) + bias_ref[:, nh + sel_cols:nh + sel_cols + win_cols]
        pw = _softmax_lanes(jnp.where(ok_w, s, NEG), ok_w.astype(F32)).astype(BF16)
        o_w = _own_group(lax.dot_general(pw, vw_ref[...], _NT, preferred_element_type=F32))

        gate = gate_ref[0]
        o_ref[0] = gate[:, 0:1] * o_c + gate[:, 1:2] * o_s + gate[:, 2:3] * o_w


def _decode_bias_idx(past_len, wb, ncp, sel_cols, win_cols):
    n = np.arange(ncp)
    cmp_d = past_len - (n * CMP_STRIDE + CMP_BLOCK - 1)
    sel_d = past_len - np.arange(sel_cols)
    win_d = np.concatenate([wb - np.arange(wb), np.zeros(win_cols - wb, np.int64)])
    idx = _t5_bucket_np(np.concatenate([cmp_d, sel_d, win_d]))
    pad = -len(idx) % (8 * LANES)
    return np.concatenate([idx, np.full(pad, N_BUCKETS - 1)]).reshape(-1, LANES).astype(np.int32)


def _nsa_decode(cache_kv, cache_win, page_table, qbd, new_cols, gate3, cw, rel_bias):
    n_pool, page = cache_kv.shape[:2]
    nseq, npages = page_table.shape
    past_len = npages * page
    wb = cache_win.shape[1]
    w1b, pe, w2b = cw
    nh = past_len // CMP_STRIDE
    n_cmp = (past_len + 1 - CMP_BLOCK) // CMP_STRIDE + 1
    sel_cols = past_len + LANES
    win_cols = wb + LANES
    nkv = N_KV * HEAD_DIM
    kvt = cache_kv.transpose(0, 2, 3, 4, 1).reshape(n_pool, 4 * nkv, page)
    wint = cache_win.transpose(0, 2, 3, 4, 1).reshape(nseq, 2 * nkv, wb)
    cov = jnp.asarray(_cover_t_np(LANES, LANES).T)
    idx = _decode_bias_idx(past_len, wb, nh, sel_cols, win_cols)
    bias = _bias_tables(rel_bias, jnp.asarray(idx)).reshape(N_HEADS, idx.size)
    ex = jnp.asarray((np.arange(LANES)[:, None] == np.arange(sel_cols)[None, :] // SEL_BLOCK).astype(np.float32), BF16)
    const = lambda a: pl.BlockSpec(a.shape, lambda b, p, pt: (0,) * a.ndim)
    pps = DECODE_PAGES_PER_STEP
    page_spec = lambda k: pl.BlockSpec((1, 4 * nkv, page), lambda b, p, pt: (pt[b, p * pps + k], 0, 0))
    grid_spec = pltpu.PrefetchScalarGridSpec(
        num_scalar_prefetch=1,
        grid=(nseq, npages // pps),
        in_specs=[page_spec(k) for k in range(pps)] + [
                  pl.BlockSpec((1, 2 * nkv, wb), lambda b, p, pt: (b, 0, 0)),
                  pl.BlockSpec((1, N_HEADS, nkv), lambda b, p, pt: (b, 0, 0)),
                  pl.BlockSpec((1, nkv, LANES), lambda b, p, pt: (b, 0, 0)),
                  pl.BlockSpec((1, N_HEADS, LANES), lambda b, p, pt: (b, 0, 0)),
                  const(w1b), const(pe), const(w2b), const(cov), const(bias), const(ex)],
        out_specs=pl.BlockSpec((1, N_HEADS, HEAD_DIM), lambda b, p, pt: (b, 0, 0)),
        scratch_shapes=[pltpu.VMEM((pps, 2 * nkv // LANES, page, LANES), F32),
                        pltpu.VMEM((2, N_KV * nh, CMP_STRIDE * HEAD_DIM), F32),
                        pltpu.VMEM((nkv, sel_cols), BF16), pltpu.VMEM((nkv, sel_cols), BF16),
                        pltpu.VMEM((nkv, win_cols), BF16), pltpu.VMEM((nkv, win_cols), BF16)])
    out = pl.pallas_call(
        functools.partial(_nsa_decode_kernel, past_len=past_len, n_cmp=n_cmp),
        grid_spec=grid_spec,
        out_shape=jax.ShapeDtypeStruct((nseq, N_HEADS, HEAD_DIM), F32),
        compiler_params=_cp(("arbitrary", "arbitrary"), 56),
        name="nsa_decode",
    )(page_table, *([kvt] * pps), wint, qbd, new_cols, gate3, w1b, pe, w2b, cov, bias, ex)
    return out.reshape(nseq, Q_COLS)


def _decode_operands(qs, kvbs, gates):
    nseq = qs.shape[0]
    nkv = N_KV * HEAD_DIM
    own = (np.arange(N_HEADS)[:, None] // GQ) == (np.arange(nkv)[None, :] // HEAD_DIM)
    qbd = jnp.where(own[None], jnp.tile(qs.reshape(nseq, N_HEADS, HEAD_DIM), (1, 1, N_KV)), 0.0)
    new_cols = jnp.pad(kvbs.astype(F32).reshape(nseq, 4, nkv).transpose(0, 2, 1), ((0, 0), (0, 0), (0, LANES - 4)))
    gate3 = jnp.pad(gates[:, :GATE_COLS].reshape(nseq, 3, N_HEADS).transpose(0, 2, 1),
                    ((0, 0), (0, 0), (0, LANES - 3)))
    return qbd, new_cols, gate3


PROMPT_TM = 512


def kernel(x_prompt, x_sample, state_s5_re, state_s5_im, cache_kv, cache_win, page_table, rel_bias, norm_mix, norm_ffn, norm_final, s5_lam_re, s5_lam_im, s5_log_dt, s5_b_re, s5_b_im, s5_c_re, s5_c_im, s5_d, s5_w_glu, ffn_w_in, ffn_w_out, nsa_w_in, nsa_phi_pe, nsa_phi_w1, nsa_phi_w2, nsa_w_o, moe_router, moe_w_in, moe_w_out):
    batch, length, _ = x_prompt.shape
    nseq = x_sample.shape[0]
    xp = x_prompt.reshape(batch * length, D_MODEL)
    xs = x_sample.reshape(nseq, D_MODEL)
    nstate = N_GROUPS * STATE

    wk, ws, wv, apow, a1 = _s5_prep(s5_lam_re, s5_lam_im, s5_log_dt, s5_b_re, s5_b_im, s5_c_re, s5_c_im)
    w_glu = s5_w_glu.astype(BF16)
    ffn_in, ffn_out = ffn_w_in.astype(BF16), ffn_w_out.astype(BF16)
    nsa_in = jnp.pad(nsa_w_in, ((0, 0), (0, NSA_PAD - nsa_w_in.shape[1]))).astype(BF16)
    w_o = nsa_w_o.astype(BF16)
    router = jnp.pad(moe_router, ((0, 0), (0, LANES - N_EXPERTS))).astype(BF16)
    moe_in, moe_out = moe_w_in.astype(BF16), moe_w_out.astype(BF16)
    cw = _compress_weights(nsa_phi_pe, nsa_phi_w1, nsa_phi_w2)
    tabs = _bias_tables(rel_bias, jnp.asarray(_prompt_bias_idx()))
    tab = tabs[:, :2 * QT]
    gflat = tabs[:, 2 * QT:].reshape(N_HEADS * QT, LANES)
    ghi = gflat.astype(BF16)
    glo = (gflat - ghi.astype(F32)).astype(BF16)

    up = _rmsnorm(xp, norm_mix[0], PROMPT_TM)
    gp, sre_p, sim_p = _s5_prompt(up, s5_d, wk, ws, wv, apow, batch, length)
    y1p, h1p = _glu(gp, xp, w_glu, norm_ffn[0], PROMPT_TM)
    y2p, h2p = _ffn(h1p, y1p, ffn_in, ffn_out, norm_mix[1], PROMPT_TM)
    q, kvf, winf, kvb, gate = _nsa_proj(h2p, nsa_in, PROMPT_TM)
    kcv = _compress_prompt(kvf, cw, batch, length)
    oc, _, selt = _cmp_attn(q, kcv, gate, ghi, glo, batch, length)
    os_ = _selected_t_attn(q, kvb, gate, _selected_tables(tab), selt, batch, length)
    ow = _window_attn(q, kvb, gate, _window_tables(tab), batch, length)
    y3p, h3p, mgp, rtp, cntp = _merge(oc, os_, ow, y2p, w_o, norm_ffn[1], router, MOE_TM)
    counts = cntp[:, 0, :N_EXPERTS].astype(jnp.int32).reshape(-1)
    yp = _moe_routed(h3p, y3p, mgp, rtp, counts, moe_in, moe_out, norm_final)

    us = _rmsnorm(xs, norm_mix[0], nseq)
    gs, sre_s, sim_s = _s5_sample(us, state_s5_re.reshape(nseq, nstate), state_s5_im.reshape(nseq, nstate),
                                  ws, wv, a1, s5_d)
    y1s, h1s = _glu(gs, xs, w_glu, norm_ffn[0], nseq)
    y2s, h2s = _ffn(h1s, y1s, ffn_in, ffn_out, norm_mix[1], nseq)
    qs, kvfs, winfs, kvbs, gates = _nsa_proj(h2s, nsa_in, nseq)
    qbd, new_cols, gate3 = _decode_operands(qs, kvbs, gates)
    od = _nsa_decode(cache_kv, cache_win, page_table, qbd, new_cols, gate3, cw, rel_bias)
    zero = jnp.zeros((nseq, D_MODEL), F32)
    y3s, h3s, mgs, _, _ = _merge(od, zero, zero, y2s, w_o, norm_ffn[1], router, nseq)
    ys = _moe(h3s, y3s, mgs, moe_in, moe_out, norm_final, nseq)

    wp = min(WINDOW, length)
    return (yp.reshape(batch, length, D_MODEL),
            ys.reshape(nseq, 1, D_MODEL),
            sre_p.reshape(batch, N_GROUPS, STATE), sim_p.reshape(batch, N_GROUPS, STATE),
            kvf.reshape(batch, length, 4, N_KV, HEAD_DIM),
            winf.reshape(batch, length, 2, N_KV, HEAD_DIM)[:, length - wp:],
            sre_s.reshape(nseq, N_GROUPS, STATE), sim_s.reshape(nseq, N_GROUPS, STATE),
            kvfs.reshape(nseq, 1, 4, N_KV, HEAD_DIM),
            winfs.reshape(nseq, 1, 2, N_KV, HEAD_DIM))
```

```python
import functools
import math

import numpy as np
import jax
import jax.numpy as jnp
from jax import lax
from jax.experimental import pallas as pl
from jax.experimental.pallas import tpu as pltpu

F32 = jnp.float32
BF16 = jnp.bfloat16
HIGHEST = lax.Precision.HIGHEST

D_MODEL = 1024
GROUP_CH = 16
N_GROUPS = D_MODEL // GROUP_CH
STATE = 64
N_HEADS = 16
HEAD_DIM = 64
N_KV = 4
GQ = N_HEADS // N_KV
CMP_BLOCK = 32
CMP_STRIDE = 16
CMP_HIDDEN = 2 * HEAD_DIM
SEL_BLOCK = 64
TOP_N = 16
WINDOW = 512
Q_COLS = N_HEADS * HEAD_DIM
KV_COLS = 6 * N_KV * HEAD_DIM
GATE_COLS = 3 * N_HEADS
N_BUCKETS = 32
MAX_DISTANCE = 128
D_FF = 2816
N_EXPERTS = 8
EXPERT_FF = 1408
EPS = 1e-6
NEG = -1e30
FORCE = 1e4

LANES = 128
S5_T = 8
S5_SG = LANES // GROUP_CH
S5_NSG = N_GROUPS // S5_SG
S5_SL = S5_SG * STATE
QT = 128


def _cp(sem, vmem_mb):
    return pltpu.CompilerParams(dimension_semantics=sem, vmem_limit_bytes=vmem_mb << 20)


def _cmul(ar, ai, br, bi):
    return ar * br - ai * bi, ar * bi + ai * br


def _s5_prep_kernel(lr_r, li_r, ldt_r, br_r, bi_r, lr_c, li_c, ldt_c, cr_c, ci_c,
                    wk_ref, ws_ref, wv_ref, apow_ref, a1_ref):
    def disc(lr, li, ldt):
        dt = jnp.exp(ldt)
        mag = jnp.exp(lr * dt)
        abr = mag * jnp.cos(li * dt)
        abi = mag * jnp.sin(li * dt)
        den = lr * lr + li * li
        nr = abr - 1.0
        fr = (nr * lr + abi * li) / den
        fi = (abi * lr - nr * li) / den
        return abr, abi, fr, fi

    abr, abi, fr, fi = disc(lr_r[0], li_r[0], ldt_r[0])
    rg = lax.broadcasted_iota(jnp.int32, (LANES, S5_SL), 0) // GROUP_CH
    cg = lax.broadcasted_iota(jnp.int32, (LANES, S5_SL), 1) // STATE
    mask_r = (rg == cg).astype(F32)
    b_r, b_i = br_r[...], bi_r[...]
    bbr = (fr * b_r - fi * b_i) * mask_r
    bbi = (fr * b_i + fi * b_r) * mask_r
    abr_c, abi_c, _, _ = disc(lr_c[0], li_c[0], ldt_c[0])
    rg2 = lax.broadcasted_iota(jnp.int32, (S5_SL, LANES), 0) // STATE
    cg2 = lax.broadcasted_iota(jnp.int32, (S5_SL, LANES), 1) // GROUP_CH
    mask_c = (rg2 == cg2).astype(F32)
    c_r = cr_c[0] * mask_c
    c_i = ci_c[0] * mask_c

    a1_ref[0, :, :S5_SL] = abr
    a1_ref[0, :, S5_SL:] = abi

    pr = jnp.ones_like(abr)
    pi = jnp.zeros_like(abr)
    pr_c = jnp.ones_like(abr_c)
    pi_c = jnp.zeros_like(abr_c)
    for tau in range(S5_T + 1):
        wv_ref[tau, 0, :S5_SL, :] = (c_r * pr_c - c_i * pi_c).astype(BF16)
        wv_ref[tau, 0, S5_SL:, :] = (-(c_r * pi_c + c_i * pr_c)).astype(BF16)
        if tau < S5_T:
            er, ei = _cmul(pr, pi, bbr, bbi)
            s = S5_T - 1 - tau
            ws_ref[s, 0, :, :S5_SL] = er.astype(BF16)
            ws_ref[s, 0, :, S5_SL:] = ei.astype(BF16)
            k = (jnp.dot(er, c_r, precision=HIGHEST, preferred_element_type=F32)
                 - jnp.dot(ei, c_i, precision=HIGHEST, preferred_element_type=F32))
            wk_ref[tau, 0] = k.astype(BF16)
        pr, pi = _cmul(pr, pi, abr, abi)
        pr_c, pi_c = _cmul(pr_c, pi_c, abr_c, abi_c)
    tr, ti = jnp.ones_like(abr), jnp.zeros_like(abr)
    for _ in range(S5_T):
        tr, ti = _cmul(tr, ti, abr, abi)
    qr, qi = jnp.ones_like(abr), jnp.zeros_like(abr)
    for i in range(S5_T + 1):
        apow_ref[0, i:i + 1, :S5_SL] = qr
        apow_ref[0, i:i + 1, S5_SL:] = qi
        qr, qi = _cmul(qr, qi, tr, ti)


def _s5_prep(lam_re, lam_im, log_dt, b_re, b_im, c_re, c_im):
    n = S5_NSG
    row = lambda a: a.reshape(n, 1, S5_SL)
    col = lambda a: a.reshape(n, S5_SL, 1)
    ldt = jnp.repeat(log_dt, STATE)
    bt = lambda b: jnp.tile(b.transpose(0, 2, 1).reshape(D_MODEL, STATE), (1, S5_SG))
    ct = lambda c: jnp.tile(c.reshape(n, LANES, STATE).transpose(0, 2, 1), (1, S5_SG, 1))
    rspec = pl.BlockSpec((1, 1, S5_SL), lambda s: (s, 0, 0))
    cspec = pl.BlockSpec((1, S5_SL, 1), lambda s: (s, 0, 0))
    t1 = S5_T + 1
    return pl.pallas_call(
        _s5_prep_kernel,
        grid=(n,),
        in_specs=[rspec, rspec, rspec,
                  pl.BlockSpec((LANES, S5_SL), lambda s: (s, 0)), pl.BlockSpec((LANES, S5_SL), lambda s: (s, 0)),
                  cspec, cspec, cspec,
                  pl.BlockSpec((1, S5_SL, LANES), lambda s: (s, 0, 0)), pl.BlockSpec((1, S5_SL, LANES), lambda s: (s, 0, 0))],
        out_specs=[pl.BlockSpec((S5_T, 1, LANES, LANES), lambda s: (0, s, 0, 0)),
                   pl.BlockSpec((S5_T, 1, LANES, 2 * S5_SL), lambda s: (0, s, 0, 0)),
                   pl.BlockSpec((t1, 1, 2 * S5_SL, LANES), lambda s: (0, s, 0, 0)),
                   pl.BlockSpec((1, t1 + 7, 2 * S5_SL), lambda s: (s, 0, 0)),
                   pl.BlockSpec((1, 1, 2 * S5_SL), lambda s: (s, 0, 0))],
        out_shape=[jax.ShapeDtypeStruct((S5_T, n, LANES, LANES), BF16),
                   jax.ShapeDtypeStruct((S5_T, n, LANES, 2 * S5_SL), BF16),
                   jax.ShapeDtypeStruct((t1, n, 2 * S5_SL, LANES), BF16),
                   jax.ShapeDtypeStruct((n, t1 + 7, 2 * S5_SL), F32),
                   jax.ShapeDtypeStruct((n, 1, 2 * S5_SL), F32)],
        compiler_params=_cp(("arbitrary",), 48),
        name="s5_prep",
    )(row(lam_re), row(lam_im), row(ldt), bt(b_re), bt(b_im),
      col(lam_re), col(lam_im), col(ldt), ct(c_re), ct(c_im))


def _rms(x, g):
    return x * lax.rsqrt(jnp.mean(x * x, axis=-1, keepdims=True) + EPS) * g


def _rmsnorm_kernel(x_ref, g_ref, o_ref):
    o_ref[...] = _rms(x_ref[...], g_ref[...]).astype(o_ref.dtype)


def _rmsnorm(x, g, tm, dtype=F32):
    n = x.shape[0]
    return pl.pallas_call(
        _rmsnorm_kernel,
        grid=(n // tm,),
        in_specs=[pl.BlockSpec((tm, D_MODEL), lambda i: (i, 0)), pl.BlockSpec((1, D_MODEL), lambda i: (0, 0))],
        out_specs=pl.BlockSpec((tm, D_MODEL), lambda i: (i, 0)),
        out_shape=jax.ShapeDtypeStruct((n, D_MODEL), dtype),
        compiler_params=_cp(("parallel",), 32),
        name="rmsnorm",
    )(x, g.reshape(1, D_MODEL))


def _s5_prompt_kernel(u_ref, d_ref, wk_ref, ws_ref, wv_ref, apow_ref, g_ref, sre_ref, sim_ref,
                      ub_ref, s_ref, p_ref, gc_ref, hs_ref, y_ref):
    length = u_ref.shape[0]
    r = length // S5_T
    ng = r // 8

    acc = jnp.zeros((r, 2 * S5_SL), F32)
    for s in range(S5_T):
        us = u_ref[pl.ds(s, r, stride=S5_T), :].astype(BF16)
        ub_ref[s] = us
        acc = acc + jnp.dot(us, ws_ref[s, 0], preferred_element_type=F32)
    for c in range(2 * S5_SL // LANES):
        s_ref[c] = acc[:, c * LANES:(c + 1) * LANES]

    nsl = S5_SL // LANES
    for c in range(nsl):
        lr = slice(c * LANES, (c + 1) * LANES)
        li = slice(S5_SL + c * LANES, S5_SL + (c + 1) * LANES)

        def apw(i):
            return apow_ref[0, i:i + 1, lr], apow_ref[0, i:i + 1, li]

        a1r, a1i = apw(1)
        pr = s_ref[c, pl.ds(0, ng, stride=8), :]
        pi = s_ref[nsl + c, pl.ds(0, ng, stride=8), :]
        p_ref[0, c] = pr
        p_ref[0, nsl + c] = pi
        for i in range(1, 8):
            qr, qi = _cmul(a1r, a1i, pr, pi)
            pr = qr + s_ref[c, pl.ds(i, ng, stride=8), :]
            pi = qi + s_ref[nsl + c, pl.ds(i, ng, stride=8), :]
            p_ref[i, c] = pr
            p_ref[i, nsl + c] = pi

        a8r, a8i = apw(8)

        def carry(m, g):
            gr, gi = g
            gc_ref[c, pl.ds(m, 1), :] = gr
            gc_ref[nsl + c, pl.ds(m, 1), :] = gi
            nr, ni = _cmul(a8r, a8i, gr, gi)
            return (nr + p_ref[7, c, pl.ds(m, 1), :], ni + p_ref[7, nsl + c, pl.ds(m, 1), :])

        z = jnp.zeros((1, LANES), F32)
        er, ei = lax.fori_loop(0, ng, carry, (z, z))
        sre_ref[0, :, lr] = er
        sim_ref[0, :, lr] = ei

        gr = gc_ref[c]
        gi = gc_ref[nsl + c]
        hs_ref[c, pl.ds(0, ng, stride=8), :] = gr
        hs_ref[nsl + c, pl.ds(0, ng, stride=8), :] = gi
        for i in range(1, 8):
            air, aii = apw(i)
            hr, hi = _cmul(air, aii, gr, gi)
            hs_ref[c, pl.ds(i, ng, stride=8), :] = hr + p_ref[i - 1, c]
            hs_ref[nsl + c, pl.ds(i, ng, stride=8), :] = hi + p_ref[i - 1, nsl + c]

    hs = jnp.concatenate([hs_ref[c] for c in range(2 * S5_SL // LANES)], axis=1).astype(BF16)
    for t in range(S5_T):
        o = jnp.dot(hs, wv_ref[t + 1, 0], preferred_element_type=F32)
        for s in range(t + 1):
            o = o + jnp.dot(ub_ref[s], wk_ref[t - s, 0], preferred_element_type=F32)
        y_ref[pl.ds(t, r, stride=S5_T), :] = o
    y = y_ref[...] + d_ref[...] * u_ref[...]
    g_ref[...] = jax.nn.gelu(y).astype(g_ref.dtype)


def _s5_prompt(u, d_skip, wk, ws, wv, apow, batch, length):
    r = length // S5_T
    t1 = S5_T + 1
    return pl.pallas_call(
        _s5_prompt_kernel,
        grid=(S5_NSG, batch),
        in_specs=[pl.BlockSpec((length, LANES), lambda s, b: (b, s)),
                  pl.BlockSpec((1, LANES), lambda s, b: (0, s)),
                  pl.BlockSpec((S5_T, 1, LANES, LANES), lambda s, b: (0, s, 0, 0)),
                  pl.BlockSpec((S5_T, 1, LANES, 2 * S5_SL), lambda s, b: (0, s, 0, 0)),
                  pl.BlockSpec((t1, 1, 2 * S5_SL, LANES), lambda s, b: (0, s, 0, 0)),
                  pl.BlockSpec((1, t1 + 7, 2 * S5_SL), lambda s, b: (s, 0, 0))],
        out_specs=[pl.BlockSpec((length, LANES), lambda s, b: (b, s)),
                   pl.BlockSpec((1, 1, S5_SL), lambda s, b: (b, 0, s)),
                   pl.BlockSpec((1, 1, S5_SL), lambda s, b: (b, 0, s))],
        out_shape=[jax.ShapeDtypeStruct((batch * length, D_MODEL), BF16),
                   jax.ShapeDtypeStruct((batch, 1, N_GROUPS * STATE), F32),
                   jax.ShapeDtypeStruct((batch, 1, N_GROUPS * STATE), F32)],
        scratch_shapes=[pltpu.VMEM((S5_T, r, LANES), BF16),
                        pltpu.VMEM((2 * S5_SL // LANES, r, LANES), F32),
                        pltpu.VMEM((8, 2 * S5_SL // LANES, r // 8, LANES), F32),
                        pltpu.VMEM((2 * S5_SL // LANES, r // 8, LANES), F32),
                        pltpu.VMEM((2 * S5_SL // LANES, r, LANES), F32),
                        pltpu.VMEM((length, LANES), F32)],
        compiler_params=_cp(("arbitrary", "arbitrary"), 56),
        name="s5_prompt",
    )(u, d_skip.reshape(1, D_MODEL), wk, ws, wv, apow)


def _resident(shape):
    nd = len(shape)
    return pl.BlockSpec(shape, lambda *_: (0,) * nd, pipeline_mode=pl.Buffered(1))


def _rows(tm, width):
    return pl.BlockSpec((tm, width), lambda i: (i, 0))


def _mm(a, b):
    return jnp.dot(a, b, preferred_element_type=F32)


def _glu_kernel(g_ref, x_ref, w_ref, gain_ref, y_ref, h_ref):
    p = _mm(g_ref[...], w_ref[...])
    y = x_ref[...] + p[:, :D_MODEL] * jax.nn.sigmoid(p[:, D_MODEL:])
    y_ref[...] = y
    h_ref[...] = _rms(y, gain_ref[...]).astype(h_ref.dtype)


def _glu(g, x, w_glu, gain, tm):
    n = g.shape[0]
    return pl.pallas_call(
        _glu_kernel,
        grid=(n // tm,),
        in_specs=[_rows(tm, D_MODEL), _rows(tm, D_MODEL), _resident((D_MODEL, 2 * D_MODEL)), _resident((1, D_MODEL))],
        out_specs=[_rows(tm, D_MODEL), _rows(tm, D_MODEL)],
        out_shape=[jax.ShapeDtypeStruct((n, D_MODEL), F32), jax.ShapeDtypeStruct((n, D_MODEL), g.dtype)],
        compiler_params=_cp(("parallel",), 48),
        name="s5_glu",
    )(g, x, w_glu, gain.reshape(1, D_MODEL))


def _chunks(total, size):
    return [(s, min(size, total - s)) for s in range(0, total, size)]


def _swiglu_acc(h, win_ref, wout_ref, ff, lead=()):
    acc = None
    for s, w in _chunks(ff, 256):
        a = jnp.dot(h, win_ref[lead + (slice(None), slice(s, s + w))], preferred_element_type=F32)
        b = jnp.dot(h, win_ref[lead + (slice(None), slice(ff + s, ff + s + w))], preferred_element_type=F32)
        act = (jax.nn.silu(a) * b).astype(BF16)
        z = jnp.dot(act, wout_ref[lead + (slice(s, s + w), slice(None))], preferred_element_type=F32)
        acc = z if acc is None else acc + z
    return acc


def _ffn_kernel(h_ref, y_ref, win_ref, wout_ref, gain_ref, yo_ref, ho_ref):
    y = y_ref[...] + _swiglu_acc(h_ref[...], win_ref, wout_ref, D_FF)
    yo_ref[...] = y
    ho_ref[...] = _rms(y, gain_ref[...]).astype(ho_ref.dtype)


def _ffn(h, y, w_in, w_out, gain, tm):
    n = h.shape[0]
    return pl.pallas_call(
        _ffn_kernel,
        grid=(n // tm,),
        in_specs=[_rows(tm, D_MODEL), _rows(tm, D_MODEL), _resident((D_MODEL, 2 * D_FF)), _resident((D_FF, D_MODEL)),
                  _resident((1, D_MODEL))],
        out_specs=[_rows(tm, D_MODEL), _rows(tm, D_MODEL)],
        out_shape=[jax.ShapeDtypeStruct((n, D_MODEL), F32), jax.ShapeDtypeStruct((n, D_MODEL), BF16)],
        compiler_params=_cp(("parallel",), 56),
        name="ffn_swiglu",
    )(h, y, w_in, w_out, gain.reshape(1, D_MODEL))


NSA_PAD = 2688
KV_OFF = Q_COLS


def _nsa_proj_kernel(h_ref, w_ref, q_ref, kvf_ref, winf_ref, kvb_ref, gate_ref):
    p = _mm(h_ref[...], w_ref[...])
    q_ref[...] = (p[:, :Q_COLS] * (HEAD_DIM ** -0.5)).astype(q_ref.dtype)
    kvf_ref[...] = p[:, KV_OFF:KV_OFF + 1024]
    winf_ref[...] = p[:, KV_OFF + 1024:KV_OFF + 1536]
    kvb_ref[...] = p[:, KV_OFF + 512:KV_OFF + 1536].astype(kvb_ref.dtype)
    gate_ref[...] = jax.nn.sigmoid(p[:, KV_OFF + 1536:])


def _nsa_proj(h, w_in_pad, tm):
    n = h.shape[0]
    return pl.pallas_call(
        _nsa_proj_kernel,
        grid=(n // tm,),
        in_specs=[_rows(tm, D_MODEL), _resident((D_MODEL, NSA_PAD))],
        out_specs=[_rows(tm, Q_COLS), _rows(tm, 1024), _rows(tm, 512), _rows(tm, 1024), _rows(tm, LANES)],
        out_shape=[jax.ShapeDtypeStruct((n, Q_COLS), h.dtype),
                   jax.ShapeDtypeStruct((n, 1024), F32),
                   jax.ShapeDtypeStruct((n, 512), F32),
                   jax.ShapeDtypeStruct((n, 1024), h.dtype),
                   jax.ShapeDtypeStruct((n, LANES), F32)],
        compiler_params=_cp(("parallel",), 48),
        name="nsa_proj",
    )(h, w_in_pad)


def _merge_kernel(oc_ref, os_ref, ow_ref, y_ref, wo_ref, gain_ref, router_ref, yo_ref, ho_ref, mg_ref, rt_ref,
                  cnt_ref):
    o = (oc_ref[...].astype(F32) + os_ref[...].astype(F32) + ow_ref[...].astype(F32)).astype(wo_ref.dtype)
    y = y_ref[...] + _mm(o, wo_ref[...])
    yo_ref[...] = y
    hf = _rms(y, gain_ref[...])
    ho_ref[...] = hf.astype(ho_ref.dtype)
    logits = _mm(hf.astype(BF16), router_ref[...])
    lane = lax.broadcasted_iota(jnp.int32, logits.shape, 1)
    logits = jnp.where(lane < N_EXPERTS, logits, -jnp.inf)
    m1 = jnp.max(logits, axis=-1, keepdims=True)
    i1 = jnp.min(jnp.where(logits == m1, lane, LANES), axis=-1, keepdims=True)
    rest = jnp.where(lane == i1, -jnp.inf, logits)
    m2 = jnp.max(rest, axis=-1, keepdims=True)
    i2 = jnp.min(jnp.where(rest == m2, lane, LANES), axis=-1, keepdims=True)
    e2 = jnp.exp(m2 - m1)
    den = 1.0 + e2
    mg_ref[...] = jnp.where(lane == i1, 1.0 / den, 0.0) + jnp.where(lane == i2, e2 / den, 0.0)
    routed = jnp.where((lane == i1) | (lane == i2), 1.0, 0.0)
    rt_ref[...] = routed
    cnt_ref[0] = jnp.broadcast_to(jnp.sum(routed, axis=0, keepdims=True), cnt_ref.shape[1:])


def _merge(oc, os_, ow, y, w_o, gain, router_pad, tm):
    n = y.shape[0]
    return pl.pallas_call(
        _merge_kernel,
        grid=(n // tm,),
        in_specs=[_rows(tm, D_MODEL)] * 4 + [_resident((D_MODEL, D_MODEL)), _resident((1, D_MODEL)),
                                             _resident((D_MODEL, LANES))],
        out_specs=[_rows(tm, D_MODEL), _rows(tm, D_MODEL), _rows(tm, LANES), _rows(tm, LANES),
                   pl.BlockSpec((1, 8, LANES), lambda i: (i, 0, 0))],
        out_shape=[jax.ShapeDtypeStruct((n, D_MODEL), F32), jax.ShapeDtypeStruct((n, D_MODEL), BF16),
                   jax.ShapeDtypeStruct((n, LANES), F32), jax.ShapeDtypeStruct((n, LANES), F32),
                   jax.ShapeDtypeStruct((n // tm, 8, LANES), F32)],
        compiler_params=_cp(("parallel",), 56),
        name="nsa_merge_router",
    )(oc, os_, ow, y, w_o, gain.reshape(1, D_MODEL), router_pad)


def _moe_kernel(h_ref, y_ref, mg_ref, win_ref, wout_ref, gain_ref, o_ref, acc_ref):
    e = pl.program_id(1)

    @pl.when(e == 0)
    def _():
        acc_ref[...] = jnp.zeros_like(acc_ref)

    mg = mg_ref[...]
    lane = lax.broadcasted_iota(jnp.int32, mg.shape, 1)
    gcol = jnp.sum(jnp.where(lane == e, mg, 0.0), axis=-1, keepdims=True)
    acc_ref[...] += gcol * _swiglu_acc(h_ref[...], win_ref, wout_ref, EXPERT_FF, lead=(0,))

    @pl.when(e == N_EXPERTS - 1)
    def _():
        o_ref[...] = _rms(y_ref[...] + acc_ref[...], gain_ref[...])


def _moe(h, y, mg, w_in, w_out, gain, tm):
    n = h.shape[0]
    rows = lambda w: pl.BlockSpec((tm, w), lambda i, e: (i, 0))
    return pl.pallas_call(
        _moe_kernel,
        grid=(n // tm, N_EXPERTS),
        in_specs=[rows(D_MODEL), rows(D_MODEL), rows(LANES),
                  pl.BlockSpec((1, D_MODEL, 2 * EXPERT_FF), lambda i, e: (e, 0, 0)),
                  pl.BlockSpec((1, EXPERT_FF, D_MODEL), lambda i, e: (e, 0, 0)),
                  pl.BlockSpec((1, D_MODEL), lambda i, e: (0, 0))],
        out_specs=rows(D_MODEL),
        out_shape=jax.ShapeDtypeStruct((n, D_MODEL), F32),
        scratch_shapes=[pltpu.VMEM((tm, D_MODEL), F32)],
        compiler_params=_cp(("parallel", "arbitrary"), 56),
        name="moe_final_norm",
    )(h, y, mg, w_in, w_out, gain.reshape(1, D_MODEL))


MOE_TM = 1024


def _moe_passes(tm):
    first = tm // 4
    return [first] + [tm // 8] * ((tm - first) // (tm // 8))


def _moe_routed_kernel(cnt_ref, h_ref, y_ref, mg_ref, rt_ref, lower_ref, win_ref, wout_ref, gain_ref, o_ref,
                       acc_ref, rcol_ref, rrow_ref, rtrow_ref):
    t = pl.program_id(0)
    e = pl.program_id(1)
    tm = h_ref.shape[0]

    @pl.when(e == 0)
    def _():
        acc_ref[...] = jnp.zeros_like(acc_ref)
        rt = rt_ref[...]
        rank = _mm(lower_ref[...], rt.astype(BF16))
        rcol_ref[...] = rank
        rrow_ref[...] = rank.T
        rtrow_ref[...] = rt.T

    n = cnt_ref[t * N_EXPERTS + e]
    lane = lax.broadcasted_iota(jnp.int32, (tm, LANES), 1)
    col = lambda ref: jnp.sum(jnp.where(lane == e, ref[...], 0.0), axis=-1, keepdims=True)
    gcol, rank_col, rt_col = col(mg_ref), col(rcol_ref), col(rt_ref)
    rank_row = rrow_ref[pl.ds(e, 1), :]
    rt_row = rtrow_ref[pl.ds(e, 1), :]
    start = 0
    for rb in _moe_passes(tm):
        @pl.when(start < n)
        def _():
            ridx = (start + lax.broadcasted_iota(jnp.int32, (rb, tm), 0)).astype(F32)
            take = jnp.where((rank_row == ridx) & (rt_row > 0.5), 1.0, 0.0).astype(BF16)
            x = _mm(take, h_ref[...]).astype(BF16)
            y = _swiglu_acc(x, win_ref, wout_ref, EXPERT_FF, lead=(0,))
            cidx = (start + lax.broadcasted_iota(jnp.int32, (tm, rb), 1)).astype(F32)
            put = jnp.where((rank_col == cidx) & (rt_col > 0.5), 1.0, 0.0).astype(BF16)
            acc_ref[...] += gcol * _mm(put, y.astype(BF16))
        start += rb

    @pl.when(e == N_EXPERTS - 1)
    def _():
        o_ref[...] = _rms(y_ref[...] + acc_ref[...], gain_ref[...])


def _moe_routed(h, y, mg, rt, cnt, w_in, w_out, gain):
    n = h.shape[0]
    tm = MOE_TM
    lower = jnp.asarray(np.tril(np.ones((tm, tm), np.float32), -1), BF16)
    rows = lambda w, **kw: pl.BlockSpec((tm, w), lambda i, e, c: (i, 0), **kw)
    grid_spec = pltpu.PrefetchScalarGridSpec(
        num_scalar_prefetch=1,
        grid=(n // tm, N_EXPERTS),
        in_specs=[rows(D_MODEL), rows(D_MODEL, pipeline_mode=pl.Buffered(1)), rows(LANES), rows(LANES),
                  pl.BlockSpec((tm, tm), lambda i, e, c: (0, 0), pipeline_mode=pl.Buffered(1)),
                  pl.BlockSpec((1, D_MODEL, 2 * EXPERT_FF), lambda i, e, c: (e, 0, 0)),
                  pl.BlockSpec((1, EXPERT_FF, D_MODEL), lambda i, e, c: (e, 0, 0)),
                  pl.BlockSpec((1, D_MODEL), lambda i, e, c: (0, 0))],
        out_specs=rows(D_MODEL),
        scratch_shapes=[pltpu.VMEM((tm, D_MODEL), F32),
                        pltpu.VMEM((tm, LANES), F32),
                        pltpu.VMEM((LANES, tm), F32),
                        pltpu.VMEM((LANES, tm), F32)])
    return pl.pallas_call(
        _moe_routed_kernel,
        grid_spec=grid_spec,
        out_shape=jax.ShapeDtypeStruct((n, D_MODEL), F32),
        compiler_params=_cp(("arbitrary", "arbitrary"), 58),
        name="moe_routed_final_norm",
    )(cnt, h, y, mg, rt, lower, w_in, w_out, gain.reshape(1, D_MODEL))


def _t5_bucket_np(dist):
    n = np.maximum(np.asarray(dist, np.int64), 0)
    exact = N_BUCKETS // 2
    logpart = exact + np.floor(np.log(np.maximum(n, 1) / exact) / math.log(MAX_DISTANCE / exact)
                               * (N_BUCKETS - exact)).astype(np.int64)
    return np.where(n < exact, n, np.minimum(logpart, N_BUCKETS - 1)).astype(np.int32)


def _bias_kernel(rb_ref, idx_ref, o_ref):
    h = pl.program_id(0)
    idx = idx_ref[...]
    far = rb_ref[N_BUCKETS - 1, h]
    acc = jnp.zeros(idx.shape, F32)
    for b in range(N_BUCKETS - 1):
        acc = jnp.where(idx == b, rb_ref[b, h] - far, acc)
    o_ref[0] = acc


def _bias_tables(rel_bias, idx):
    m = idx.shape[0]
    return pl.pallas_call(
        _bias_kernel,
        grid=(N_HEADS,),
        in_specs=[pl.BlockSpec(memory_space=pltpu.SMEM), pl.BlockSpec((m, LANES), lambda h: (0, 0))],
        out_specs=pl.BlockSpec((1, m, LANES), lambda h: (h, 0, 0)),
        out_shape=jax.ShapeDtypeStruct((N_HEADS, m, LANES), F32),
        compiler_params=_cp(("arbitrary",), 32),
        name="t5_bias_tables",
    )(rel_bias, idx)


CMP_NEAR = 32


def _prompt_bias_idx():
    a = np.arange(QT)[:, None]
    b = np.arange(QT)[None, :]
    t0 = _t5_bucket_np(a - b)
    t1 = _t5_bucket_np(QT + a - b)
    m = np.arange(LANES)[None, :]
    gc = np.where(m < CMP_NEAR, _t5_bucket_np(a + (CMP_NEAR // 2) * CMP_STRIDE - (CMP_BLOCK - 1) - CMP_STRIDE * m),
                  N_BUCKETS - 1)
    return np.concatenate([t0, t1, gc], axis=0).astype(np.int32)


def _flatten_pairs(load, nh):
    lane = lax.broadcasted_iota(jnp.int32, (nh, LANES), 1)
    ev, od = [], []
    for m in range(CMP_STRIDE // 2):
        sa = load(2 * m)
        sb = load(2 * m + 1)
        ev.append(jnp.where(lane < HEAD_DIM, sa, pltpu.roll(sb, HEAD_DIM, 1)))
        od.append(jnp.where(lane < HEAD_DIM, pltpu.roll(sa, HEAD_DIM, 1), sb))
    return jnp.concatenate(ev, axis=1), jnp.concatenate(od, axis=1)


def _compress_rows(xf, w1_ref, pe_ref, w2_ref, slot):
    rows = xf.shape[0]
    half = CMP_STRIDE * HEAD_DIM
    first = _mm((xf + pe_ref[slot, 0:1, :half]).astype(BF16), w1_ref[slot, :, :CMP_HIDDEN])
    second = _mm((xf + pe_ref[slot, 0:1, half:]).astype(BF16), w1_ref[slot, :, CMP_HIDDEN:])
    act = jax.nn.gelu(first + pltpu.roll(second, rows - 1, 0))
    return _mm(act.astype(BF16), w2_ref[slot])


def _compress_prompt_kernel(x0, x1, x2, x3, w1_ref, pe_ref, w2_ref, o_ref):
    nh = x0.shape[0] // CMP_STRIDE
    xs = (x0, x1, x2, x3)
    for slot in range(2):
        parts = []
        for c in range(2):
            x_ref = xs[2 * slot + c]
            ev, od = _flatten_pairs(lambda s: x_ref[pl.ds(s, nh, stride=CMP_STRIDE), :], nh)
            parts += [ev, od]
        xf = jnp.concatenate(parts, axis=0)
        o_ref[0, slot] = _compress_rows(xf, w1_ref, pe_ref, w2_ref, slot).astype(o_ref.dtype)


def _compress_weights(phi_pe, phi_w1, phi_w2):
    half = CMP_BLOCK // 2
    w1 = jnp.concatenate([phi_w1[:, :half].reshape(2, half * HEAD_DIM, CMP_HIDDEN),
                          phi_w1[:, half:].reshape(2, half * HEAD_DIM, CMP_HIDDEN)], axis=2)
    pe = jnp.broadcast_to(phi_pe.reshape(2, 1, CMP_BLOCK * HEAD_DIM), (2, 8, CMP_BLOCK * HEAD_DIM))
    return w1.astype(BF16), pe, phi_w2.astype(BF16)


def _compress_prompt(kvf, cw, batch, length):
    w1b, pe, w2b = cw
    nh = length // CMP_STRIDE
    xspec = lambda c: pl.BlockSpec((length, LANES), lambda b: (b, c))
    return pl.pallas_call(
        _compress_prompt_kernel,
        grid=(batch,),
        in_specs=[xspec(0), xspec(1), xspec(2), xspec(3), _resident(w1b.shape), _resident(pe.shape),
                  _resident(w2b.shape)],
        out_specs=pl.BlockSpec((1, 2, N_KV * nh, HEAD_DIM), lambda b: (b, 0, 0, 0)),
        out_shape=jax.ShapeDtypeStruct((batch, 2, N_KV * nh, HEAD_DIM), BF16),
        compiler_params=_cp(("parallel",), 48),
        name="nsa_compress_prompt",
    )(kvf, kvf, kvf, kvf, w1b, pe, w2b)


_NT = (((1,), (1,)), ((), ()))


def _cover_t_np(nc_pad, nb_pad):
    ratio = SEL_BLOCK // CMP_STRIDE
    span = CMP_BLOCK // CMP_STRIDE
    off = (np.arange(ratio)[:, None] - np.arange(span)[None, :]).reshape(-1)
    target = np.arange(nb_pad)[:, None] * ratio + off[None, :]
    cov = np.sum(np.arange(nc_pad)[:, None, None] == target[None], axis=-1)
    return np.ascontiguousarray(cov.T).astype(np.float32)


def _rank_select(score, jidx, nsel):
    nb = score.shape[0]
    cnt = jnp.zeros(score.shape, F32)
    for i in range(nb):
        ri = score[i:i + 1, :]
        cnt = cnt + jnp.where((ri > score) | ((ri == score) & (jidx > i)), 1.0, 0.0)
    return cnt < nsel


def _cmp_attn_kernel(q_ref, kcv_ref, gate_ref, ghi_ref, glo_ref, covt_ref, o_ref, selt_ref, bias_ref):
    i = pl.program_id(1)
    q0 = i * QT
    nc = kcv_ref.shape[2] // N_KV
    nb = covt_ref.shape[0]
    base = q0 // CMP_STRIDE - CMP_NEAR // 2
    rr = lax.broadcasted_iota(jnp.int32, (LANES, nc), 0)
    cc = lax.broadcasted_iota(jnp.int32, (LANES, nc), 1)
    place = ((cc == base + rr) & (rr < CMP_NEAR)).astype(BF16)
    bias_ref[...] = (jnp.dot(ghi_ref[...], place, preferred_element_type=F32)
                     + jnp.dot(glo_ref[...], place, preferred_element_type=F32))
    t = q0 + lax.broadcasted_iota(jnp.int32, (QT, nc), 0)
    cend = lax.broadcasted_iota(jnp.int32, (QT, nc), 1) * CMP_STRIDE + (CMP_BLOCK - 1)
    vis = t >= cend
    visf = vis.astype(F32)
    madd = jnp.where(vis, 0.0, NEG)
    gate = gate_ref[...]
    jidx = lax.broadcasted_iota(jnp.int32, (nb, QT), 0)
    tq = q0 + lax.broadcasted_iota(jnp.int32, (nb, QT), 1)
    jt = tq // SEL_BLOCK
    forced = (jidx == 0) | (jidx == jt) | (jidx == jt - 1)
    visb = jidx * SEL_BLOCK <= tq
    for g in range(N_KV):
        kc = kcv_ref[0, 0, g * nc:(g + 1) * nc, :]
        vc = kcv_ref[0, 1, g * nc:(g + 1) * nc, :]
        psum = jnp.zeros((QT, nc), F32)
        for r in range(GQ):
            h = g * GQ + r
            qh = q_ref[:, h * HEAD_DIM:(h + 1) * HEAD_DIM]
            s = lax.dot_general(qh, kc, _NT, preferred_element_type=F32)
            s = s + bias_ref[h * QT:(h + 1) * QT, :] + madd
            m = jnp.max(s, axis=-1, keepdims=True)
            e = jnp.exp(s - m)
            p = ((e * visf) * (1.0 / jnp.sum(e, axis=-1, keepdims=True))).astype(BF16)
            psum = psum + p.astype(F32)
            o = _mm(p, vc)
            o_ref[:, h * HEAD_DIM:(h + 1) * HEAD_DIM] = (o * gate[:, h:h + 1]).astype(o_ref.dtype)
        imp = lax.dot_general(covt_ref[...], psum, _NT, precision=HIGHEST, preferred_element_type=F32)
        score = jnp.where(visb, jnp.where(forced, FORCE, imp), NEG)
        sel = _rank_select(score, jidx, min(TOP_N, nb)) & visb
        selt_ref[0, g * nb:(g + 1) * nb, :] = sel.astype(F32)


def _cmp_attn(q, kcv, gate, ghi, glo, batch, length):
    nq = length // QT
    nc = kcv.shape[2] // N_KV
    nb = length // SEL_BLOCK
    covt = jnp.asarray(_cover_t_np(nc, nb))
    row = lambda w: pl.BlockSpec((QT, w), lambda b, i: (b * nq + i, 0))
    return pl.pallas_call(
        _cmp_attn_kernel,
        grid=(batch, nq),
        in_specs=[row(Q_COLS), pl.BlockSpec((1, 2, N_KV * nc, HEAD_DIM), lambda b, i: (b, 0, 0, 0)), row(LANES),
                  _resident(ghi.shape), _resident(glo.shape), _resident(covt.shape)],
        out_specs=[row(Q_COLS), pl.BlockSpec((1, N_KV * nb, QT), lambda b, i: (b * nq + i, 0, 0))],
        out_shape=[jax.ShapeDtypeStruct((batch * length, Q_COLS), BF16),
                   jax.ShapeDtypeStruct((batch * nq, N_KV * nb, QT), F32)],
        scratch_shapes=[pltpu.VMEM((N_HEADS * QT, nc), F32)],
        compiler_params=_cp(("parallel", "arbitrary"), 48),
        name="nsa_cmp_attn_select",
    )(q, kcv, gate, ghi, glo, covt)


SEL_CHUNK = 4
SEL_TAIL = SEL_CHUNK + 1
SEL_ZERO, SEL_MASKED = 2, 3


_TN = (((0,), (0,)), ((), ()))


def _fold(x, op):
    return op(x.reshape(x.shape[0] // 8, 8, x.shape[1]), axis=0)


def _selected_t_kernel(q_ref, kv_ref, gate_ref, stab_ref, selt_ref, ext_ref, o_ref, *scratch):
    i = pl.program_id(1)
    length = kv_ref.shape[0]
    nb = ext_ref.shape[1]
    wide = SEL_CHUNK * QT
    tail = SEL_TAIL * QT
    nbulk = jnp.maximum(i - 1, 0) // SEL_CHUNK
    tstart = jnp.minimum(SEL_CHUNK * nbulk, length // QT - SEL_TAIL)
    trows = pl.ds(pl.multiple_of(tstart * QT, QT), tail)
    rep = lambda a: jnp.concatenate([a] * GQ, axis=1)
    gate_t = gate_ref[...].T
    kinds = []
    for jj in range(SEL_TAIL):
        tile = tstart + jj
        kinds.append(jnp.where((tile < SEL_CHUNK * nbulk) | (tile > i), SEL_MASKED, jnp.minimum(i - tile, SEL_ZERO)))

    kcol = lambda g: slice(g * HEAD_DIM, (g + 1) * HEAD_DIM)
    vcol = lambda g: slice((N_KV + g) * HEAD_DIM, (N_KV + g + 1) * HEAD_DIM)
    q4 = lambda g: jnp.concatenate(
        [q_ref[:, (g * GQ + r) * HEAD_DIM:(g * GQ + r + 1) * HEAD_DIM] for r in range(GQ)], axis=0)

    per_set = len(scratch) // 2
    sets = (scratch[:per_set], scratch[per_set:])

    def prep(g):
        s_ref, mx_ref, mb_ref, ls_ref, acc_ref, madd_ref = sets[g % 2]
        selt = selt_ref[0, g * nb:(g + 1) * nb, :].astype(BF16)
        madd_ref[...] = (jnp.dot(ext_ref[...], selt, preferred_element_type=F32) - 1.0) * (-NEG)
        mx_ref[...] = jnp.full(mx_ref.shape, -jnp.inf, F32)
        ls_ref[...] = jnp.zeros(ls_ref.shape, F32)
        acc_ref[...] = jnp.zeros(acc_ref.shape, F32)

    def logits_chunk(g, c):
        s_ref, mx_ref, mb_ref, ls_ref, acc_ref, madd_ref = sets[g % 2]
        rows = pl.ds(pl.multiple_of(c * wide, wide), wide)
        st = (lax.dot_general(kv_ref[rows, kcol(g)], q4(g), _NT, preferred_element_type=F32)
              + rep(madd_ref[rows, :]))
        s_ref[rows, :] = st
        mx_ref[...] = jnp.maximum(mx_ref[...], _fold(st, jnp.max))

    def logits_tail(g):
        s_ref, mx_ref, mb_ref, ls_ref, acc_ref, madd_ref = sets[g % 2]
        st = lax.dot_general(kv_ref[trows, kcol(g)], q4(g), _NT, preferred_element_type=F32)
        adds = [jnp.concatenate([stab_ref[g * GQ + r, pl.ds(pl.multiple_of(kinds[jj] * QT, QT), QT), :]
                                 for r in range(GQ)], axis=1) for jj in range(SEL_TAIL)]
        st = st + jnp.concatenate(adds, axis=0) + rep(madd_ref[trows, :])
        s_ref[length:, :] = st
        mx = jnp.maximum(mx_ref[...], _fold(st, jnp.max))
        mb_ref[...] = jnp.broadcast_to(jnp.max(mx, axis=0, keepdims=True), mx.shape)

    def probs(st, m8):
        return jnp.exp(st.reshape(st.shape[0] // 8, 8, st.shape[1]) - m8[None]).reshape(st.shape)

    def pv_chunk(g, c):
        s_ref, mx_ref, mb_ref, ls_ref, acc_ref, madd_ref = sets[g % 2]
        rows = pl.ds(pl.multiple_of(c * wide, wide), wide)
        p = probs(s_ref[rows, :], mb_ref[...])
        ls_ref[...] += _fold(p, jnp.sum)
        acc_ref[...] += lax.dot_general(kv_ref[rows, vcol(g)], p.astype(BF16), _TN, preferred_element_type=F32)

    def pv_tail(g):
        s_ref, mx_ref, mb_ref, ls_ref, acc_ref, madd_ref = sets[g % 2]
        p = probs(s_ref[length:, :], mb_ref[...])
        l = jnp.sum(ls_ref[...] + _fold(p, jnp.sum), axis=0, keepdims=True)
        acc = acc_ref[...] + lax.dot_general(kv_ref[trows, vcol(g)], p.astype(BF16), _TN,
                                             preferred_element_type=F32)
        first = N_HEADS + g * GQ
        gates = jnp.concatenate([gate_t[first + r:first + r + 1, :] for r in range(GQ)], axis=1)
        o4 = (acc * (gates / l)).T
        for r in range(GQ):
            h = g * GQ + r
            o_ref[:, h * HEAD_DIM:(h + 1) * HEAD_DIM] = o4[r * QT:(r + 1) * QT, :].astype(o_ref.dtype)

    def loop(*stages):
        def body(c, carry):
            for stage, g in stages:
                stage(g, c)
            return carry
        lax.fori_loop(0, nbulk, body, 0)

    prep(0)
    loop((logits_chunk, 0))
    logits_tail(0)
    for g in range(1, N_KV):
        prep(g)
        loop((logits_chunk, g), (pv_chunk, g - 1))
        pv_tail(g - 1)
        logits_tail(g)
    loop((pv_chunk, N_KV - 1))
    pv_tail(N_KV - 1)


def _selected_t_attn(q, kvb, gate, stab, selt, batch, length):
    nq = length // QT
    nb = length // SEL_BLOCK
    assert nq >= SEL_TAIL
    row = lambda w: pl.BlockSpec((QT, w), lambda b, i: (b * nq + i, 0))
    ext = jnp.asarray(_ex_np(nb, length).T, BF16)
    stab_t = stab.reshape(N_HEADS, QT, -1, QT).transpose(0, 2, 3, 1).reshape(N_HEADS, -1, QT)
    lanes4 = GQ * QT
    return pl.pallas_call(
        _selected_t_kernel,
        grid=(batch, nq),
        in_specs=[row(Q_COLS), pl.BlockSpec((length, 2 * N_KV * HEAD_DIM), lambda b, i: (b, 0)), row(LANES),
                  _resident(stab_t.shape), pl.BlockSpec((1, N_KV * nb, QT), lambda b, i: (b * nq + i, 0, 0)),
                  _resident(ext.shape)],
        out_specs=row(Q_COLS),
        out_shape=jax.ShapeDtypeStruct((batch * length, Q_COLS), BF16),
        scratch_shapes=2 * [pltpu.VMEM((length + SEL_TAIL * QT, lanes4), F32),
                            pltpu.VMEM((8, lanes4), F32),
                            pltpu.VMEM((8, lanes4), F32),
                            pltpu.VMEM((8, lanes4), F32),
                            pltpu.VMEM((HEAD_DIM, lanes4), F32),
                            pltpu.VMEM((length, QT), F32)],
        compiler_params=_cp(("parallel", "arbitrary"), 56),
        name="nsa_selected_attn",
    )(q, kvb, gate, stab_t, selt, ext)


WIN_TILES = WINDOW // QT + 1


def _window_kernel(q_ref, kv_ref, gate_ref, wtab_ref, o_ref):
    i = pl.program_id(1)
    first = jnp.maximum(i - (WIN_TILES - 1), 0)
    rows = pl.ds(pl.multiple_of(first * QT, QT), WIN_TILES * QT)
    gate_t = gate_ref[...].T
    for g in range(N_KV):
        kcol = slice(g * HEAD_DIM, (g + 1) * HEAD_DIM)
        vcol = slice(N_KV * HEAD_DIM + g * HEAD_DIM, N_KV * HEAD_DIM + (g + 1) * HEAD_DIM)
        q4 = jnp.concatenate([q_ref[:, (g * GQ + r) * HEAD_DIM:(g * GQ + r + 1) * HEAD_DIM] for r in range(GQ)],
                             axis=0)
        st = lax.dot_general(kv_ref[rows, kcol], q4, _NT, preferred_element_type=F32)
        adds = []
        for jj in range(WIN_TILES):
            d = i - first - jj
            kind = jnp.where(d < 0, WIN_TILES, d)
            krows = pl.ds(pl.multiple_of(kind * QT, QT), QT)
            adds.append(jnp.concatenate([wtab_ref[g * GQ + r, krows, :] for r in range(GQ)], axis=1))
        st = st + jnp.concatenate(adds, axis=0)
        m = jnp.max(_fold(st, jnp.max), axis=0, keepdims=True)
        m8 = jnp.broadcast_to(m, (8, m.shape[1]))
        p = jnp.exp(st.reshape(st.shape[0] // 8, 8, st.shape[1]) - m8[None]).reshape(st.shape)
        l = jnp.sum(_fold(p, jnp.sum), axis=0, keepdims=True)
        acc = lax.dot_general(kv_ref[rows, vcol], p.astype(BF16), _TN, preferred_element_type=F32)
        first_gate = 2 * N_HEADS + g * GQ
        gates = jnp.concatenate([gate_t[first_gate + r:first_gate + r + 1, :] for r in range(GQ)], axis=1)
        o4 = (acc * (gates / l)).T
        for r in range(GQ):
            h = g * GQ + r
            o_ref[:, h * HEAD_DIM:(h + 1) * HEAD_DIM] = o4[r * QT:(r + 1) * QT, :].astype(o_ref.dtype)


def _window_tables(tab):
    a = np.arange(QT)[:, None]
    b = np.arange(QT)[None, :]
    causal = np.where(a >= b, 0.0, NEG).astype(np.float32)
    edge = np.where(a <= b, 0.0, NEG).astype(np.float32)
    const = lambda m: jnp.broadcast_to(jnp.asarray(m, F32), (N_HEADS, QT, QT))
    zero = np.zeros((QT, QT), np.float32)
    return jnp.concatenate([tab[:, :QT] + causal, tab[:, QT:], const(zero), const(zero), const(edge),
                            const(np.full((QT, QT), NEG, np.float32))], axis=2)


def _window_attn(q, kvb, gate, wtab, batch, length):
    nq = length // QT
    row = lambda w: pl.BlockSpec((QT, w), lambda b, i: (b * nq + i, 0))
    wtab = wtab.reshape(N_HEADS, QT, -1, QT).transpose(0, 2, 3, 1).reshape(N_HEADS, -1, QT)
    return pl.pallas_call(
        _window_kernel,
        grid=(batch, nq),
        in_specs=[row(Q_COLS), pl.BlockSpec((length, 2 * N_KV * HEAD_DIM), lambda b, i: (b, 1)), row(LANES),
                  _resident(wtab.shape)],
        out_specs=row(Q_COLS),
        out_shape=jax.ShapeDtypeStruct((batch * length, Q_COLS), BF16),
        compiler_params=_cp(("parallel", "arbitrary"), 48),
        name="nsa_window_attn",
    )(q, kvb, gate, wtab)


def _ex_np(nb, length):
    return (np.arange(nb)[:, None] == (np.arange(length)[None, :] // SEL_BLOCK)).astype(np.float32)


def _selected_tables(tab):
    a = np.arange(QT)[:, None]
    b = np.arange(QT)[None, :]
    causal = np.where(a >= b, 0.0, NEG).astype(np.float32)
    const = lambda v: jnp.full((N_HEADS, QT, QT), v, F32)
    return jnp.concatenate([tab[:, :QT] + causal, tab[:, QT:], const(0.0), const(NEG)], axis=2)


def _s5_sample_kernel(u_ref, hr_ref, hi_ref, ws_ref, wv_ref, a1_ref, d_ref, g_ref, sre_ref, sim_ref):
    u = u_ref[...]
    x = _mm(u.astype(BF16), ws_ref[0, 0])
    ar, ai = a1_ref[0, :, :S5_SL], a1_ref[0, :, S5_SL:]
    h0r, h0i = hr_ref[...], hi_ref[...]
    hr = x[:, :S5_SL] + (ar * h0r - ai * h0i)
    hi = x[:, S5_SL:] + (ar * h0i + ai * h0r)
    sre_ref[...] = hr
    sim_ref[...] = hi
    y = _mm(jnp.concatenate([hr, hi], axis=1).astype(BF16), wv_ref[0, 0])
    g_ref[...] = jax.nn.gelu(y + d_ref[...] * u).astype(g_ref.dtype)


def _s5_sample(u, h0_re, h0_im, ws, wv, a1, d_skip):
    n = u.shape[0]
    st = pl.BlockSpec((n, S5_SL), lambda s: (0, s))
    return pl.pallas_call(
        _s5_sample_kernel,
        grid=(S5_NSG,),
        in_specs=[pl.BlockSpec((n, LANES), lambda s: (0, s)), st, st,
                  pl.BlockSpec((1, 1, LANES, 2 * S5_SL), lambda s: (S5_T - 1, s, 0, 0)),
                  pl.BlockSpec((1, 1, 2 * S5_SL, LANES), lambda s: (0, s, 0, 0)),
                  pl.BlockSpec((1, 1, 2 * S5_SL), lambda s: (s, 0, 0)),
                  pl.BlockSpec((1, LANES), lambda s: (0, s))],
        out_specs=[pl.BlockSpec((n, LANES), lambda s: (0, s)), st, st],
        out_shape=[jax.ShapeDtypeStruct((n, D_MODEL), BF16),
                   jax.ShapeDtypeStruct((n, N_GROUPS * STATE), F32),
                   jax.ShapeDtypeStruct((n, N_GROUPS * STATE), F32)],
        compiler_params=_cp(("parallel",), 32),
        name="s5_sample",
    )(u, h0_re, h0_im, ws, wv, a1, d_skip.reshape(1, D_MODEL))


def _softmax_lanes(s, valid):
    m = jnp.max(s, axis=-1, keepdims=True)
    e = jnp.exp(s - m)
    return (e * valid) * (1.0 / jnp.sum(e, axis=-1, keepdims=True))


def _own_group(o):
    rowg = lax.broadcasted_iota(jnp.int32, (N_HEADS, HEAD_DIM), 0) // GQ
    out = jnp.zeros((N_HEADS, HEAD_DIM), F32)
    for g in range(N_KV):
        out = out + jnp.where(rowg == g, o[:, g * HEAD_DIM:(g + 1) * HEAD_DIM], 0.0)
    return out


DECODE_SEQS_PER_STEP = 1


def _nsa_decode_kernel(pt_ref, *refs, past_len, n_cmp, npages):
    nseq = DECODE_SEQS_PER_STEP
    x_refs = refs[:nseq * npages]
    (win_ref, qbd_ref, new_ref, gate_ref, w1_ref, pe_ref, w2_ref, cov_ref, bias_ref, ex_ref, o_ref,
     xt_ref, xf_ref, ks_ref, vs_ref, kw_ref, vw_ref) = refs[nseq * npages:]
    page = x_refs[0].shape[2]
    hp = page // CMP_STRIDE
    nh = past_len // CMP_STRIDE
    nkv = N_KV * HEAD_DIM
    sel_cols = ks_ref.shape[2]
    win_cols = kw_ref.shape[2]
    wb = win_ref.shape[2]

    def stash(ref, sq, cols, x):
        ref[sq, :, cols] = x.astype(BF16)

    for sq in range(nseq):
        for p in range(npages):
            x_ref = x_refs[sq * npages + p]
            xt = x_ref[0, :2 * nkv, :].T
            for c in range(2 * nkv // LANES):
                xt_ref[sq, p, c] = xt[:, c * LANES:(c + 1) * LANES]
            for slot in range(2):
                for c in range(2):
                    blk = 2 * slot + c
                    ev, od = _flatten_pairs(lambda s: xt_ref[sq, p, blk, pl.ds(s, hp, stride=CMP_STRIDE), :], hp)
                    for head, val in ((2 * c, ev), (2 * c + 1, od)):
                        xf_ref[sq, slot, head * nh + p * hp:head * nh + (p + 1) * hp, :] = val
            cols = slice(p * page, (p + 1) * page)
            stash(ks_ref, sq, cols, x_ref[0, 2 * nkv:3 * nkv, :])
            stash(vs_ref, sq, cols, x_ref[0, 3 * nkv:, :])

    for sq in range(nseq):
        qbd = qbd_ref[sq]
        new = new_ref[sq]
        lane0 = lax.broadcasted_iota(jnp.int32, (nkv, LANES), 1) == 0
        newcol = lambda k: jnp.where(lane0, jnp.broadcast_to(new[:, k:k + 1], (nkv, LANES)), 0.0)
        stash(ks_ref, sq, slice(past_len, sel_cols), newcol(0))
        stash(vs_ref, sq, slice(past_len, sel_cols), newcol(1))
        stash(kw_ref, sq, slice(0, wb), win_ref[sq, :nkv, :])
        stash(vw_ref, sq, slice(0, wb), win_ref[sq, nkv:, :])
        stash(kw_ref, sq, slice(wb, win_cols), newcol(2))
        stash(vw_ref, sq, slice(wb, win_cols), newcol(3))
        rowh = lax.broadcasted_iota(jnp.int32, (N_HEADS, LANES), 0)

        kc = _compress_rows(xf_ref[sq, 0], w1_ref, pe_ref, w2_ref, 0).astype(BF16)
        vc = _compress_rows(xf_ref[sq, 1], w1_ref, pe_ref, w2_ref, 1).astype(BF16)
        s = jnp.zeros((N_HEADS, nh), F32)
        for g in range(N_KV):
            s = s + lax.dot_general(qbd[:, g * HEAD_DIM:(g + 1) * HEAD_DIM], kc[g * nh:(g + 1) * nh], _NT,
                                    preferred_element_type=F32)
        nidx = lax.broadcasted_iota(jnp.int32, (N_HEADS, nh), 1)
        s = s + bias_ref[:, :nh] + jnp.where(nidx < n_cmp, 0.0, NEG)
        pc = _softmax_lanes(s, (nidx < n_cmp).astype(F32)).astype(BF16)
        o_c = _own_group(jnp.concatenate([_mm(pc, vc[g * nh:(g + 1) * nh]) for g in range(N_KV)], axis=1))

        pcf = pc.astype(F32)
        psum = jnp.concatenate([jnp.sum(pcf[g * GQ:(g + 1) * GQ], axis=0, keepdims=True) for g in range(N_KV)]
                               + [jnp.zeros((8 - N_KV, nh), F32)], axis=0)
        imp = jnp.dot(psum, cov_ref[...], precision=HIGHEST, preferred_element_type=F32)
        jt = past_len // SEL_BLOCK
        nb = jt + 1
        lane = lax.broadcasted_iota(jnp.int32, (LANES, LANES), 1)
        subl = lax.broadcasted_iota(jnp.int32, (LANES, LANES), 0)

        def scored(x, j):
            forced = (j == 0) | (j == jt) | (j == jt - 1)
            return jnp.where(j < nb, jnp.where(forced, FORCE, x), NEG)

        imp_sq = jnp.concatenate([imp, jnp.zeros((LANES - 8, LANES), F32)], axis=0)
        sc_row = scored(imp_sq, lane)
        sc_col = scored(imp_sq.T, subl)
        selh = jnp.zeros((N_HEADS, LANES), F32)
        for g in range(N_KV):
            col = sc_col[:, g:g + 1]
            row = sc_row[g:g + 1, :]
            ahead = (col > row) | ((col == row) & (subl < lane))
            rank = jnp.sum(jnp.where(ahead, 1.0, 0.0), axis=0, keepdims=True)
            chosen = jnp.where((rank < min(TOP_N, nb)) & (lane[:1] < nb), 1.0, 0.0)
            selh = selh + jnp.where(rowh // GQ == g, chosen, 0.0)

        pos = lax.broadcasted_iota(jnp.int32, (N_HEADS, sel_cols), 1)
        inblk = jnp.dot(selh.astype(BF16), ex_ref[...], preferred_element_type=F32)
        ok_s = (inblk > 0.5) & (pos <= past_len)
        s = _mm(qbd, ks_ref[sq]) + bias_ref[:, nh:nh + sel_cols]
        ps = _softmax_lanes(jnp.where(ok_s, s, NEG), ok_s.astype(F32)).astype(BF16)
        o_s = _own_group(lax.dot_general(ps, vs_ref[sq], _NT, preferred_element_type=F32))

        posw = lax.broadcasted_iota(jnp.int32, (N_HEADS, win_cols), 1)
        ok_w = posw <= wb
        s = _mm(qbd, kw_ref[sq]) + bias_ref[:, nh + sel_cols:nh + sel_cols + win_cols]
        pw = _softmax_lanes(jnp.where(ok_w, s, NEG), ok_w.astype(F32)).astype(BF16)
        o_w = _own_group(lax.dot_general(pw, vw_ref[sq], _NT, preferred_element_type=F32))

        gate = gate_ref[sq]
        o_ref[sq] = gate[:, 0:1] * o_c + gate[:, 1:2] * o_s + gate[:, 2:3] * o_w


def _decode_bias_idx(past_len, wb, ncp, sel_cols, win_cols):
    n = np.arange(ncp)
    cmp_d = past_len - (n * CMP_STRIDE + CMP_BLOCK - 1)
    sel_d = past_len - np.arange(sel_cols)
    win_d = np.concatenate([wb - np.arange(wb), np.zeros(win_cols - wb, np.int64)])
    idx = _t5_bucket_np(np.concatenate([cmp_d, sel_d, win_d]))
    pad = -len(idx) % (8 * LANES)
    return np.concatenate([idx, np.full(pad, N_BUCKETS - 1)]).reshape(-1, LANES).astype(np.int32)


def _nsa_decode(cache_kv, cache_win, page_table, qbd, new_cols, gate3, cw, rel_bias):
    n_pool, page = cache_kv.shape[:2]
    nseq, npages = page_table.shape
    past_len = npages * page
    wb = cache_win.shape[1]
    w1b, pe, w2b = cw
    nh = past_len // CMP_STRIDE
    n_cmp = (past_len + 1 - CMP_BLOCK) // CMP_STRIDE + 1
    sel_cols = past_len + LANES
    win_cols = wb + LANES
    nkv = N_KV * HEAD_DIM
    kvt = cache_kv.transpose(0, 2, 3, 4, 1).reshape(n_pool, 4 * nkv, page)
    wint = cache_win.transpose(0, 2, 3, 4, 1).reshape(nseq, 2 * nkv, wb)
    cov = jnp.asarray(_cover_t_np(LANES, LANES).T)
    idx = _decode_bias_idx(past_len, wb, nh, sel_cols, win_cols)
    bias = _bias_tables(rel_bias, jnp.asarray(idx)).reshape(N_HEADS, idx.size)
    ex = jnp.asarray((np.arange(LANES)[:, None] == np.arange(sel_cols)[None, :] // SEL_BLOCK).astype(np.float32), BF16)
    const = lambda a: pl.BlockSpec(a.shape, lambda b, pt: (0,) * a.ndim)
    sps = DECODE_SEQS_PER_STEP
    assert nseq % sps == 0
    page_spec = lambda sq, p: pl.BlockSpec((1, 4 * nkv, page), lambda b, pt: (pt[b * sps + sq, p], 0, 0))
    per_seq = lambda *dims: pl.BlockSpec((sps,) + dims, lambda b, pt: (b,) + (0,) * len(dims))
    grid_spec = pltpu.PrefetchScalarGridSpec(
        num_scalar_prefetch=1,
        grid=(nseq // sps,),
        in_specs=[page_spec(sq, p) for sq in range(sps) for p in range(npages)] + [
                  per_seq(2 * nkv, wb), per_seq(N_HEADS, nkv), per_seq(nkv, LANES), per_seq(N_HEADS, LANES),
                  const(w1b), const(pe), const(w2b), const(cov), const(bias), const(ex)],
        out_specs=per_seq(N_HEADS, HEAD_DIM),
        scratch_shapes=[pltpu.VMEM((sps, npages, 2 * nkv // LANES, page, LANES), F32),
                        pltpu.VMEM((sps, 2, N_KV * nh, CMP_STRIDE * HEAD_DIM), F32),
                        pltpu.VMEM((sps, nkv, sel_cols), BF16), pltpu.VMEM((sps, nkv, sel_cols), BF16),
                        pltpu.VMEM((sps, nkv, win_cols), BF16), pltpu.VMEM((sps, nkv, win_cols), BF16)])
    out = pl.pallas_call(
        functools.partial(_nsa_decode_kernel, past_len=past_len, n_cmp=n_cmp, npages=npages),
        grid_spec=grid_spec,
        out_shape=jax.ShapeDtypeStruct((nseq, N_HEADS, HEAD_DIM), F32),
        compiler_params=_cp(("arbitrary",), 58),
        name="nsa_decode",
    )(page_table, *([kvt] * (sps * npages)), wint, qbd, new_cols, gate3, w1b, pe, w2b, cov, bias, ex)
    return out.reshape(nseq, Q_COLS)


def _decode_operands(qs, kvbs, gates):
    nseq = qs.shape[0]
    nkv = N_KV * HEAD_DIM
    own = (np.arange(N_HEADS)[:, None] // GQ) == (np.arange(nkv)[None, :] // HEAD_DIM)
    qbd = jnp.where(own[None], jnp.tile(qs.reshape(nseq, N_HEADS, HEAD_DIM), (1, 1, N_KV)), 0.0)
    new_cols = jnp.pad(kvbs.astype(F32).reshape(nseq, 4, nkv).transpose(0, 2, 1), ((0, 0), (0, 0), (0, LANES - 4)))
    gate3 = jnp.pad(gates[:, :GATE_COLS].reshape(nseq, 3, N_HEADS).transpose(0, 2, 1),
                    ((0, 0), (0, 0), (0, LANES - 3)))
    return qbd, new_cols, gate3


PROMPT_TM = 512


def kernel(x_prompt, x_sample, state_s5_re, state_s5_im, cache_kv, cache_win, page_table, rel_bias, norm_mix, norm_ffn, norm_final, s5_lam_re, s5_lam_im, s5_log_dt, s5_b_re, s5_b_im, s5_c_re, s5_c_im, s5_d, s5_w_glu, ffn_w_in, ffn_w_out, nsa_w_in, nsa_phi_pe, nsa_phi_w1, nsa_phi_w2, nsa_w_o, moe_router, moe_w_in, moe_w_out):
    batch, length, _ = x_prompt.shape
    nseq = x_sample.shape[0]
    xp = x_prompt.reshape(batch * length, D_MODEL)
    xs = x_sample.reshape(nseq, D_MODEL)
    nstate = N_GROUPS * STATE

    wk, ws, wv, apow, a1 = _s5_prep(s5_lam_re, s5_lam_im, s5_log_dt, s5_b_re, s5_b_im, s5_c_re, s5_c_im)
    w_glu = s5_w_glu.astype(BF16)
    ffn_in, ffn_out = ffn_w_in.astype(BF16), ffn_w_out.astype(BF16)
    nsa_in = jnp.pad(nsa_w_in, ((0, 0), (0, NSA_PAD - nsa_w_in.shape[1]))).astype(BF16)
    w_o = nsa_w_o.astype(BF16)
    router = jnp.pad(moe_router, ((0, 0), (0, LANES - N_EXPERTS))).astype(BF16)
    moe_in, moe_out = moe_w_in.astype(BF16), moe_w_out.astype(BF16)
    cw = _compress_weights(nsa_phi_pe, nsa_phi_w1, nsa_phi_w2)
    tabs = _bias_tables(rel_bias, jnp.asarray(_prompt_bias_idx()))
    tab = tabs[:, :2 * QT]
    gflat = tabs[:, 2 * QT:].reshape(N_HEADS * QT, LANES)
    ghi = gflat.astype(BF16)
    glo = (gflat - ghi.astype(F32)).astype(BF16)

    up = _rmsnorm(xp, norm_mix[0], PROMPT_TM)
    gp, sre_p, sim_p = _s5_prompt(up, s5_d, wk, ws, wv, apow, batch, length)
    y1p, h1p = _glu(gp, xp, w_glu, norm_ffn[0], PROMPT_TM)
    y2p, h2p = _ffn(h1p, y1p, ffn_in, ffn_out, norm_mix[1], PROMPT_TM)
    q, kvf, winf, kvb, gate = _nsa_proj(h2p, nsa_in, PROMPT_TM)
    kcv = _compress_prompt(kvf, cw, batch, length)
    oc, selt = _cmp_attn(q, kcv, gate, ghi, glo, batch, length)
    os_ = _selected_t_attn(q, kvb, gate, _selected_tables(tab), selt, batch, length)
    ow = _window_attn(q, kvb, gate, _window_tables(tab), batch, length)
    y3p, h3p, mgp, rtp, cntp = _merge(oc, os_, ow, y2p, w_o, norm_ffn[1], router, MOE_TM)
    counts = cntp[:, 0, :N_EXPERTS].astype(jnp.int32).reshape(-1)
    yp = _moe_routed(h3p, y3p, mgp, rtp, counts, moe_in, moe_out, norm_final)

    us = _rmsnorm(xs, norm_mix[0], nseq)
    gs, sre_s, sim_s = _s5_sample(us, state_s5_re.reshape(nseq, nstate), state_s5_im.reshape(nseq, nstate),
                                  ws, wv, a1, s5_d)
    y1s, h1s = _glu(gs, xs, w_glu, norm_ffn[0], nseq)
    y2s, h2s = _ffn(h1s, y1s, ffn_in, ffn_out, norm_mix[1], nseq)
    qs, kvfs, winfs, kvbs, gates = _nsa_proj(h2s, nsa_in, nseq)
    qbd, new_cols, gate3 = _decode_operands(qs, kvbs, gates)
    od = _nsa_decode(cache_kv, cache_win, page_table, qbd, new_cols, gate3, cw, rel_bias)
    zero = jnp.zeros((nseq, D_MODEL), F32)
    y3s, h3s, mgs, _, _ = _merge(od, zero, zero, y2s, w_o, norm_ffn[1], router, nseq)
    ys = _moe(h3s, y3s, mgs, moe_in, moe_out, norm_final, nseq)

    wp = min(WINDOW, length)
    return (yp.reshape(batch, length, D_MODEL),
            ys.reshape(nseq, 1, D_MODEL),
            sre_p.reshape(batch, N_GROUPS, STATE), sim_p.reshape(batch, N_GROUPS, STATE),
            kvf.reshape(batch, length, 4, N_KV, HEAD_DIM),
            winf.reshape(batch, length, 2, N_KV, HEAD_DIM)[:, length - wp:],
            sre_s.reshape(nseq, N_GROUPS, STATE), sim_s.reshape(nseq, N_GROUPS, STATE),
            kvfs.reshape(nseq, 1, 4, N_KV, HEAD_DIM),
            winfs.reshape(nseq, 1, 2, N_KV, HEAD_DIM))
```

```python
import functools
import math

import numpy as np
import jax
import jax.numpy as jnp
from jax import lax
from jax.experimental import pallas as pl
from jax.experimental.pallas import tpu as pltpu

F32 = jnp.float32
BF16 = jnp.bfloat16
HIGHEST = lax.Precision.HIGHEST

D_MODEL = 1024
GROUP_CH = 16
N_GROUPS = D_MODEL // GROUP_CH
STATE = 64
N_HEADS = 16
HEAD_DIM = 64
N_KV = 4
GQ = N_HEADS // N_KV
CMP_BLOCK = 32
CMP_STRIDE = 16
CMP_HIDDEN = 2 * HEAD_DIM
SEL_BLOCK = 64
TOP_N = 16
WINDOW = 512
Q_COLS = N_HEADS * HEAD_DIM
KV_COLS = 6 * N_KV * HEAD_DIM
GATE_COLS = 3 * N_HEADS
N_BUCKETS = 32
MAX_DISTANCE = 128
D_FF = 2816
N_EXPERTS = 8
EXPERT_FF = 1408
EPS = 1e-6
NEG = -1e30
FORCE = 1e4

LANES = 128
S5_T = 8
S5_SG = LANES // GROUP_CH
S5_NSG = N_GROUPS // S5_SG
S5_SL = S5_SG * STATE
QT = 128


def _cp(sem, vmem_mb):
    return pltpu.CompilerParams(dimension_semantics=sem, vmem_limit_bytes=vmem_mb << 20)


def _cmul(ar, ai, br, bi):
    return ar * br - ai * bi, ar * bi + ai * br


def _s5_prep_kernel(lr_r, li_r, ldt_r, br_r, bi_r, lr_c, li_c, ldt_c, cr_c, ci_c,
                    wk_ref, ws_ref, wv_ref, apow_ref, a1_ref):
    def disc(lr, li, ldt):
        dt = jnp.exp(ldt)
        mag = jnp.exp(lr * dt)
        abr = mag * jnp.cos(li * dt)
        abi = mag * jnp.sin(li * dt)
        den = lr * lr + li * li
        nr = abr - 1.0
        fr = (nr * lr + abi * li) / den
        fi = (abi * lr - nr * li) / den
        return abr, abi, fr, fi

    abr, abi, fr, fi = disc(lr_r[0], li_r[0], ldt_r[0])
    rg = lax.broadcasted_iota(jnp.int32, (LANES, S5_SL), 0) // GROUP_CH
    cg = lax.broadcasted_iota(jnp.int32, (LANES, S5_SL), 1) // STATE
    mask_r = (rg == cg).astype(F32)
    b_r, b_i = br_r[...], bi_r[...]
    bbr = (fr * b_r - fi * b_i) * mask_r
    bbi = (fr * b_i + fi * b_r) * mask_r
    abr_c, abi_c, _, _ = disc(lr_c[0], li_c[0], ldt_c[0])
    rg2 = lax.broadcasted_iota(jnp.int32, (S5_SL, LANES), 0) // STATE
    cg2 = lax.broadcasted_iota(jnp.int32, (S5_SL, LANES), 1) // GROUP_CH
    mask_c = (rg2 == cg2).astype(F32)
    c_r = cr_c[0] * mask_c
    c_i = ci_c[0] * mask_c

    a1_ref[0, :, :S5_SL] = abr
    a1_ref[0, :, S5_SL:] = abi

    pr = jnp.ones_like(abr)
    pi = jnp.zeros_like(abr)
    pr_c = jnp.ones_like(abr_c)
    pi_c = jnp.zeros_like(abr_c)
    for tau in range(S5_T + 1):
        wv_ref[tau, 0, :S5_SL, :] = (c_r * pr_c - c_i * pi_c).astype(BF16)
        wv_ref[tau, 0, S5_SL:, :] = (-(c_r * pi_c + c_i * pr_c)).astype(BF16)
        if tau < S5_T:
            er, ei = _cmul(pr, pi, bbr, bbi)
            s = S5_T - 1 - tau
            ws_ref[s, 0, :, :S5_SL] = er.astype(BF16)
            ws_ref[s, 0, :, S5_SL:] = ei.astype(BF16)
            k = (jnp.dot(er, c_r, precision=HIGHEST, preferred_element_type=F32)
                 - jnp.dot(ei, c_i, precision=HIGHEST, preferred_element_type=F32))
            wk_ref[tau, 0] = k.astype(BF16)
        pr, pi = _cmul(pr, pi, abr, abi)
        pr_c, pi_c = _cmul(pr_c, pi_c, abr_c, abi_c)
    tr, ti = jnp.ones_like(abr), jnp.zeros_like(abr)
    for _ in range(S5_T):
        tr, ti = _cmul(tr, ti, abr, abi)
    qr, qi = jnp.ones_like(abr), jnp.zeros_like(abr)
    for i in range(S5_T + 1):
        apow_ref[0, i:i + 1, :S5_SL] = qr
        apow_ref[0, i:i + 1, S5_SL:] = qi
        qr, qi = _cmul(qr, qi, tr, ti)


def _s5_prep(lam_re, lam_im, log_dt, b_re, b_im, c_re, c_im):
    n = S5_NSG
    row = lambda a: a.reshape(n, 1, S5_SL)
    col = lambda a: a.reshape(n, S5_SL, 1)
    ldt = jnp.repeat(log_dt, STATE)
    bt = lambda b: jnp.tile(b.transpose(0, 2, 1).reshape(D_MODEL, STATE), (1, S5_SG))
    ct = lambda c: jnp.tile(c.reshape(n, LANES, STATE).transpose(0, 2, 1), (1, S5_SG, 1))
    rspec = pl.BlockSpec((1, 1, S5_SL), lambda s: (s, 0, 0))
    cspec = pl.BlockSpec((1, S5_SL, 1), lambda s: (s, 0, 0))
    t1 = S5_T + 1
    return pl.pallas_call(
        _s5_prep_kernel,
        grid=(n,),
        in_specs=[rspec, rspec, rspec,
                  pl.BlockSpec((LANES, S5_SL), lambda s: (s, 0)), pl.BlockSpec((LANES, S5_SL), lambda s: (s, 0)),
                  cspec, cspec, cspec,
                  pl.BlockSpec((1, S5_SL, LANES), lambda s: (s, 0, 0)), pl.BlockSpec((1, S5_SL, LANES), lambda s: (s, 0, 0))],
        out_specs=[pl.BlockSpec((S5_T, 1, LANES, LANES), lambda s: (0, s, 0, 0)),
                   pl.BlockSpec((S5_T, 1, LANES, 2 * S5_SL), lambda s: (0, s, 0, 0)),
                   pl.BlockSpec((t1, 1, 2 * S5_SL, LANES), lambda s: (0, s, 0, 0)),
                   pl.BlockSpec((1, t1 + 7, 2 * S5_SL), lambda s: (s, 0, 0)),
                   pl.BlockSpec((1, 1, 2 * S5_SL), lambda s: (s, 0, 0))],
        out_shape=[jax.ShapeDtypeStruct((S5_T, n, LANES, LANES), BF16),
                   jax.ShapeDtypeStruct((S5_T, n, LANES, 2 * S5_SL), BF16),
                   jax.ShapeDtypeStruct((t1, n, 2 * S5_SL, LANES), BF16),
                   jax.ShapeDtypeStruct((n, t1 + 7, 2 * S5_SL), F32),
                   jax.ShapeDtypeStruct((n, 1, 2 * S5_SL), F32)],
        compiler_params=_cp(("arbitrary",), 48),
        name="s5_prep",
    )(row(lam_re), row(lam_im), row(ldt), bt(b_re), bt(b_im),
      col(lam_re), col(lam_im), col(ldt), ct(c_re), ct(c_im))


def _rms(x, g):
    return x * lax.rsqrt(jnp.mean(x * x, axis=-1, keepdims=True) + EPS) * g


def _rmsnorm_kernel(x_ref, g_ref, o_ref):
    o_ref[...] = _rms(x_ref[...], g_ref[...]).astype(o_ref.dtype)


def _rmsnorm(x, g, tm, dtype=F32):
    n = x.shape[0]
    return pl.pallas_call(
        _rmsnorm_kernel,
        grid=(n // tm,),
        in_specs=[pl.BlockSpec((tm, D_MODEL), lambda i: (i, 0)), pl.BlockSpec((1, D_MODEL), lambda i: (0, 0))],
        out_specs=pl.BlockSpec((tm, D_MODEL), lambda i: (i, 0)),
        out_shape=jax.ShapeDtypeStruct((n, D_MODEL), dtype),
        compiler_params=_cp(("parallel",), 32),
        name="rmsnorm",
    )(x, g.reshape(1, D_MODEL))


def _s5_prompt_kernel(u_ref, d_ref, wk_ref, ws_ref, wv_ref, apow_ref, g_ref, sre_ref, sim_ref,
                      ub_ref, s_ref, p_ref, gc_ref, hs_ref, y_ref):
    length = u_ref.shape[0]
    r = length // S5_T
    ng = r // 8

    acc = jnp.zeros((r, 2 * S5_SL), F32)
    for s in range(S5_T):
        us = u_ref[pl.ds(s, r, stride=S5_T), :].astype(BF16)
        ub_ref[s] = us
        acc = acc + jnp.dot(us, ws_ref[s, 0], preferred_element_type=F32)
    for c in range(2 * S5_SL // LANES):
        s_ref[c] = acc[:, c * LANES:(c + 1) * LANES]

    nsl = S5_SL // LANES
    for c in range(nsl):
        lr = slice(c * LANES, (c + 1) * LANES)
        li = slice(S5_SL + c * LANES, S5_SL + (c + 1) * LANES)

        def apw(i):
            return apow_ref[0, i:i + 1, lr], apow_ref[0, i:i + 1, li]

        a1r, a1i = apw(1)
        pr = s_ref[c, pl.ds(0, ng, stride=8), :]
        pi = s_ref[nsl + c, pl.ds(0, ng, stride=8), :]
        p_ref[0, c] = pr
        p_ref[0, nsl + c] = pi
        for i in range(1, 8):
            qr, qi = _cmul(a1r, a1i, pr, pi)
            pr = qr + s_ref[c, pl.ds(i, ng, stride=8), :]
            pi = qi + s_ref[nsl + c, pl.ds(i, ng, stride=8), :]
            p_ref[i, c] = pr
            p_ref[i, nsl + c] = pi

        a8r, a8i = apw(8)

        def carry(m, g):
            gr, gi = g
            gc_ref[c, pl.ds(m, 1), :] = gr
            gc_ref[nsl + c, pl.ds(m, 1), :] = gi
            nr, ni = _cmul(a8r, a8i, gr, gi)
            return (nr + p_ref[7, c, pl.ds(m, 1), :], ni + p_ref[7, nsl + c, pl.ds(m, 1), :])

        z = jnp.zeros((1, LANES), F32)
        er, ei = lax.fori_loop(0, ng, carry, (z, z))
        sre_ref[0, :, lr] = er
        sim_ref[0, :, lr] = ei

        gr = gc_ref[c]
        gi = gc_ref[nsl + c]
        hs_ref[c, pl.ds(0, ng, stride=8), :] = gr
        hs_ref[nsl + c, pl.ds(0, ng, stride=8), :] = gi
        for i in range(1, 8):
            air, aii = apw(i)
            hr, hi = _cmul(air, aii, gr, gi)
            hs_ref[c, pl.ds(i, ng, stride=8), :] = hr + p_ref[i - 1, c]
            hs_ref[nsl + c, pl.ds(i, ng, stride=8), :] = hi + p_ref[i - 1, nsl + c]

    hs = jnp.concatenate([hs_ref[c] for c in range(2 * S5_SL // LANES)], axis=1).astype(BF16)
    for t in range(S5_T):
        o = jnp.dot(hs, wv_ref[t + 1, 0], preferred_element_type=F32)
        for s in range(t + 1):
            o = o + jnp.dot(ub_ref[s], wk_ref[t - s, 0], preferred_element_type=F32)
        y_ref[pl.ds(t, r, stride=S5_T), :] = o
    y = y_ref[...] + d_ref[...] * u_ref[...]
    g_ref[...] = jax.nn.gelu(y).astype(g_ref.dtype)


def _s5_prompt(u, d_skip, wk, ws, wv, apow, batch, length):
    r = length // S5_T
    t1 = S5_T + 1
    return pl.pallas_call(
        _s5_prompt_kernel,
        grid=(S5_NSG, batch),
        in_specs=[pl.BlockSpec((length, LANES), lambda s, b: (b, s)),
                  pl.BlockSpec((1, LANES), lambda s, b: (0, s)),
                  pl.BlockSpec((S5_T, 1, LANES, LANES), lambda s, b: (0, s, 0, 0)),
                  pl.BlockSpec((S5_T, 1, LANES, 2 * S5_SL), lambda s, b: (0, s, 0, 0)),
                  pl.BlockSpec((t1, 1, 2 * S5_SL, LANES), lambda s, b: (0, s, 0, 0)),
                  pl.BlockSpec((1, t1 + 7, 2 * S5_SL), lambda s, b: (s, 0, 0))],
        out_specs=[pl.BlockSpec((length, LANES), lambda s, b: (b, s)),
                   pl.BlockSpec((1, 1, S5_SL), lambda s, b: (b, 0, s)),
                   pl.BlockSpec((1, 1, S5_SL), lambda s, b: (b, 0, s))],
        out_shape=[jax.ShapeDtypeStruct((batch * length, D_MODEL), BF16),
                   jax.ShapeDtypeStruct((batch, 1, N_GROUPS * STATE), F32),
                   jax.ShapeDtypeStruct((batch, 1, N_GROUPS * STATE), F32)],
        scratch_shapes=[pltpu.VMEM((S5_T, r, LANES), BF16),
                        pltpu.VMEM((2 * S5_SL // LANES, r, LANES), F32),
                        pltpu.VMEM((8, 2 * S5_SL // LANES, r // 8, LANES), F32),
                        pltpu.VMEM((2 * S5_SL // LANES, r // 8, LANES), F32),
                        pltpu.VMEM((2 * S5_SL // LANES, r, LANES), F32),
                        pltpu.VMEM((length, LANES), F32)],
        compiler_params=_cp(("arbitrary", "arbitrary"), 56),
        name="s5_prompt",
    )(u, d_skip.reshape(1, D_MODEL), wk, ws, wv, apow)


def _resident(shape):
    nd = len(shape)
    return pl.BlockSpec(shape, lambda *_: (0,) * nd, pipeline_mode=pl.Buffered(1))


def _rows(tm, width):
    return pl.BlockSpec((tm, width), lambda i: (i, 0))


def _mm(a, b):
    return jnp.dot(a, b, preferred_element_type=F32)


def _glu_kernel(g_ref, x_ref, w_ref, gain_ref, y_ref, h_ref):
    p = _mm(g_ref[...], w_ref[...])
    y = x_ref[...] + p[:, :D_MODEL] * jax.nn.sigmoid(p[:, D_MODEL:])
    y_ref[...] = y
    h_ref[...] = _rms(y, gain_ref[...]).astype(h_ref.dtype)


def _glu(g, x, w_glu, gain, tm):
    n = g.shape[0]
    return pl.pallas_call(
        _glu_kernel,
        grid=(n // tm,),
        in_specs=[_rows(tm, D_MODEL), _rows(tm, D_MODEL), _resident((D_MODEL, 2 * D_MODEL)), _resident((1, D_MODEL))],
        out_specs=[_rows(tm, D_MODEL), _rows(tm, D_MODEL)],
        out_shape=[jax.ShapeDtypeStruct((n, D_MODEL), F32), jax.ShapeDtypeStruct((n, D_MODEL), g.dtype)],
        compiler_params=_cp(("parallel",), 48),
        name="s5_glu",
    )(g, x, w_glu, gain.reshape(1, D_MODEL))


def _chunks(total, size):
    return [(s, min(size, total - s)) for s in range(0, total, size)]


def _swiglu_acc(h, win_ref, wout_ref, ff, lead=()):
    acc = None
    for s, w in _chunks(ff, 256):
        a = jnp.dot(h, win_ref[lead + (slice(None), slice(s, s + w))], preferred_element_type=F32)
        b = jnp.dot(h, win_ref[lead + (slice(None), slice(ff + s, ff + s + w))], preferred_element_type=F32)
        act = (jax.nn.silu(a) * b).astype(BF16)
        z = jnp.dot(act, wout_ref[lead + (slice(s, s + w), slice(None))], preferred_element_type=F32)
        acc = z if acc is None else acc + z
    return acc


def _ffn_kernel(h_ref, y_ref, win_ref, wout_ref, gain_ref, yo_ref, ho_ref):
    y = y_ref[...] + _swiglu_acc(h_ref[...], win_ref, wout_ref, D_FF)
    yo_ref[...] = y
    ho_ref[...] = _rms(y, gain_ref[...]).astype(ho_ref.dtype)


def _ffn(h, y, w_in, w_out, gain, tm):
    n = h.shape[0]
    return pl.pallas_call(
        _ffn_kernel,
        grid=(n // tm,),
        in_specs=[_rows(tm, D_MODEL), _rows(tm, D_MODEL), _resident((D_MODEL, 2 * D_FF)), _resident((D_FF, D_MODEL)),
                  _resident((1, D_MODEL))],
        out_specs=[_rows(tm, D_MODEL), _rows(tm, D_MODEL)],
        out_shape=[jax.ShapeDtypeStruct((n, D_MODEL), F32), jax.ShapeDtypeStruct((n, D_MODEL), BF16)],
        compiler_params=_cp(("parallel",), 56),
        name="ffn_swiglu",
    )(h, y, w_in, w_out, gain.reshape(1, D_MODEL))


NSA_PAD = 2688
KV_OFF = Q_COLS


def _nsa_proj_kernel(h_ref, w_ref, q_ref, kvf_ref, winf_ref, kvb_ref, gate_ref):
    p = _mm(h_ref[...], w_ref[...])
    q_ref[...] = (p[:, :Q_COLS] * (HEAD_DIM ** -0.5)).astype(q_ref.dtype)
    kvf_ref[...] = p[:, KV_OFF:KV_OFF + 1024]
    winf_ref[...] = p[:, KV_OFF + 1024:KV_OFF + 1536]
    kvb_ref[...] = p[:, KV_OFF + 512:KV_OFF + 1536].astype(kvb_ref.dtype)
    gate_ref[...] = jax.nn.sigmoid(p[:, KV_OFF + 1536:])


def _nsa_proj(h, w_in_pad, tm):
    n = h.shape[0]
    return pl.pallas_call(
        _nsa_proj_kernel,
        grid=(n // tm,),
        in_specs=[_rows(tm, D_MODEL), _resident((D_MODEL, NSA_PAD))],
        out_specs=[_rows(tm, Q_COLS), _rows(tm, 1024), _rows(tm, 512), _rows(tm, 1024), _rows(tm, LANES)],
        out_shape=[jax.ShapeDtypeStruct((n, Q_COLS), h.dtype),
                   jax.ShapeDtypeStruct((n, 1024), F32),
                   jax.ShapeDtypeStruct((n, 512), F32),
                   jax.ShapeDtypeStruct((n, 1024), h.dtype),
                   jax.ShapeDtypeStruct((n, LANES), F32)],
        compiler_params=_cp(("parallel",), 48),
        name="nsa_proj",
    )(h, w_in_pad)


def _merge_kernel(oc_ref, os_ref, ow_ref, y_ref, wo_ref, gain_ref, router_ref, yo_ref, ho_ref, mg_ref, rt_ref,
                  cnt_ref):
    o = (oc_ref[...].astype(F32) + os_ref[...].astype(F32) + ow_ref[...].astype(F32)).astype(wo_ref.dtype)
    y = y_ref[...] + _mm(o, wo_ref[...])
    yo_ref[...] = y
    hf = _rms(y, gain_ref[...])
    ho_ref[...] = hf.astype(ho_ref.dtype)
    logits = _mm(hf.astype(BF16), router_ref[...])
    lane = lax.broadcasted_iota(jnp.int32, logits.shape, 1)
    logits = jnp.where(lane < N_EXPERTS, logits, -jnp.inf)
    m1 = jnp.max(logits, axis=-1, keepdims=True)
    i1 = jnp.min(jnp.where(logits == m1, lane, LANES), axis=-1, keepdims=True)
    rest = jnp.where(lane == i1, -jnp.inf, logits)
    m2 = jnp.max(rest, axis=-1, keepdims=True)
    i2 = jnp.min(jnp.where(rest == m2, lane, LANES), axis=-1, keepdims=True)
    e2 = jnp.exp(m2 - m1)
    den = 1.0 + e2
    mg_ref[...] = jnp.where(lane == i1, 1.0 / den, 0.0) + jnp.where(lane == i2, e2 / den, 0.0)
    routed = jnp.where((lane == i1) | (lane == i2), 1.0, 0.0)
    rt_ref[...] = routed
    cnt_ref[0] = jnp.broadcast_to(jnp.sum(routed, axis=0, keepdims=True), cnt_ref.shape[1:])


def _merge(oc, os_, ow, y, w_o, gain, router_pad, tm):
    n = y.shape[0]
    return pl.pallas_call(
        _merge_kernel,
        grid=(n // tm,),
        in_specs=[_rows(tm, D_MODEL)] * 4 + [_resident((D_MODEL, D_MODEL)), _resident((1, D_MODEL)),
                                             _resident((D_MODEL, LANES))],
        out_specs=[_rows(tm, D_MODEL), _rows(tm, D_MODEL), _rows(tm, LANES), _rows(tm, LANES),
                   pl.BlockSpec((1, 8, LANES), lambda i: (i, 0, 0))],
        out_shape=[jax.ShapeDtypeStruct((n, D_MODEL), F32), jax.ShapeDtypeStruct((n, D_MODEL), BF16),
                   jax.ShapeDtypeStruct((n, LANES), F32), jax.ShapeDtypeStruct((n, LANES), F32),
                   jax.ShapeDtypeStruct((n // tm, 8, LANES), F32)],
        compiler_params=_cp(("parallel",), 56),
        name="nsa_merge_router",
    )(oc, os_, ow, y, w_o, gain.reshape(1, D_MODEL), router_pad)


def _moe_kernel(h_ref, y_ref, mg_ref, win_ref, wout_ref, gain_ref, o_ref, acc_ref):
    e = pl.program_id(1)

    @pl.when(e == 0)
    def _():
        acc_ref[...] = jnp.zeros_like(acc_ref)

    mg = mg_ref[...]
    lane = lax.broadcasted_iota(jnp.int32, mg.shape, 1)
    gcol = jnp.sum(jnp.where(lane == e, mg, 0.0), axis=-1, keepdims=True)
    acc_ref[...] += gcol * _swiglu_acc(h_ref[...], win_ref, wout_ref, EXPERT_FF, lead=(0,))

    @pl.when(e == N_EXPERTS - 1)
    def _():
        o_ref[...] = _rms(y_ref[...] + acc_ref[...], gain_ref[...])


def _moe(h, y, mg, w_in, w_out, gain, tm):
    n = h.shape[0]
    rows = lambda w: pl.BlockSpec((tm, w), lambda i, e: (i, 0))
    return pl.pallas_call(
        _moe_kernel,
        grid=(n // tm, N_EXPERTS),
        in_specs=[rows(D_MODEL), rows(D_MODEL), rows(LANES),
                  pl.BlockSpec((1, D_MODEL, 2 * EXPERT_FF), lambda i, e: (e, 0, 0)),
                  pl.BlockSpec((1, EXPERT_FF, D_MODEL), lambda i, e: (e, 0, 0)),
                  pl.BlockSpec((1, D_MODEL), lambda i, e: (0, 0))],
        out_specs=rows(D_MODEL),
        out_shape=jax.ShapeDtypeStruct((n, D_MODEL), F32),
        scratch_shapes=[pltpu.VMEM((tm, D_MODEL), F32)],
        compiler_params=_cp(("parallel", "arbitrary"), 56),
        name="moe_final_norm",
    )(h, y, mg, w_in, w_out, gain.reshape(1, D_MODEL))


MOE_TM = 1024


def _moe_passes(tm):
    first = tm // 4
    return [first] + [tm // 8] * ((tm - first) // (tm // 8))


def _moe_routed_kernel(cnt_ref, h_ref, y_ref, mg_ref, rt_ref, lower_ref, win_ref, wout_ref, gain_ref, o_ref,
                       acc_ref, rcol_ref, rrow_ref, rtrow_ref):
    t = pl.program_id(0)
    e = pl.program_id(1)
    tm = h_ref.shape[0]

    @pl.when(e == 0)
    def _():
        acc_ref[...] = jnp.zeros_like(acc_ref)
        rt = rt_ref[...]
        rank = _mm(lower_ref[...], rt.astype(BF16))
        rcol_ref[...] = rank
        rrow_ref[...] = rank.T
        rtrow_ref[...] = rt.T

    n = cnt_ref[t * N_EXPERTS + e]
    lane = lax.broadcasted_iota(jnp.int32, (tm, LANES), 1)
    col = lambda ref: jnp.sum(jnp.where(lane == e, ref[...], 0.0), axis=-1, keepdims=True)
    gcol, rank_col, rt_col = col(mg_ref), col(rcol_ref), col(rt_ref)
    rank_row = rrow_ref[pl.ds(e, 1), :]
    rt_row = rtrow_ref[pl.ds(e, 1), :]
    start = 0
    for rb in _moe_passes(tm):
        @pl.when(start < n)
        def _():
            ridx = (start + lax.broadcasted_iota(jnp.int32, (rb, tm), 0)).astype(F32)
            take = jnp.where((rank_row == ridx) & (rt_row > 0.5), 1.0, 0.0).astype(BF16)
            x = _mm(take, h_ref[...]).astype(BF16)
            y = _swiglu_acc(x, win_ref, wout_ref, EXPERT_FF, lead=(0,))
            cidx = (start + lax.broadcasted_iota(jnp.int32, (tm, rb), 1)).astype(F32)
            put = jnp.where((rank_col == cidx) & (rt_col > 0.5), 1.0, 0.0).astype(BF16)
            acc_ref[...] += gcol * _mm(put, y.astype(BF16))
        start += rb

    @pl.when(e == N_EXPERTS - 1)
    def _():
        o_ref[...] = _rms(y_ref[...] + acc_ref[...], gain_ref[...])


def _moe_routed(h, y, mg, rt, cnt, w_in, w_out, gain):
    n = h.shape[0]
    tm = MOE_TM
    lower = jnp.asarray(np.tril(np.ones((tm, tm), np.float32), -1), BF16)
    rows = lambda w, **kw: pl.BlockSpec((tm, w), lambda i, e, c: (i, 0), **kw)
    grid_spec = pltpu.PrefetchScalarGridSpec(
        num_scalar_prefetch=1,
        grid=(n // tm, N_EXPERTS),
        in_specs=[rows(D_MODEL), rows(D_MODEL, pipeline_mode=pl.Buffered(1)), rows(LANES), rows(LANES),
                  pl.BlockSpec((tm, tm), lambda i, e, c: (0, 0), pipeline_mode=pl.Buffered(1)),
                  pl.BlockSpec((1, D_MODEL, 2 * EXPERT_FF), lambda i, e, c: (e, 0, 0)),
                  pl.BlockSpec((1, EXPERT_FF, D_MODEL), lambda i, e, c: (e, 0, 0)),
                  pl.BlockSpec((1, D_MODEL), lambda i, e, c: (0, 0))],
        out_specs=rows(D_MODEL),
        scratch_shapes=[pltpu.VMEM((tm, D_MODEL), F32),
                        pltpu.VMEM((tm, LANES), F32),
                        pltpu.VMEM((LANES, tm), F32),
                        pltpu.VMEM((LANES, tm), F32)])
    return pl.pallas_call(
        _moe_routed_kernel,
        grid_spec=grid_spec,
        out_shape=jax.ShapeDtypeStruct((n, D_MODEL), F32),
        compiler_params=_cp(("arbitrary", "arbitrary"), 58),
        name="moe_routed_final_norm",
    )(cnt, h, y, mg, rt, lower, w_in, w_out, gain.reshape(1, D_MODEL))


def _t5_bucket_np(dist):
    n = np.maximum(np.asarray(dist, np.int64), 0)
    exact = N_BUCKETS // 2
    logpart = exact + np.floor(np.log(np.maximum(n, 1) / exact) / math.log(MAX_DISTANCE / exact)
                               * (N_BUCKETS - exact)).astype(np.int64)
    return np.where(n < exact, n, np.minimum(logpart, N_BUCKETS - 1)).astype(np.int32)


def _bias_kernel(rb_ref, idx_ref, o_ref):
    h = pl.program_id(0)
    idx = idx_ref[...]
    far = rb_ref[N_BUCKETS - 1, h]
    acc = jnp.zeros(idx.shape, F32)
    for b in range(N_BUCKETS - 1):
        acc = jnp.where(idx == b, rb_ref[b, h] - far, acc)
    o_ref[0] = acc


def _bias_tables(rel_bias, idx):
    m = idx.shape[0]
    return pl.pallas_call(
        _bias_kernel,
        grid=(N_HEADS,),
        in_specs=[pl.BlockSpec(memory_space=pltpu.SMEM), pl.BlockSpec((m, LANES), lambda h: (0, 0))],
        out_specs=pl.BlockSpec((1, m, LANES), lambda h: (h, 0, 0)),
        out_shape=jax.ShapeDtypeStruct((N_HEADS, m, LANES), F32),
        compiler_params=_cp(("arbitrary",), 32),
        name="t5_bias_tables",
    )(rel_bias, idx)


CMP_NEAR = 32


def _prompt_bias_idx():
    a = np.arange(QT)[:, None]
    b = np.arange(QT)[None, :]
    t0 = _t5_bucket_np(a - b)
    t1 = _t5_bucket_np(QT + a - b)
    m = np.arange(LANES)[None, :]
    gc = np.where(m < CMP_NEAR, _t5_bucket_np(a + (CMP_NEAR // 2) * CMP_STRIDE - (CMP_BLOCK - 1) - CMP_STRIDE * m),
                  N_BUCKETS - 1)
    return np.concatenate([t0, t1, gc], axis=0).astype(np.int32)


def _flatten_pairs(load, nh):
    lane = lax.broadcasted_iota(jnp.int32, (nh, LANES), 1)
    ev, od = [], []
    for m in range(CMP_STRIDE // 2):
        sa = load(2 * m)
        sb = load(2 * m + 1)
        ev.append(jnp.where(lane < HEAD_DIM, sa, pltpu.roll(sb, HEAD_DIM, 1)))
        od.append(jnp.where(lane < HEAD_DIM, pltpu.roll(sa, HEAD_DIM, 1), sb))
    return jnp.concatenate(ev, axis=1), jnp.concatenate(od, axis=1)


def _compress_rows(xf, w1_ref, pe_ref, w2_ref, slot):
    rows = xf.shape[0]
    half = CMP_STRIDE * HEAD_DIM
    first = _mm((xf + pe_ref[slot, 0:1, :half]).astype(BF16), w1_ref[slot, :, :CMP_HIDDEN])
    second = _mm((xf + pe_ref[slot, 0:1, half:]).astype(BF16), w1_ref[slot, :, CMP_HIDDEN:])
    act = jax.nn.gelu(first + pltpu.roll(second, rows - 1, 0))
    return _mm(act.astype(BF16), w2_ref[slot])


def _compress_prompt_kernel(x0, x1, x2, x3, w1_ref, pe_ref, w2_ref, o_ref):
    nh = x0.shape[0] // CMP_STRIDE
    xs = (x0, x1, x2, x3)
    for slot in range(2):
        parts = []
        for c in range(2):
            x_ref = xs[2 * slot + c]
            ev, od = _flatten_pairs(lambda s: x_ref[pl.ds(s, nh, stride=CMP_STRIDE), :], nh)
            parts += [ev, od]
        xf = jnp.concatenate(parts, axis=0)
        o_ref[0, slot] = _compress_rows(xf, w1_ref, pe_ref, w2_ref, slot).astype(o_ref.dtype)


def _compress_weights(phi_pe, phi_w1, phi_w2):
    half = CMP_BLOCK // 2
    w1 = jnp.concatenate([phi_w1[:, :half].reshape(2, half * HEAD_DIM, CMP_HIDDEN),
                          phi_w1[:, half:].reshape(2, half * HEAD_DIM, CMP_HIDDEN)], axis=2)
    pe = jnp.broadcast_to(phi_pe.reshape(2, 1, CMP_BLOCK * HEAD_DIM), (2, 8, CMP_BLOCK * HEAD_DIM))
    return w1.astype(BF16), pe, phi_w2.astype(BF16)


def _compress_prompt(kvf, cw, batch, length):
    w1b, pe, w2b = cw
    nh = length // CMP_STRIDE
    xspec = lambda c: pl.BlockSpec((length, LANES), lambda b: (b, c))
    return pl.pallas_call(
        _compress_prompt_kernel,
        grid=(batch,),
        in_specs=[xspec(0), xspec(1), xspec(2), xspec(3), _resident(w1b.shape), _resident(pe.shape),
                  _resident(w2b.shape)],
        out_specs=pl.BlockSpec((1, 2, N_KV * nh, HEAD_DIM), lambda b: (b, 0, 0, 0)),
        out_shape=jax.ShapeDtypeStruct((batch, 2, N_KV * nh, HEAD_DIM), BF16),
        compiler_params=_cp(("parallel",), 48),
        name="nsa_compress_prompt",
    )(kvf, kvf, kvf, kvf, w1b, pe, w2b)


_NT = (((1,), (1,)), ((), ()))


def _cover_t_np(nc_pad, nb_pad):
    ratio = SEL_BLOCK // CMP_STRIDE
    span = CMP_BLOCK // CMP_STRIDE
    off = (np.arange(ratio)[:, None] - np.arange(span)[None, :]).reshape(-1)
    target = np.arange(nb_pad)[:, None] * ratio + off[None, :]
    cov = np.sum(np.arange(nc_pad)[:, None, None] == target[None], axis=-1)
    return np.ascontiguousarray(cov.T).astype(np.float32)


def _rank_select(score, jidx, nsel):
    nb = score.shape[0]
    cnt = jnp.zeros(score.shape, F32)
    for i in range(nb):
        ri = score[i:i + 1, :]
        cnt = cnt + jnp.where((ri > score) | ((ri == score) & (jidx > i)), 1.0, 0.0)
    return cnt < nsel


def _cmp_attn_kernel(q_ref, kcv_ref, gate_ref, ghi_ref, glo_ref, covt_ref, o_ref, selt_ref, bias_ref):
    i = pl.program_id(1)
    q0 = i * QT
    nc_all = kcv_ref.shape[2] // N_KV
    nb = covt_ref.shape[0]

    def run(nc):
        base = q0 // CMP_STRIDE - CMP_NEAR // 2
        rr = lax.broadcasted_iota(jnp.int32, (LANES, nc), 0)
        cc = lax.broadcasted_iota(jnp.int32, (LANES, nc), 1)
        place = ((cc == base + rr) & (rr < CMP_NEAR)).astype(BF16)
        bias_ref[:, :nc] = (jnp.dot(ghi_ref[...], place, preferred_element_type=F32)
                            + jnp.dot(glo_ref[...], place, preferred_element_type=F32))
        t = q0 + lax.broadcasted_iota(jnp.int32, (QT, nc), 0)
        cend = lax.broadcasted_iota(jnp.int32, (QT, nc), 1) * CMP_STRIDE + (CMP_BLOCK - 1)
        vis = t >= cend
        visf = vis.astype(F32)
        madd = jnp.where(vis, 0.0, NEG)
        gate = gate_ref[...]
        jidx = lax.broadcasted_iota(jnp.int32, (nb, QT), 0)
        tq = q0 + lax.broadcasted_iota(jnp.int32, (nb, QT), 1)
        jt = tq // SEL_BLOCK
        forced = (jidx == 0) | (jidx == jt) | (jidx == jt - 1)
        visb = jidx * SEL_BLOCK <= tq
        for g in range(N_KV):
            kc = kcv_ref[0, 0, g * nc_all:g * nc_all + nc, :]
            vc = kcv_ref[0, 1, g * nc_all:g * nc_all + nc, :]
            psum = jnp.zeros((QT, nc), F32)
            for r in range(GQ):
                h = g * GQ + r
                qh = q_ref[:, h * HEAD_DIM:(h + 1) * HEAD_DIM]
                s = lax.dot_general(qh, kc, _NT, preferred_element_type=F32)
                s = s + bias_ref[h * QT:(h + 1) * QT, :nc] + madd
                m = jnp.max(s, axis=-1, keepdims=True)
                e = jnp.exp(s - m)
                p = ((e * visf) * (1.0 / jnp.sum(e, axis=-1, keepdims=True))).astype(BF16)
                psum = psum + p.astype(F32)
                o = _mm(p, vc)
                o_ref[:, h * HEAD_DIM:(h + 1) * HEAD_DIM] = (o * gate[:, h:h + 1]).astype(o_ref.dtype)
            imp = lax.dot_general(covt_ref[:, :nc], psum, _NT, precision=HIGHEST, preferred_element_type=F32)
            score = jnp.where(visb, jnp.where(forced, FORCE, imp), NEG)
            sel = _rank_select(score, jidx, min(TOP_N, nb)) & visb
            selt_ref[0, g * nb:(g + 1) * nb, :] = sel.astype(F32)

    run(nc_all)


def _cmp_attn(q, kcv, gate, ghi, glo, batch, length):
    nq = length // QT
    nc = kcv.shape[2] // N_KV
    nb = length // SEL_BLOCK
    covt = jnp.asarray(_cover_t_np(nc, nb))
    row = lambda w: pl.BlockSpec((QT, w), lambda b, i: (b * nq + i, 0))
    return pl.pallas_call(
        _cmp_attn_kernel,
        grid=(batch, nq),
        in_specs=[row(Q_COLS), pl.BlockSpec((1, 2, N_KV * nc, HEAD_DIM), lambda b, i: (b, 0, 0, 0)), row(LANES),
                  _resident(ghi.shape), _resident(glo.shape), _resident(covt.shape)],
        out_specs=[row(Q_COLS), pl.BlockSpec((1, N_KV * nb, QT), lambda b, i: (b * nq + i, 0, 0))],
        out_shape=[jax.ShapeDtypeStruct((batch * length, Q_COLS), BF16),
                   jax.ShapeDtypeStruct((batch * nq, N_KV * nb, QT), F32)],
        scratch_shapes=[pltpu.VMEM((N_HEADS * QT, nc), F32)],
        compiler_params=_cp(("parallel", "arbitrary"), 48),
        name="nsa_cmp_attn_select",
    )(q, kcv, gate, ghi, glo, covt)


SEL_CHUNK = 4
SEL_TAIL = SEL_CHUNK + 1
SEL_ZERO, SEL_MASKED = 2, 3


_TN = (((0,), (0,)), ((), ()))


def _fold(x, op):
    return op(x.reshape(x.shape[0] // 8, 8, x.shape[1]), axis=0)


def _selected_t_kernel(q_ref, kv_ref, gate_ref, stab_ref, selt_ref, ext_ref, o_ref, *scratch):
    i = pl.program_id(1)
    length = kv_ref.shape[0]
    nb = ext_ref.shape[1]
    wide = SEL_CHUNK * QT
    tail = SEL_TAIL * QT
    nbulk = jnp.maximum(i - 1, 0) // SEL_CHUNK
    tstart = jnp.minimum(SEL_CHUNK * nbulk, length // QT - SEL_TAIL)
    trows = pl.ds(pl.multiple_of(tstart * QT, QT), tail)
    rep = lambda a: jnp.concatenate([a] * GQ, axis=1)
    gate_t = gate_ref[...].T
    kinds = []
    for jj in range(SEL_TAIL):
        tile = tstart + jj
        kinds.append(jnp.where((tile < SEL_CHUNK * nbulk) | (tile > i), SEL_MASKED, jnp.minimum(i - tile, SEL_ZERO)))

    kcol = lambda g: slice(g * HEAD_DIM, (g + 1) * HEAD_DIM)
    vcol = lambda g: slice((N_KV + g) * HEAD_DIM, (N_KV + g + 1) * HEAD_DIM)
    q4 = lambda g: jnp.concatenate(
        [q_ref[:, (g * GQ + r) * HEAD_DIM:(g * GQ + r + 1) * HEAD_DIM] for r in range(GQ)], axis=0)

    per_set = len(scratch) // 2
    sets = (scratch[:per_set], scratch[per_set:])

    def prep(g):
        s_ref, mx_ref, mb_ref, ls_ref, acc_ref = sets[g % 2]
        mx_ref[...] = jnp.full(mx_ref.shape, -jnp.inf, F32)
        ls_ref[...] = jnp.zeros(ls_ref.shape, F32)
        acc_ref[...] = jnp.zeros(acc_ref.shape, F32)

    def block_mask(g, rows):
        selt = selt_ref[0, g * nb:(g + 1) * nb, :].astype(BF16)
        return (jnp.dot(ext_ref[rows, :], selt, preferred_element_type=F32) - 1.0) * (-NEG)

    def logits_chunk(g, c):
        s_ref, mx_ref, mb_ref, ls_ref, acc_ref = sets[g % 2]
        rows = pl.ds(pl.multiple_of(c * wide, wide), wide)
        st = (lax.dot_general(kv_ref[rows, kcol(g)], q4(g), _NT, preferred_element_type=F32)
              + rep(block_mask(g, rows)))
        s_ref[rows, :] = st
        mx_ref[...] = jnp.maximum(mx_ref[...], _fold(st, jnp.max))

    def logits_tail(g):
        s_ref, mx_ref, mb_ref, ls_ref, acc_ref = sets[g % 2]
        st = lax.dot_general(kv_ref[trows, kcol(g)], q4(g), _NT, preferred_element_type=F32)
        adds = [jnp.concatenate([stab_ref[g * GQ + r, pl.ds(pl.multiple_of(kinds[jj] * QT, QT), QT), :]
                                 for r in range(GQ)], axis=1) for jj in range(SEL_TAIL)]
        st = st + jnp.concatenate(adds, axis=0) + rep(block_mask(g, trows))
        s_ref[length:, :] = st
        mx = jnp.maximum(mx_ref[...], _fold(st, jnp.max))
        mb_ref[...] = jnp.broadcast_to(jnp.max(mx, axis=0, keepdims=True), mx.shape)

    def probs(st, m8):
        return jnp.exp(st.reshape(st.shape[0] // 8, 8, st.shape[1]) - m8[None]).reshape(st.shape)

    def pv_chunk(g, c):
        s_ref, mx_ref, mb_ref, ls_ref, acc_ref = sets[g % 2]
        rows = pl.ds(pl.multiple_of(c * wide, wide), wide)
        p = probs(s_ref[rows, :], mb_ref[...])
        ls_ref[...] += _fold(p, jnp.sum)
        acc_ref[...] += lax.dot_general(kv_ref[rows, vcol(g)], p.astype(BF16), _TN, preferred_element_type=F32)

    def pv_tail(g):
        s_ref, mx_ref, mb_ref, ls_ref, acc_ref = sets[g % 2]
        p = probs(s_ref[length:, :], mb_ref[...])
        l = jnp.sum(ls_ref[...] + _fold(p, jnp.sum), axis=0, keepdims=True)
        acc = acc_ref[...] + lax.dot_general(kv_ref[trows, vcol(g)], p.astype(BF16), _TN,
                                             preferred_element_type=F32)
        first = N_HEADS + g * GQ
        gates = jnp.concatenate([gate_t[first + r:first + r + 1, :] for r in range(GQ)], axis=1)
        o4 = (acc * (gates / l)).T
        for r in range(GQ):
            h = g * GQ + r
            o_ref[:, h * HEAD_DIM:(h + 1) * HEAD_DIM] = o4[r * QT:(r + 1) * QT, :].astype(o_ref.dtype)

    def loop(*stages):
        def body(c, carry):
            for stage, g in stages:
                stage(g, c)
            return carry
        lax.fori_loop(0, nbulk, body, 0)

    prep(0)
    loop((logits_chunk, 0))
    logits_tail(0)
    for g in range(1, N_KV):
        prep(g)
        loop((logits_chunk, g), (pv_chunk, g - 1))
        pv_tail(g - 1)
        logits_tail(g)
    loop((pv_chunk, N_KV - 1))
    pv_tail(N_KV - 1)


def _selected_t_attn(q, kvb, gate, stab, selt, batch, length):
    nq = length // QT
    nb = length // SEL_BLOCK
    assert nq >= SEL_TAIL
    row = lambda w: pl.BlockSpec((QT, w), lambda b, i: (b * nq + i, 0))
    ext = jnp.asarray(_ex_np(nb, length).T, BF16)
    stab_t = stab.reshape(N_HEADS, QT, -1, QT).transpose(0, 2, 3, 1).reshape(N_HEADS, -1, QT)
    lanes4 = GQ * QT
    return pl.pallas_call(
        _selected_t_kernel,
        grid=(batch, nq),
        in_specs=[row(Q_COLS), pl.BlockSpec((length, 2 * N_KV * HEAD_DIM), lambda b, i: (b, 0)), row(LANES),
                  _resident(stab_t.shape), pl.BlockSpec((1, N_KV * nb, QT), lambda b, i: (b * nq + i, 0, 0)),
                  _resident(ext.shape)],
        out_specs=row(Q_COLS),
        out_shape=jax.ShapeDtypeStruct((batch * length, Q_COLS), BF16),
        scratch_shapes=2 * [pltpu.VMEM((length + SEL_TAIL * QT, lanes4), F32),
                            pltpu.VMEM((8, lanes4), F32),
                            pltpu.VMEM((8, lanes4), F32),
                            pltpu.VMEM((8, lanes4), F32),
                            pltpu.VMEM((HEAD_DIM, lanes4), F32)],
        compiler_params=_cp(("parallel", "arbitrary"), 56),
        name="nsa_selected_attn",
    )(q, kvb, gate, stab_t, selt, ext)


WIN_TILES = WINDOW // QT + 1


def _window_kernel(q_ref, kv_ref, gate_ref, wtab_ref, o_ref):
    i = pl.program_id(1)
    first = jnp.maximum(i - (WIN_TILES - 1), 0)
    rows = pl.ds(pl.multiple_of(first * QT, QT), WIN_TILES * QT)
    gate_t = gate_ref[...].T
    for g in range(N_KV):
        kcol = slice(g * HEAD_DIM, (g + 1) * HEAD_DIM)
        vcol = slice(N_KV * HEAD_DIM + g * HEAD_DIM, N_KV * HEAD_DIM + (g + 1) * HEAD_DIM)
        q4 = jnp.concatenate([q_ref[:, (g * GQ + r) * HEAD_DIM:(g * GQ + r + 1) * HEAD_DIM] for r in range(GQ)],
                             axis=0)
        st = lax.dot_general(kv_ref[rows, kcol], q4, _NT, preferred_element_type=F32)
        adds = []
        for jj in range(WIN_TILES):
            d = i - first - jj
            kind = jnp.where(d < 0, WIN_TILES, d)
            krows = pl.ds(pl.multiple_of(kind * QT, QT), QT)
            adds.append(jnp.concatenate([wtab_ref[g * GQ + r, krows, :] for r in range(GQ)], axis=1))
        st = st + jnp.concatenate(adds, axis=0)
        m = jnp.max(_fold(st, jnp.max), axis=0, keepdims=True)
        m8 = jnp.broadcast_to(m, (8, m.shape[1]))
        p = jnp.exp(st.reshape(st.shape[0] // 8, 8, st.shape[1]) - m8[None]).reshape(st.shape)
        l = jnp.sum(_fold(p, jnp.sum), axis=0, keepdims=True)
        acc = lax.dot_general(kv_ref[rows, vcol], p.astype(BF16), _TN, preferred_element_type=F32)
        first_gate = 2 * N_HEADS + g * GQ
        gates = jnp.concatenate([gate_t[first_gate + r:first_gate + r + 1, :] for r in range(GQ)], axis=1)
        o4 = (acc * (gates / l)).T
        for r in range(GQ):
            h = g * GQ + r
            o_ref[:, h * HEAD_DIM:(h + 1) * HEAD_DIM] = o4[r * QT:(r + 1) * QT, :].astype(o_ref.dtype)


def _window_tables(tab):
    a = np.arange(QT)[:, None]
    b = np.arange(QT)[None, :]
    causal = np.where(a >= b, 0.0, NEG).astype(np.float32)
    edge = np.where(a <= b, 0.0, NEG).astype(np.float32)
    const = lambda m: jnp.broadcast_to(jnp.asarray(m, F32), (N_HEADS, QT, QT))
    zero = np.zeros((QT, QT), np.float32)
    return jnp.concatenate([tab[:, :QT] + causal, tab[:, QT:], const(zero), const(zero), const(edge),
                            const(np.full((QT, QT), NEG, np.float32))], axis=2)


def _window_attn(q, kvb, gate, wtab, batch, length):
    nq = length // QT
    row = lambda w: pl.BlockSpec((QT, w), lambda b, i: (b * nq + i, 0))
    wtab = wtab.reshape(N_HEADS, QT, -1, QT).transpose(0, 2, 3, 1).reshape(N_HEADS, -1, QT)
    return pl.pallas_call(
        _window_kernel,
        grid=(batch, nq),
        in_specs=[row(Q_COLS), pl.BlockSpec((length, 2 * N_KV * HEAD_DIM), lambda b, i: (b, 1)), row(LANES),
                  _resident(wtab.shape)],
        out_specs=row(Q_COLS),
        out_shape=jax.ShapeDtypeStruct((batch * length, Q_COLS), BF16),
        compiler_params=_cp(("parallel", "arbitrary"), 48),
        name="nsa_window_attn",
    )(q, kvb, gate, wtab)


def _ex_np(nb, length):
    return (np.arange(nb)[:, None] == (np.arange(length)[None, :] // SEL_BLOCK)).astype(np.float32)


def _selected_tables(tab):
    a = np.arange(QT)[:, None]
    b = np.arange(QT)[None, :]
    causal = np.where(a >= b, 0.0, NEG).astype(np.float32)
    const = lambda v: jnp.full((N_HEADS, QT, QT), v, F32)
    return jnp.concatenate([tab[:, :QT] + causal, tab[:, QT:], const(0.0), const(NEG)], axis=2)


def _s5_sample_kernel(u_ref, hr_ref, hi_ref, ws_ref, wv_ref, a1_ref, d_ref, g_ref, sre_ref, sim_ref):
    u = u_ref[...]
    x = _mm(u.astype(BF16), ws_ref[0, 0])
    ar, ai = a1_ref[0, :, :S5_SL], a1_ref[0, :, S5_SL:]
    h0r, h0i = hr_ref[...], hi_ref[...]
    hr = x[:, :S5_SL] + (ar * h0r - ai * h0i)
    hi = x[:, S5_SL:] + (ar * h0i + ai * h0r)
    sre_ref[...] = hr
    sim_ref[...] = hi
    y = _mm(jnp.concatenate([hr, hi], axis=1).astype(BF16), wv_ref[0, 0])
    g_ref[...] = jax.nn.gelu(y + d_ref[...] * u).astype(g_ref.dtype)


def _s5_sample(u, h0_re, h0_im, ws, wv, a1, d_skip):
    n = u.shape[0]
    st = pl.BlockSpec((n, S5_SL), lambda s: (0, s))
    return pl.pallas_call(
        _s5_sample_kernel,
        grid=(S5_NSG,),
        in_specs=[pl.BlockSpec((n, LANES), lambda s: (0, s)), st, st,
                  pl.BlockSpec((1, 1, LANES, 2 * S5_SL), lambda s: (S5_T - 1, s, 0, 0)),
                  pl.BlockSpec((1, 1, 2 * S5_SL, LANES), lambda s: (0, s, 0, 0)),
                  pl.BlockSpec((1, 1, 2 * S5_SL), lambda s: (s, 0, 0)),
                  pl.BlockSpec((1, LANES), lambda s: (0, s))],
        out_specs=[pl.BlockSpec((n, LANES), lambda s: (0, s)), st, st],
        out_shape=[jax.ShapeDtypeStruct((n, D_MODEL), BF16),
                   jax.ShapeDtypeStruct((n, N_GROUPS * STATE), F32),
                   jax.ShapeDtypeStruct((n, N_GROUPS * STATE), F32)],
        compiler_params=_cp(("parallel",), 32),
        name="s5_sample",
    )(u, h0_re, h0_im, ws, wv, a1, d_skip.reshape(1, D_MODEL))


def _softmax_lanes(s, valid):
    m = jnp.max(s, axis=-1, keepdims=True)
    e = jnp.exp(s - m)
    return (e * valid) * (1.0 / jnp.sum(e, axis=-1, keepdims=True))


def _own_group(o):
    rowg = lax.broadcasted_iota(jnp.int32, (N_HEADS, HEAD_DIM), 0) // GQ
    out = jnp.zeros((N_HEADS, HEAD_DIM), F32)
    for g in range(N_KV):
        out = out + jnp.where(rowg == g, o[:, g * HEAD_DIM:(g + 1) * HEAD_DIM], 0.0)
    return out


DECODE_SEQS_PER_STEP = 1


def _nsa_decode_kernel(pt_ref, *refs, past_len, n_cmp, npages):
    nseq = DECODE_SEQS_PER_STEP
    x_refs = refs[:nseq * npages]
    (win_ref, qbd_ref, new_ref, gate_ref, w1_ref, pe_ref, w2_ref, cov_ref, bias_ref, ex_ref, o_ref,
     xt_ref, xf_ref, ks_ref, vs_ref, kw_ref, vw_ref) = refs[nseq * npages:]
    page = x_refs[0].shape[2]
    hp = page // CMP_STRIDE
    nh = past_len // CMP_STRIDE
    nkv = N_KV * HEAD_DIM
    sel_cols = ks_ref.shape[2]
    win_cols = kw_ref.shape[2]
    wb = win_ref.shape[2]

    def stash(ref, sq, cols, x):
        ref[sq, :, cols] = x.astype(BF16)

    for sq in range(nseq):
        for p in range(npages):
            x_ref = x_refs[sq * npages + p]
            xt = x_ref[0, :2 * nkv, :].T
            for c in range(2 * nkv // LANES):
                xt_ref[sq, p, c] = xt[:, c * LANES:(c + 1) * LANES]
            for slot in range(2):
                for c in range(2):
                    blk = 2 * slot + c
                    ev, od = _flatten_pairs(lambda s: xt_ref[sq, p, blk, pl.ds(s, hp, stride=CMP_STRIDE), :], hp)
                    for head, val in ((2 * c, ev), (2 * c + 1, od)):
                        xf_ref[sq, slot, head * nh + p * hp:head * nh + (p + 1) * hp, :] = val
            cols = slice(p * page, (p + 1) * page)
            stash(ks_ref, sq, cols, x_ref[0, 2 * nkv:3 * nkv, :])
            stash(vs_ref, sq, cols, x_ref[0, 3 * nkv:, :])

    for sq in range(nseq):
        qbd = qbd_ref[sq]
        new = new_ref[sq]
        lane0 = lax.broadcasted_iota(jnp.int32, (nkv, LANES), 1) == 0
        newcol = lambda k: jnp.where(lane0, jnp.broadcast_to(new[:, k:k + 1], (nkv, LANES)), 0.0)
        stash(ks_ref, sq, slice(past_len, sel_cols), newcol(0))
        stash(vs_ref, sq, slice(past_len, sel_cols), newcol(1))
        stash(kw_ref, sq, slice(0, wb), win_ref[sq, :nkv, :])
        stash(vw_ref, sq, slice(0, wb), win_ref[sq, nkv:, :])
        stash(kw_ref, sq, slice(wb, win_cols), newcol(2))
        stash(vw_ref, sq, slice(wb, win_cols), newcol(3))
        rowh = lax.broadcasted_iota(jnp.int32, (N_HEADS, LANES), 0)

        kc = _compress_rows(xf_ref[sq, 0], w1_ref, pe_ref, w2_ref, 0).astype(BF16)
        vc = _compress_rows(xf_ref[sq, 1], w1_ref, pe_ref, w2_ref, 1).astype(BF16)
        s = jnp.zeros((N_HEADS, nh), F32)
        for g in range(N_KV):
            s = s + lax.dot_general(qbd[:, g * HEAD_DIM:(g + 1) * HEAD_DIM], kc[g * nh:(g + 1) * nh], _NT,
                                    preferred_element_type=F32)
        nidx = lax.broadcasted_iota(jnp.int32, (N_HEADS, nh), 1)
        s = s + bias_ref[:, :nh] + jnp.where(nidx < n_cmp, 0.0, NEG)
        pc = _softmax_lanes(s, (nidx < n_cmp).astype(F32)).astype(BF16)
        o_c = _own_group(jnp.concatenate([_mm(pc, vc[g * nh:(g + 1) * nh]) for g in range(N_KV)], axis=1))

        pcf = pc.astype(F32)
        psum = jnp.concatenate([jnp.sum(pcf[g * GQ:(g + 1) * GQ], axis=0, keepdims=True) for g in range(N_KV)]
                               + [jnp.zeros((8 - N_KV, nh), F32)], axis=0)
        imp = jnp.dot(psum, cov_ref[...], precision=HIGHEST, preferred_element_type=F32)
        jt = past_len // SEL_BLOCK
        nb = jt + 1
        lane = lax.broadcasted_iota(jnp.int32, (LANES, LANES), 1)
        subl = lax.broadcasted_iota(jnp.int32, (LANES, LANES), 0)

        def scored(x, j):
            forced = (j == 0) | (j == jt) | (j == jt - 1)
            return jnp.where(j < nb, jnp.where(forced, FORCE, x), NEG)

        imp_sq = jnp.concatenate([imp, jnp.zeros((LANES - 8, LANES), F32)], axis=0)
        sc_row = scored(imp_sq, lane)
        sc_col = scored(imp_sq.T, subl)
        selh = jnp.zeros((N_HEADS, LANES), F32)
        for g in range(N_KV):
            col = sc_col[:, g:g + 1]
            row = sc_row[g:g + 1, :]
            ahead = (col > row) | ((col == row) & (subl < lane))
            rank = jnp.sum(jnp.where(ahead, 1.0, 0.0), axis=0, keepdims=True)
            chosen = jnp.where((rank < min(TOP_N, nb)) & (lane[:1] < nb), 1.0, 0.0)
            selh = selh + jnp.where(rowh // GQ == g, chosen, 0.0)

        pos = lax.broadcasted_iota(jnp.int32, (N_HEADS, sel_cols), 1)
        inblk = jnp.dot(selh.astype(BF16), ex_ref[...], preferred_element_type=F32)
        ok_s = (inblk > 0.5) & (pos <= past_len)
        s = _mm(qbd, ks_ref[sq]) + bias_ref[:, nh:nh + sel_cols]
        ps = _softmax_lanes(jnp.where(ok_s, s, NEG), ok_s.astype(F32)).astype(BF16)
        o_s = _own_group(lax.dot_general(ps, vs_ref[sq], _NT, preferred_element_type=F32))

        posw = lax.broadcasted_iota(jnp.int32, (N_HEADS, win_cols), 1)
        ok_w = posw <= wb
        s = _mm(qbd, kw_ref[sq]) + bias_ref[:, nh + sel_cols:nh + sel_cols + win_cols]
        pw = _softmax_lanes(jnp.where(ok_w, s, NEG), ok_w.astype(F32)).astype(BF16)
        o_w = _own_group(lax.dot_general(pw, vw_ref[sq], _NT, preferred_element_type=F32))

        gate = gate_ref[sq]
        o_ref[sq] = gate[:, 0:1] * o_c + gate[:, 1:2] * o_s + gate[:, 2:3] * o_w


def _decode_bias_idx(past_len, wb, ncp, sel_cols, win_cols):
    n = np.arange(ncp)
    cmp_d = past_len - (n * CMP_STRIDE + CMP_BLOCK - 1)
    sel_d = past_len - np.arange(sel_cols)
    win_d = np.concatenate([wb - np.arange(wb), np.zeros(win_cols - wb, np.int64)])
    idx = _t5_bucket_np(np.concatenate([cmp_d, sel_d, win_d]))
    pad = -len(idx) % (8 * LANES)
    return np.concatenate([idx, np.full(pad, N_BUCKETS - 1)]).reshape(-1, LANES).astype(np.int32)


def _nsa_decode(cache_kv, cache_win, page_table, qbd, new_cols, gate3, cw, rel_bias):
    n_pool, page = cache_kv.shape[:2]
    nseq, npages = page_table.shape
    past_len = npages * page
    wb = cache_win.shape[1]
    w1b, pe, w2b = cw
    nh = past_len // CMP_STRIDE
    n_cmp = (past_len + 1 - CMP_BLOCK) // CMP_STRIDE + 1
    sel_cols = past_len + LANES
    win_cols = wb + LANES
    nkv = N_KV * HEAD_DIM
    kvt = cache_kv.transpose(0, 2, 3, 4, 1).reshape(n_pool, 4 * nkv, page)
    wint = cache_win.transpose(0, 2, 3, 4, 1).reshape(nseq, 2 * nkv, wb)
    cov = jnp.asarray(_cover_t_np(LANES, LANES).T)
    idx = _decode_bias_idx(past_len, wb, nh, sel_cols, win_cols)
    bias = _bias_tables(rel_bias, jnp.asarray(idx)).reshape(N_HEADS, idx.size)
    ex = jnp.asarray((np.arange(LANES)[:, None] == np.arange(sel_cols)[None, :] // SEL_BLOCK).astype(np.float32), BF16)
    const = lambda a: pl.BlockSpec(a.shape, lambda b, pt: (0,) * a.ndim)
    sps = DECODE_SEQS_PER_STEP
    assert nseq % sps == 0
    page_spec = lambda sq, p: pl.BlockSpec((1, 4 * nkv, page), lambda b, pt: (pt[b * sps + sq, p], 0, 0))
    per_seq = lambda *dims: pl.BlockSpec((sps,) + dims, lambda b, pt: (b,) + (0,) * len(dims))
    grid_spec = pltpu.PrefetchScalarGridSpec(
        num_scalar_prefetch=1,
        grid=(nseq // sps,),
        in_specs=[page_spec(sq, p) for sq in range(sps) for p in range(npages)] + [
                  per_seq(2 * nkv, wb), per_seq(N_HEADS, nkv), per_seq(nkv, LANES), per_seq(N_HEADS, LANES),
                  const(w1b), const(pe), const(w2b), const(cov), const(bias), const(ex)],
        out_specs=per_seq(N_HEADS, HEAD_DIM),
        scratch_shapes=[pltpu.VMEM((sps, npages, 2 * nkv // LANES, page, LANES), F32),
                        pltpu.VMEM((sps, 2, N_KV * nh, CMP_STRIDE * HEAD_DIM), F32),
                        pltpu.VMEM((sps, nkv, sel_cols), BF16), pltpu.VMEM((sps, nkv, sel_cols), BF16),
                        pltpu.VMEM((sps, nkv, win_cols), BF16), pltpu.VMEM((sps, nkv, win_cols), BF16)])
    out = pl.pallas_call(
        functools.partial(_nsa_decode_kernel, past_len=past_len, n_cmp=n_cmp, npages=npages),
        grid_spec=grid_spec,
        out_shape=jax.ShapeDtypeStruct((nseq, N_HEADS, HEAD_DIM), F32),
        compiler_params=_cp(("arbitrary",), 58),
        name="nsa_decode",
    )(page_table, *([kvt] * (sps * npages)), wint, qbd, new_cols, gate3, w1b, pe, w2b, cov, bias, ex)
    return out.reshape(nseq, Q_COLS)


def _decode_operands(qs, kvbs, gates):
    nseq = qs.shape[0]
    nkv = N_KV * HEAD_DIM
    own = (np.arange(N_HEADS)[:, None] // GQ) == (np.arange(nkv)[None, :] // HEAD_DIM)
    qbd = jnp.where(own[None], jnp.tile(qs.reshape(nseq, N_HEADS, HEAD_DIM), (1, 1, N_KV)), 0.0)
    new_cols = jnp.pad(kvbs.astype(F32).reshape(nseq, 4, nkv).transpose(0, 2, 1), ((0, 0), (0, 0), (0, LANES - 4)))
    gate3 = jnp.pad(gates[:, :GATE_COLS].reshape(nseq, 3, N_HEADS).transpose(0, 2, 1),
                    ((0, 0), (0, 0), (0, LANES - 3)))
    return qbd, new_cols, gate3


PROMPT_TM = 512


def kernel(x_prompt, x_sample, state_s5_re, state_s5_im, cache_kv, cache_win, page_table, rel_bias, norm_mix, norm_ffn, norm_final, s5_lam_re, s5_lam_im, s5_log_dt, s5_b_re, s5_b_im, s5_c_re, s5_c_im, s5_d, s5_w_glu, ffn_w_in, ffn_w_out, nsa_w_in, nsa_phi_pe, nsa_phi_w1, nsa_phi_w2, nsa_w_o, moe_router, moe_w_in, moe_w_out):
    batch, length, _ = x_prompt.shape
    nseq = x_sample.shape[0]
    xp = x_prompt.reshape(batch * length, D_MODEL)
    xs = x_sample.reshape(nseq, D_MODEL)
    nstate = N_GROUPS * STATE

    wk, ws, wv, apow, a1 = _s5_prep(s5_lam_re, s5_lam_im, s5_log_dt, s5_b_re, s5_b_im, s5_c_re, s5_c_im)
    w_glu = s5_w_glu.astype(BF16)
    ffn_in, ffn_out = ffn_w_in.astype(BF16), ffn_w_out.astype(BF16)
    nsa_in = jnp.pad(nsa_w_in, ((0, 0), (0, NSA_PAD - nsa_w_in.shape[1]))).astype(BF16)
    w_o = nsa_w_o.astype(BF16)
    router = jnp.pad(moe_router, ((0, 0), (0, LANES - N_EXPERTS))).astype(BF16)
    moe_in, moe_out = moe_w_in.astype(BF16), moe_w_out.astype(BF16)
    cw = _compress_weights(nsa_phi_pe, nsa_phi_w1, nsa_phi_w2)
    tabs = _bias_tables(rel_bias, jnp.asarray(_prompt_bias_idx()))
    tab = tabs[:, :2 * QT]
    gflat = tabs[:, 2 * QT:].reshape(N_HEADS * QT, LANES)
    ghi = gflat.astype(BF16)
    glo = (gflat - ghi.astype(F32)).astype(BF16)

    up = _rmsnorm(xp, norm_mix[0], PROMPT_TM)
    gp, sre_p, sim_p = _s5_prompt(up, s5_d, wk, ws, wv, apow, batch, length)
    y1p, h1p = _glu(gp, xp, w_glu, norm_ffn[0], PROMPT_TM)
    y2p, h2p = _ffn(h1p, y1p, ffn_in, ffn_out, norm_mix[1], PROMPT_TM)
    q, kvf, winf, kvb, gate = _nsa_proj(h2p, nsa_in, PROMPT_TM)
    kcv = _compress_prompt(kvf, cw, batch, length)
    oc, selt = _cmp_attn(q, kcv, gate, ghi, glo, batch, length)
    os_ = _selected_t_attn(q, kvb, gate, _selected_tables(tab), selt, batch, length)
    ow = _window_attn(q, kvb, gate, _window_tables(tab), batch, length)
    y3p, h3p, mgp, rtp, cntp = _merge(oc, os_, ow, y2p, w_o, norm_ffn[1], router, MOE_TM)
    counts = cntp[:, 0, :N_EXPERTS].astype(jnp.int32).reshape(-1)
    yp = _moe_routed(h3p, y3p, mgp, rtp, counts, moe_in, moe_out, norm_final)

    us = _rmsnorm(xs, norm_mix[0], nseq)
    gs, sre_s, sim_s = _s5_sample(us, state_s5_re.reshape(nseq, nstate), state_s5_im.reshape(nseq, nstate),
                                  ws, wv, a1, s5_d)
    y1s, h1s = _glu(gs, xs, w_glu, norm_ffn[0], nseq)
    y2s, h2s = _ffn(h1s, y1s, ffn_in, ffn_out, norm_mix[1], nseq)
    qs, kvfs, winfs, kvbs, gates = _nsa_proj(h2s, nsa_in, nseq)
    qbd, new_cols, gate3 = _decode_operands(qs, kvbs, gates)
    od = _nsa_decode(cache_kv, cache_win, page_table, qbd, new_cols, gate3, cw, rel_bias)
    zero = jnp.zeros((nseq, D_MODEL), F32)
    y3s, h3s, mgs, _, _ = _merge(od, zero, zero, y2s, w_o, norm_ffn[1], router, nseq)
    ys = _moe(h3s, y3s, mgs, moe_in, moe_out, norm_final, nseq)

    wp = min(WINDOW, length)
    return (yp.reshape(batch, length, D_MODEL),
            ys.reshape(nseq, 1, D_MODEL),
            sre_p.reshape(batch, N_GROUPS, STATE), sim_p.reshape(batch, N_GROUPS, STATE),
            kvf.reshape(batch, length, 4, N_KV, HEAD_DIM),
            winf.reshape(batch, length, 2, N_KV, HEAD_DIM)[:, length - wp:],
            sre_s.reshape(nseq, N_GROUPS, STATE), sim_s.reshape(nseq, N_GROUPS, STATE),
            kvfs.reshape(nseq, 1, 4, N_KV, HEAD_DIM),
            winfs.reshape(nseq, 1, 2, N_KV, HEAD_DIM))
```
